```python
import math
import jax, jax.numpy as jnp
from jax import lax
import numpy as np

D_MODEL = 2048
BATCH = 4
SEQ = 2048
DEPTH = 2
DEC_BATCH = 32
DEC_SEQ = 8
PAST_LEN = 8192
PAGE_SIZE = 128

N_ATT_LAYERS = (DEPTH + 1) // 2
N_SSM_LAYERS = DEPTH // 2
SC_CH = D_MODEL // 2
SC_WIDTH = 3
ATT_HEADS = 8
ATT_HEAD_DIM = 128
ATT_DIM = ATT_HEADS * ATT_HEAD_DIM
Q_BLOCK = 128
IN_EVEN = 3 * SC_CH + 3 * ATT_DIM + ATT_HEADS
FORGET_BIAS = 3.0
FORGET_CACHE_BIAS = 8.0
SSM_INNER = 2 * D_MODEL
SSM_HEAD_DIM = 64
SSM_HEADS = SSM_INNER // SSM_HEAD_DIM
SSM_GROUPS = 8
D_STATE = 128
SSM_CONV = 4
SSM_CHUNK = 128
CONV_DIM = SSM_INNER + 2 * SSM_GROUPS * D_STATE
IN_SSM = SSM_INNER + CONV_DIM + SSM_HEADS
DT_MIN = 0.001
DT_MAX = 0.1
D_FF = ((8 * D_MODEL // 3 + 255) // 256) * 256
EPS = 1e-6

kernel_name = 'hybrid_conv_fox_ssd_decoder_step'


def rmsnorm(x, w):
    xf = x.astype(jnp.float32)
    y = xf * lax.rsqrt(jnp.mean(xf * xf, axis=-1, keepdims=True) + EPS)
    return (y * w.astype(jnp.float32)).astype(x.dtype)


def swiglu(h, w_gate, w_up, w_down):
    return (jax.nn.silu(h @ w_gate) * (h @ w_up)) @ w_down


def causal_dwconv(x_ext, w):
    K = w.shape[0]
    T = x_ext.shape[1] - K + 1
    return sum(x_ext[:, kk:kk + T] * w[kk] for kk in range(K))


def even_in_proj(h, w_in, b_f):
    b, T, _ = h.shape
    p = h @ w_in
    cuts = [SC_CH, 2 * SC_CH, 3 * SC_CH, 3 * SC_CH + ATT_DIM, 3 * SC_CH + 2 * ATT_DIM, 3 * SC_CH + 3 * ATT_DIM]
    gb, gc, u, q, k, v, f = jnp.split(p, cuts, axis=-1)
    shp = (b, T, ATT_HEADS, ATT_HEAD_DIM)
    logf = jax.nn.log_sigmoid(f.astype(jnp.float32) + b_f.astype(jnp.float32))
    return gb, gc, u, q.reshape(shp), k.reshape(shp), v.reshape(shp), logf


def short_conv(gb, gc, u, hist, w):
    z = gc * u
    ext = jnp.concatenate([hist.astype(z.dtype), z], axis=1)
    return gb * causal_dwconv(ext, w), ext[:, -(SC_WIDTH - 1):]


def fox_prompt(q, k, v, logf):
    b, S, H, dh = q.shape
    nb = S // Q_BLOCK
    scale = dh ** -0.5
    F = jnp.cumsum(logf, axis=1)
    kf = k.astype(jnp.float32)
    vf = v.astype(jnp.float32)
    Fk = jnp.transpose(F, (0, 2, 1))[:, :, None, :]
    kpos = jnp.arange(S)
    qb = q.reshape(b, nb, Q_BLOCK, H, dh).transpose(1, 0, 2, 3, 4)
    Fq = F.reshape(b, nb, Q_BLOCK, H).transpose(1, 0, 3, 2)

    def block(args):
        qi, Fi, bi = args
        s = jnp.einsum('bqhd,bkhd->bhqk', qi.astype(jnp.float32) * scale, kf) + (Fi[..., None] - Fk)
        qpos = bi * Q_BLOCK + jnp.arange(Q_BLOCK)
        s = jnp.where(kpos[None, :] <= qpos[:, None], s, -jnp.inf)
        p = jax.nn.softmax(s, axis=-1)
        return jnp.einsum('bhqk,bkhd->bqhd', p, vf)

    o = lax.map(block, (qb, Fq, jnp.arange(nb)))
    return o.transpose(1, 0, 2, 3, 4).reshape(b, S, H * dh).astype(q.dtype)


def fox_sample(q, k, v, logf, k_pool, v_pool, lf_pool, page_table):
    b, T, H, dh = q.shape
    scale = dh ** -0.5
    kp = k_pool[page_table].reshape(b, -1, H, dh).astype(jnp.float32)
    vp = v_pool[page_table].reshape(b, -1, H, dh).astype(jnp.float32)
    lfp = lf_pool[page_table].reshape(b, -1, H).astype(jnp.float32)
    P = kp.shape[1]
    R = lax.cumsum(lfp, axis=1, reverse=True) - lfp
    Fn = jnp.cumsum(logf, axis=1)
    FnT = jnp.transpose(Fn, (0, 2, 1))
    RT = jnp.transpose(R, (0, 2, 1))
    qf = q.astype(jnp.float32) * scale
    s_past = jnp.einsum('bqhd,bkhd->bhqk', qf, kp) + FnT[..., None] + RT[:, :, None, :]
    s_new = jnp.einsum('bqhd,bkhd->bhqk', qf, k.astype(jnp.float32)) + FnT[..., None] - FnT[:, :, None, :]
    tri = jnp.tril(jnp.ones((T, T), bool))
    s_new = jnp.where(tri, s_new, -jnp.inf)
    p = jax.nn.softmax(jnp.concatenate([s_past, s_new], axis=-1), axis=-1)
    o = jnp.einsum('bhqk,bkhd->bqhd', p[..., :P], vp) + jnp.einsum('bhqk,bkhd->bqhd', p[..., P:], v.astype(jnp.float32))
    return o.reshape(b, T, H * dh).astype(q.dtype)


def ssd(x, dt, A, Bm, Cm, init_state):
    b, T, Hs, P = x.shape
    G, N = Bm.shape[2], Bm.shape[3]
    R = Hs // G
    L = min(SSM_CHUNK, T)
    pad = (-T) % L
    if pad:
        padw = lambda a: jnp.pad(a, [(0, 0), (0, pad)] + [(0, 0)] * (a.ndim - 2))
        x, dt, Bm, Cm = padw(x), padw(dt), padw(Bm), padw(Cm)
    c = (T + pad) // L
    xf = x.astype(jnp.float32).reshape(b, c, L, G, R, P)
    dtc = dt.reshape(b, c, L, G, R)
    acum = jnp.cumsum(dtc * A.reshape(G, R), axis=2)
    Bc = Bm.astype(jnp.float32).reshape(b, c, L, G, N)
    Cc = Cm.astype(jnp.float32).reshape(b, c, L, G, N)
    xdt = xf * dtc[..., None]
    tri = jnp.tril(jnp.ones((L, L), bool))[:, :, None, None]
    seg = acum[:, :, :, None] - acum[:, :, None, :]
    decay = jnp.exp(jnp.where(tri, seg, -jnp.inf))
    cb = jnp.einsum('bclgn,bcsgn->bclsg', Cc, Bc)
    y_diag = jnp.einsum('bclsgr,bcsgrp->bclgrp', cb[..., None] * decay, xdt)
    dstates = jnp.exp(acum[:, :, -1:] - acum)
    states = jnp.einsum('bclgn,bclgrp->bcgrpn', Bc, xdt * dstates[..., None])
    chunk_decay = jnp.exp(acum[:, :, -1])

    def step(h, inp):
        st, dec = inp
        return h * dec[..., None, None] + st, h

    h0 = init_state.astype(jnp.float32).reshape(b, G, R, P, N)
    hT, h_in = lax.scan(step, h0, (jnp.moveaxis(states, 1, 0), jnp.moveaxis(chunk_decay, 1, 0)))
    h_in = jnp.moveaxis(h_in, 0, 1)
    y_off = jnp.einsum('bclgn,bcgrpn->bclgrp', Cc, h_in) * jnp.exp(acum)[..., None]
    y = (y_diag + y_off).reshape(b, c * L, Hs, P)[:, :T]
    return y, hT.reshape(b, Hs, P, N)


def ssm_mixer(h, conv_hist, ssm_state, w_in, conv_w, conv_b, dt_b, a_lg, d_sk, gn_w, w_out):
    b, T, _ = h.shape
    p = h @ w_in
    z, xbc, dt = jnp.split(p, [SSM_INNER, SSM_INNER + CONV_DIM], axis=-1)
    ext = jnp.concatenate([conv_hist.astype(xbc.dtype), xbc], axis=1)
    xbc = jax.nn.silu(causal_dwconv(ext, conv_w) + conv_b)
    new_hist = ext[:, -(SSM_CONV - 1):]
    xs, Bm, Cm = jnp.split(xbc, [SSM_INNER, SSM_INNER + SSM_GROUPS * D_STATE], axis=-1)
    xs = xs.reshape(b, T, SSM_HEADS, SSM_HEAD_DIM)
    Bm = Bm.reshape(b, T, SSM_GROUPS, D_STATE)
    Cm = Cm.reshape(b, T, SSM_GROUPS, D_STATE)
    dtp = jax.nn.softplus(dt.astype(jnp.float32) + dt_b.astype(jnp.float32))
    A = -jnp.exp(a_lg.astype(jnp.float32))
    y, hT = ssd(xs, dtp, A, Bm, Cm, ssm_state)
    y = y + d_sk.astype(jnp.float32)[:, None] * xs.astype(jnp.float32)
    y = y.reshape(b, T, SSM_INNER) * jax.nn.silu(z.astype(jnp.float32))
    yg = y.reshape(b, T, SSM_GROUPS, SSM_INNER // SSM_GROUPS)
    yg = yg * lax.rsqrt(jnp.mean(yg * yg, axis=-1, keepdims=True) + EPS)
    y = (yg.reshape(b, T, SSM_INNER) * gn_w.astype(jnp.float32)).astype(h.dtype)
    return y @ w_out, new_hist, hT


def setup_inputs(seed: int = 0) -> dict:
    key = jax.random.key(seed)
    ks = iter(jax.random.split(key, 32))
    f32 = jnp.float32
    nrm = lambda shape, scale: jax.random.normal(next(ks), shape, f32) * scale
    n_pages = PAST_LEN // PAGE_SIZE
    n_used = DEC_BATCH * n_pages
    n_phys = n_used + (n_used + 3) // 4
    x_prompt = nrm((BATCH, SEQ, D_MODEL), 1.0)
    x_sample = nrm((DEC_BATCH, DEC_SEQ, D_MODEL), 1.0)
    perm = jax.random.permutation(next(ks), n_phys)
    page_table = perm[:n_used].reshape(DEC_BATCH, n_pages).astype(jnp.int32)
    cache_k = nrm((N_ATT_LAYERS, n_phys, PAGE_SIZE, ATT_HEADS, ATT_HEAD_DIM), 1.0)
    cache_v = nrm((N_ATT_LAYERS, n_phys, PAGE_SIZE, ATT_HEADS, ATT_HEAD_DIM), 1.0)
    cache_logf = jax.nn.log_sigmoid(nrm((N_ATT_LAYERS, n_phys, PAGE_SIZE, ATT_HEADS), 0.5) + FORGET_CACHE_BIAS)
    state_sconv = nrm((N_ATT_LAYERS, DEC_BATCH, SC_WIDTH - 1, SC_CH), 1.0)
    state_mconv = nrm((N_SSM_LAYERS, DEC_BATCH, SSM_CONV - 1, CONV_DIM), 1.0)
    state_ssm = nrm((N_SSM_LAYERS, DEC_BATCH, SSM_HEADS, SSM_HEAD_DIM, D_STATE), 0.1)
    norm_mix = 1.0 + nrm((DEPTH, D_MODEL), 0.02)
    norm_ffn = 1.0 + nrm((DEPTH, D_MODEL), 0.02)
    norm_final = 1.0 + nrm((D_MODEL,), 0.02)
    w_in_even = nrm((N_ATT_LAYERS, D_MODEL, IN_EVEN), D_MODEL ** -0.5)
    b_forget = FORGET_BIAS + nrm((N_ATT_LAYERS, ATT_HEADS), 0.1)
    w_sconv = nrm((N_ATT_LAYERS, SC_WIDTH, SC_CH), SC_WIDTH ** -0.5)
    w_out_even = nrm((N_ATT_LAYERS, SC_CH + ATT_DIM, D_MODEL), (SC_CH + ATT_DIM) ** -0.5)
    w_in_ssm = nrm((N_SSM_LAYERS, D_MODEL, IN_SSM), D_MODEL ** -0.5)
    w_mconv = nrm((N_SSM_LAYERS, SSM_CONV, CONV_DIM), SSM_CONV ** -0.5)
    b_mconv = nrm((N_SSM_LAYERS, CONV_DIM), 0.01)
    dt0 = jnp.exp(jax.random.uniform(next(ks), (N_SSM_LAYERS, SSM_HEADS), f32, math.log(DT_MIN), math.log(DT_MAX)))
    dt_bias = dt0 + jnp.log(-jnp.expm1(-dt0))
    a_log = jnp.log(jax.random.uniform(next(ks), (N_SSM_LAYERS, SSM_HEADS), f32, 1.0, 16.0))
    d_skip = 1.0 + nrm((N_SSM_LAYERS, SSM_HEADS), 0.02)
    w_gnorm = 1.0 + nrm((N_SSM_LAYERS, SSM_INNER), 0.02)
    w_out_ssm = nrm((N_SSM_LAYERS, SSM_INNER, D_MODEL), SSM_INNER ** -0.5)
    w_gate = nrm((DEPTH, D_MODEL, D_FF), D_MODEL ** -0.5)
    w_up = nrm((DEPTH, D_MODEL, D_FF), D_MODEL ** -0.5)
    w_down = nrm((DEPTH, D_FF, D_MODEL), D_FF ** -0.5)
    return {'x_prompt': x_prompt, 'x_sample': x_sample, 'cache_k': cache_k, 'cache_v': cache_v,
            'cache_logf': cache_logf, 'page_table': page_table, 'state_sconv': state_sconv,
            'state_mconv': state_mconv, 'state_ssm': state_ssm, 'norm_mix': norm_mix,
            'norm_ffn': norm_ffn, 'norm_final': norm_final, 'w_in_even': w_in_even,
            'b_forget': b_forget, 'w_sconv': w_sconv, 'w_out_even': w_out_even,
            'w_in_ssm': w_in_ssm, 'w_mconv': w_mconv, 'b_mconv': b_mconv, 'dt_bias': dt_bias,
            'a_log': a_log, 'd_skip': d_skip, 'w_gnorm': w_gnorm, 'w_out_ssm': w_out_ssm,
            'w_gate': w_gate, 'w_up': w_up, 'w_down': w_down}


def reference(x_prompt, x_sample, cache_k, cache_v, cache_logf, page_table, state_sconv, state_mconv,
              state_ssm, norm_mix, norm_ffn, norm_final, w_in_even, b_forget, w_sconv, w_out_even,
              w_in_ssm, w_mconv, b_mconv, dt_bias, a_log, d_skip, w_gnorm, w_out_ssm,
              w_gate, w_up, w_down):
    xp, xs = x_prompt, x_sample
    bp = xp.shape[0]
    kp_l, vp_l, lp_l, scp_l, mcp_l, ssp_l = [], [], [], [], [], []
    ks_l, vs_l, ls_l, scs_l, mcs_l, sss_l = [], [], [], [], [], []
    for layer in range(DEPTH):
        hp = rmsnorm(xp, norm_mix[layer])
        hs = rmsnorm(xs, norm_mix[layer])
        if layer % 2 == 0:
            i = layer // 2
            gb, gc, u, q, k, v, lf = even_in_proj(hp, w_in_even[i], b_forget[i])
            hist0 = jnp.zeros((bp, SC_WIDTH - 1, SC_CH), hp.dtype)
            y_a, sc = short_conv(gb, gc, u, hist0, w_sconv[i])
            y_b = fox_prompt(q, k, v, lf)
            xp = xp + jnp.concatenate([y_a, y_b], axis=-1) @ w_out_even[i]
            kp_l.append(k); vp_l.append(v); lp_l.append(lf.astype(cache_logf.dtype)); scp_l.append(sc)
            gb, gc, u, q, k, v, lf = even_in_proj(hs, w_in_even[i], b_forget[i])
            y_a, sc = short_conv(gb, gc, u, state_sconv[i], w_sconv[i])
            y_b = fox_sample(q, k, v, lf, cache_k[i], cache_v[i], cache_logf[i], page_table)
            xs = xs + jnp.concatenate([y_a, y_b], axis=-1) @ w_out_even[i]
            ks_l.append(k); vs_l.append(v); ls_l.append(lf.astype(cache_logf.dtype)); scs_l.append(sc)
        else:
            j = layer // 2
            hist0 = jnp.zeros((bp, SSM_CONV - 1, CONV_DIM), hp.dtype)
            s0 = jnp.zeros((bp, SSM_HEADS, SSM_HEAD_DIM, D_STATE), jnp.float32)
            y, mc, ss = ssm_mixer(hp, hist0, s0, w_in_ssm[j], w_mconv[j], b_mconv[j], dt_bias[j],
                                  a_log[j], d_skip[j], w_gnorm[j], w_out_ssm[j])
            xp = xp + y
            mcp_l.append(mc); ssp_l.append(ss.astype(state_ssm.dtype))
            y, mc, ss = ssm_mixer(hs, state_mconv[j], state_ssm[j], w_in_ssm[j], w_mconv[j], b_mconv[j],
                                  dt_bias[j], a_log[j], d_skip[j], w_gnorm[j], w_out_ssm[j])
            xs = xs + y
            mcs_l.append(mc); sss_l.append(ss.astype(state_ssm.dtype))
        xp = xp + swiglu(rmsnorm(xp, norm_ffn[layer]), w_gate[layer], w_up[layer], w_down[layer])
        xs = xs + swiglu(rmsnorm(xs, norm_ffn[layer]), w_gate[layer], w_up[layer], w_down[layer])
    y_prompt = rmsnorm(xp, norm_final)
    y_sample = rmsnorm(xs, norm_final)
    new_k_prompt = jnp.stack(kp_l); new_v_prompt = jnp.stack(vp_l); new_logf_prompt = jnp.stack(lp_l)
    new_sconv_prompt = jnp.stack(scp_l); new_mconv_prompt = jnp.stack(mcp_l); new_ssm_prompt = jnp.stack(ssp_l)
    new_k_sample = jnp.stack(ks_l); new_v_sample = jnp.stack(vs_l); new_logf_sample = jnp.stack(ls_l)
    new_sconv_sample = jnp.stack(scs_l); new_mconv_sample = jnp.stack(mcs_l); new_ssm_sample = jnp.stack(sss_l)
    return (y_prompt, y_sample, new_k_prompt, new_v_prompt, new_logf_prompt, new_sconv_prompt,
            new_mconv_prompt, new_ssm_prompt, new_k_sample, new_v_sample, new_logf_sample,
            new_sconv_sample, new_mconv_sample, new_ssm_sample)
```

```python
import functools

import jax
import jax.numpy as jnp
from jax import lax
from jax.experimental import pallas as pl
from jax.experimental.pallas import tpu as pltpu

F32 = jnp.float32
BF16 = jnp.bfloat16

SC_WIDTH = 3
ATT_HEADS = 8
ATT_HEAD_DIM = 128
SSM_HEAD_DIM = 64
SSM_GROUPS = 8
D_STATE = 128
SSM_CONV = 4
SSM_CHUNK = 128
EPS = 1e-6

LANES = 128
SUBLANES = 8
VMEM_LIMIT_BYTES = 56 * 1024 * 1024
NEG_BIG = -1e30

HIGHEST = lax.Precision.HIGHEST
NT_DIMS = (((1,), (1,)), ((), ()))
TN_DIMS = (((0,), (0,)), ((), ()))


def _params(n_grid_dims):
    return pltpu.CompilerParams(dimension_semantics=("arbitrary",) * n_grid_dims,
                                vmem_limit_bytes=VMEM_LIMIT_BYTES)


def _row_tile(mp, pref):
    t = pref
    while mp % t:
        t //= 2
    assert t >= SUBLANES, (mp, pref)
    return t


def _rmsnorm_body(nt, xp, xs, w, op, os):
    i = pl.program_id(0)

    def norm(x):
        y = x * lax.rsqrt(jnp.mean(x * x, axis=-1, keepdims=True) + EPS)
        return (y * w[...]).astype(op.dtype)

    @pl.when(i < nt)
    def _():
        op[...] = norm(xp[...])

    @pl.when(i == nt)
    def _():
        os[...] = norm(xs[...])


def _rmsnorm_pair(x, w, out_dtype):
    xp, xs = x
    mp, d = xp.shape
    ms = xs.shape[0]
    tm = _row_tile(mp, 512)
    nt = mp // tm
    pidx = lambda i: (jnp.minimum(i, nt - 1), 0)
    return pl.pallas_call(
        functools.partial(_rmsnorm_body, nt),
        grid=(nt + 1,),
        in_specs=[pl.BlockSpec((tm, d), pidx), pl.BlockSpec((ms, d), lambda i: (0, 0)),
                  pl.BlockSpec((1, d), lambda i: (0, 0))],
        out_specs=[pl.BlockSpec((tm, d), pidx), pl.BlockSpec((ms, d), lambda i: (0, 0))],
        out_shape=[jax.ShapeDtypeStruct((mp, d), out_dtype), jax.ShapeDtypeStruct((ms, d), out_dtype)],
        compiler_params=_params(1), name="rmsnorm",
    )(xp, xs, w.reshape(1, d).astype(F32))


def _softplus(x):
    return jnp.maximum(x, 0.0) + jnp.log1p(jnp.exp(-jnp.abs(x)))


def _silu(x):
    return x * (1.0 / (1.0 + jnp.exp(-x)))


def _mm_body(nt, n_lhs, mode, has_res, has_bias, scale, *refs):
    i = pl.program_id(1)
    xrefs = refs[:2 * n_lhs]
    n_w = 2 if mode == "swiglu" else n_lhs
    wrefs = refs[2 * n_lhs:2 * n_lhs + n_w]
    pos = 2 * n_lhs + n_w
    rp = rs = bias = None
    if has_res:
        rp, rs = refs[pos], refs[pos + 1]
        pos += 2
    if has_bias:
        bias = refs[pos]
        pos += 1
    op, os = refs[pos], refs[pos + 1]

    def compute(which, out, res):
        if mode == "swiglu":
            x = xrefs[which][...]
            g = jnp.dot(x, wrefs[0][...], preferred_element_type=F32)
            u = jnp.dot(x, wrefs[1][...], preferred_element_type=F32)
            acc = _silu(g) * u
        else:
            acc = jnp.dot(xrefs[which][...], wrefs[0][...], preferred_element_type=F32)
            for k in range(1, n_lhs):
                acc += jnp.dot(xrefs[2 * k + which][...], wrefs[k][...], preferred_element_type=F32)
        if scale is not None:
            acc = acc * scale
        if has_bias:
            acc = acc + bias[...]
        if mode == "softplus":
            acc = _softplus(acc)
        if has_res:
            acc = res[...] + acc
        out[...] = acc.astype(out.dtype)

    @pl.when(i < nt)
    def _():
        compute(0, op, rp)

    @pl.when(i == nt)
    def _():
        compute(1, os, rs)


def _mm_pair(xs_list, ws_list, *, out_dtype=F32, mode="plain", res=None, bias=None, scale=None, name="matmul"):
    n_lhs = len(xs_list)
    mp = xs_list[0][0].shape[0]
    ms = xs_list[0][1].shape[0]
    n = ws_list[0].shape[1]
    kmax = max(w.shape[0] for w in ws_list)
    tm = _row_tile(mp, 1024 if kmax <= 4096 else 512)
    nt = mp // tm
    if n % 1024 == 0 and kmax <= 2048 and mode != "swiglu":
        tn = 1024
    elif n % 512 == 0:
        tn = 512
    else:
        assert n % LANES == 0 and n <= 512, n
        tn = n
    pidx_x = lambda j, i: (jnp.minimum(i, nt - 1), 0)
    pidx_o = lambda j, i: (jnp.minimum(i, nt - 1), j)
    in_specs, args = [], []
    for xp, xs in xs_list:
        kk = xp.shape[1]
        in_specs += [pl.BlockSpec((tm, kk), pidx_x), pl.BlockSpec((ms, kk), lambda j, i: (0, 0))]
        args += [xp, xs]
    for w in ws_list:
        in_specs.append(pl.BlockSpec((w.shape[0], tn), lambda j, i: (0, j)))
        args.append(w)
    if res is not None:
        in_specs += [pl.BlockSpec((tm, tn), pidx_o), pl.BlockSpec((ms, tn), lambda j, i: (0, j))]
        args += [res[0], res[1]]
    if bias is not None:
        in_specs.append(pl.BlockSpec((1, tn), lambda j, i: (0, j)))
        args.append(bias.reshape(1, n).astype(F32))
    outs = pl.pallas_call(
        functools.partial(_mm_body, nt, n_lhs, mode, res is not None, bias is not None, scale),
        grid=(n // tn, nt + 1),
        in_specs=in_specs,
        out_specs=[pl.BlockSpec((tm, tn), pidx_o), pl.BlockSpec((ms, tn), lambda j, i: (0, j))],
        out_shape=[jax.ShapeDtypeStruct((mp, n), out_dtype), jax.ShapeDtypeStruct((ms, n), out_dtype)],
        compiler_params=_params(2), name=name,
    )(*args)
    return outs[0], outs[1]


def _log_sigmoid(x):
    return -_softplus(-x)


def _forget_body(nb, seq, dec_seq, ck, xp, xs, w, b, lp, fp, ls, fs):
    i = pl.program_id(0)

    @pl.when(i < nb)
    def _():
        r = lax.broadcasted_iota(jnp.int32, (ck, ck), 0)
        c = lax.broadcasted_iota(jnp.int32, (ck, ck), 1)
        tri = (r >= c).astype(F32)
        carry = jnp.zeros((1, LANES), F32)
        for t in range(seq // ck):
            f = jnp.dot(xp[t * ck:(t + 1) * ck, :], w[...], preferred_element_type=F32) + b[...]
            lf = _log_sigmoid(f)
            lp[t * ck:(t + 1) * ck, :] = lf
            cs = jnp.dot(tri, lf, preferred_element_type=F32, precision=HIGHEST) + carry
            fp[t * ck:(t + 1) * ck, :] = cs
            carry = cs[ck - 1:ck, :]

    @pl.when(i == nb)
    def _():
        ms = xs.shape[0]
        r = lax.broadcasted_iota(jnp.int32, (ms, ms), 0)
        c = lax.broadcasted_iota(jnp.int32, (ms, ms), 1)
        tri = ((r // dec_seq == c // dec_seq) & (r >= c)).astype(F32)
        f = jnp.dot(xs[...], w[...], preferred_element_type=F32) + b[...]
        lf = _log_sigmoid(f)
        ls[...] = lf
        fs[...] = jnp.dot(tri, lf, preferred_element_type=F32, precision=HIGHEST)


def _forget_pair(h, w_f, b_f, nb, seq, dec_seq):
    hp, hs = h
    mp, d = hp.shape
    ms = hs.shape[0]
    assert ms <= 512 and ms % SUBLANES == 0
    ck = _row_tile(seq, 256)
    n_h = w_f.shape[1]
    wpad = jnp.zeros((d, LANES), BF16).at[:, :n_h].set(w_f.astype(BF16))
    bpad = jnp.zeros((1, LANES), F32).at[0, :n_h].set(b_f.astype(F32))
    pidx = lambda i: (jnp.minimum(i, nb - 1), 0)
    cidx = lambda i: (0, 0)
    return pl.pallas_call(
        functools.partial(_forget_body, nb, seq, dec_seq, ck),
        grid=(nb + 1,),
        in_specs=[pl.BlockSpec((seq, d), pidx), pl.BlockSpec((ms, d), cidx),
                  pl.BlockSpec((d, LANES), cidx), pl.BlockSpec((1, LANES), cidx)],
        out_specs=[pl.BlockSpec((seq, LANES), pidx), pl.BlockSpec((seq, LANES), pidx),
                   pl.BlockSpec((ms, LANES), cidx), pl.BlockSpec((ms, LANES), cidx)],
        out_shape=[jax.ShapeDtypeStruct((mp, LANES), F32), jax.ShapeDtypeStruct((mp, LANES), F32),
                   jax.ShapeDtypeStruct((ms, LANES), F32), jax.ShapeDtypeStruct((ms, LANES), F32)],
        compiler_params=_params(1), name="forget_gates",
    )(hp, hs, wpad, bpad)


def _shifted(x, hist, k, width, t_in_seq):
    n = x.shape[0]
    xk = pltpu.roll(x, k, axis=0)
    if hist is None:
        return xk
    back = (width - 1 - k) % n
    hk = hist if back == 0 else pltpu.roll(hist, n - back, axis=0)
    return jnp.where(t_in_seq >= k, xk, hk)


def _conv_rows(x, hist, wref, width, t_in_seq):
    acc = x * wref[width - 1:width, :]
    for k in range(1, width):
        acc += _shifted(x, hist, k, width, t_in_seq) * wref[width - 1 - k:width - k, :]
    return acc


def _conv_body(nt, tiles_per_seq, width, kind, dec_seq, *refs):
    if kind == "sconv":
        (gbp, gcp, up, gbs, gcs, us, hs, w, yp, ys, zp_tail, zs, carry) = refs
    else:
        (xp, xs, hs, w, b, yp, ys, carry) = refs
    i = pl.program_id(1)

    def finish(conv, gate):
        if kind == "sconv":
            return gate * conv
        return _silu(conv + b[...])

    @pl.when(i < nt)
    def _():
        t0 = i % tiles_per_seq
        x = gcp[...] * up[...] if kind == "sconv" else xp[...]
        ts = x.shape[0]

        @pl.when(t0 == 0)
        def _():
            carry[...] = jnp.zeros_like(carry)

        y = _conv_rows(x, None, w, width, None)
        gate = gbp[...] if kind == "sconv" else None
        yp[...] = finish(y, gate).astype(yp.dtype)
        head = x[:SUBLANES, :]
        t8 = lax.broadcasted_iota(jnp.int32, head.shape, 0)
        y8 = _conv_rows(head, carry[...], w, width, t8)
        yp[:SUBLANES, :] = finish(y8, gate[:SUBLANES, :] if kind == "sconv" else None).astype(yp.dtype)
        tail = x[ts - SUBLANES:, :]
        carry[...] = pltpu.roll(tail, width - 1, axis=0)
        if kind == "sconv":
            zp_tail[0] = tail

    @pl.when(i == nt)
    def _():
        x = gcs[...] * us[...] if kind == "sconv" else xs[...]
        t = lax.broadcasted_iota(jnp.int32, x.shape, 0) % dec_seq
        y = _conv_rows(x, hs[...], w, width, t)
        ys[...] = finish(y, gbs[...] if kind == "sconv" else None).astype(ys.dtype)
        if kind == "sconv":
            zs[...] = x


def _pad_hist(state, dec_seq):
    b, hw, c = state.shape
    assert dec_seq >= hw
    return jnp.concatenate([state.astype(F32), jnp.zeros((b, dec_seq - hw, c), F32)], axis=1).reshape(b * dec_seq, c)


def _sconv_pair(gcu, state, w, nb, seq, dec_seq):
    gp, gs = gcu
    mp = gp.shape[0]
    ms = gs.shape[0]
    ch = gp.shape[1] // 3
    tc = 512
    ncb = ch // tc
    ts = _row_tile(seq, 512)
    tps = seq // ts
    nt = mp // ts
    pr = lambda j, i: jnp.minimum(i, nt - 1)
    in_specs = []
    for part in range(3):
        in_specs.append(pl.BlockSpec((ts, tc), lambda j, i, part=part: (pr(j, i), part * ncb + j)))
    for part in range(3):
        in_specs.append(pl.BlockSpec((ms, tc), lambda j, i, part=part: (0, part * ncb + j)))
    in_specs += [pl.BlockSpec((ms, tc), lambda j, i: (0, j)), pl.BlockSpec((SC_WIDTH, tc), lambda j, i: (0, j))]
    out_specs = [pl.BlockSpec((ts, tc), lambda j, i: (pr(j, i), j)),
                 pl.BlockSpec((ms, tc), lambda j, i: (0, j)),
                 pl.BlockSpec((1, SUBLANES, tc), lambda j, i: (pr(j, i) // tps, 0, j)),
                 pl.BlockSpec((ms, tc), lambda j, i: (0, j))]
    out_shape = [jax.ShapeDtypeStruct((mp, ch), BF16), jax.ShapeDtypeStruct((ms, ch), BF16),
                 jax.ShapeDtypeStruct((nb, SUBLANES, ch), F32), jax.ShapeDtypeStruct((ms, ch), F32)]
    yp, ys, zp_tail, zs = pl.pallas_call(
        functools.partial(_conv_body, nt, tps, SC_WIDTH, "sconv", dec_seq),
        grid=(ncb, nt + 1), in_specs=in_specs, out_specs=out_specs, out_shape=out_shape,
        scratch_shapes=[pltpu.VMEM((SUBLANES, tc), F32)],
        compiler_params=_params(2), name="short_conv",
    )(gp, gp, gp, gs, gs, gs, _pad_hist(state, dec_seq), w.astype(F32))
    hw = SC_WIDTH - 1
    new_p = zp_tail[:, SUBLANES - hw:, :]
    new_s = zs.reshape(ms // dec_seq, dec_seq, ch)[:, dec_seq - hw:, :]
    return (yp, ys), new_p, new_s


def _mconv_pair(xbc, state, w, b, seq, dec_seq):
    xp, xs = xbc
    mp, ch = xp.shape
    ms = xs.shape[0]
    tc = 512
    ncb = ch // tc
    ts = _row_tile(seq, 512)
    tps = seq // ts
    nt = mp // ts
    pr = lambda j, i: jnp.minimum(i, nt - 1)
    in_specs = [pl.BlockSpec((ts, tc), lambda j, i: (pr(j, i), j)),
                pl.BlockSpec((ms, tc), lambda j, i: (0, j)),
                pl.BlockSpec((ms, tc), lambda j, i: (0, j)),
                pl.BlockSpec((SSM_CONV, tc), lambda j, i: (0, j)),
                pl.BlockSpec((1, tc), lambda j, i: (0, j))]
    out_specs = [pl.BlockSpec((ts, tc), lambda j, i: (pr(j, i), j)), pl.BlockSpec((ms, tc), lambda j, i: (0, j))]
    out_shape = [jax.ShapeDtypeStruct((mp, ch), F32), jax.ShapeDtypeStruct((ms, ch), F32)]
    yp, ys = pl.pallas_call(
        functools.partial(_conv_body, nt, tps, SSM_CONV, "mconv", dec_seq),
        grid=(ncb, nt + 1), in_specs=in_specs, out_specs=out_specs, out_shape=out_shape,
        scratch_shapes=[pltpu.VMEM((SUBLANES, tc), F32)],
        compiler_params=_params(2), name="mamba_conv",
    )(xp, xs, _pad_hist(state, dec_seq), w.astype(F32), b.reshape(1, ch).astype(F32))
    return yp, ys


def _fox_prompt_body(tq, q, k, v, fc, fr, o, m_s, l_s, acc_s):
    qi = pl.program_id(2)
    qv = q[...]
    fcv = fc[0, 0]
    m_s[...] = jnp.full_like(m_s, NEG_BIG)
    l_s[...] = jnp.zeros_like(l_s)
    acc_s[...] = jnp.zeros_like(acc_s)

    def step(ki, masked):
        start = pl.multiple_of(ki * tq, tq)
        kb = k[pl.ds(start, tq), :].astype(BF16)
        vb = v[pl.ds(start, tq), :].astype(BF16)
        s = lax.dot_general(qv, kb, NT_DIMS, preferred_element_type=F32)
        s = s + fcv - fr[0, 0, ki]
        if masked:
            r = lax.broadcasted_iota(jnp.int32, s.shape, 0)
            c = lax.broadcasted_iota(jnp.int32, s.shape, 1)
            s = jnp.where(r >= c, s, NEG_BIG)
        m_prev = m_s[...]
        m_new = jnp.maximum(m_prev, jnp.max(s, axis=-1, keepdims=True))
        p = jnp.exp(s - m_new)
        alpha = jnp.exp(m_prev - m_new)
        l_s[...] = alpha * l_s[...] + jnp.sum(p, axis=-1, keepdims=True)
        acc_s[...] = alpha * acc_s[...] + jnp.dot(p.astype(BF16), vb, preferred_element_type=F32)
        m_s[...] = m_new

    def loop_body(ki, carry):
        step(ki, False)
        return carry

    lax.fori_loop(0, qi, loop_body, 0)
    step(qi, True)
    o[...] = (acc_s[...] / l_s[...]).astype(o.dtype)


def _fox_prompt(q, k, v, fcum, nb, seq):
    mp = q.shape[0]
    dh = ATT_HEAD_DIM
    nh = q.shape[1] // dh
    tq = _row_tile(seq, 512)
    nq = seq // tq
    ft = jnp.transpose(fcum, (0, 2, 1))
    fcol = ft[..., None]
    frow = ft.reshape(nb, nh, nq, 1, tq)
    return pl.pallas_call(
        functools.partial(_fox_prompt_body, tq),
        grid=(nb, nh, nq),
        in_specs=[pl.BlockSpec((tq, dh), lambda b, h, i: (b * nq + i, h)),
                  pl.BlockSpec((seq, dh), lambda b, h, i: (b, h)),
                  pl.BlockSpec((seq, dh), lambda b, h, i: (b, h)),
                  pl.BlockSpec((1, 1, tq, 1), lambda b, h, i: (b, h, i, 0)),
                  pl.BlockSpec((1, 1, nq, 1, tq), lambda b, h, i: (b, h, 0, 0, 0))],
        out_specs=pl.BlockSpec((tq, dh), lambda b, h, i: (b * nq + i, h)),
        out_shape=jax.ShapeDtypeStruct((mp, nh * dh), BF16),
        scratch_shapes=[pltpu.VMEM((tq, 1), F32), pltpu.VMEM((tq, 1), F32), pltpu.VMEM((tq, dh), F32)],
        compiler_params=_params(3), name="fox_prompt",
    )(q, k, v, fcol, frow)


def _past_bias_body(n_pages, page, nh, *refs):
    lf_refs = refs[1:1 + n_pages]
    out = refs[1 + n_pages]
    rows = refs[2 + n_pages]
    w = page * nh
    rows[...] = jnp.zeros_like(rows)
    for p in range(n_pages):
        rows[p:p + 1, :] = lf_refs[p][0]
    lf = rows[...]
    a = lax.broadcasted_iota(jnp.int32, (w, w), 0)
    c = lax.broadcasted_iota(jnp.int32, (w, w), 1)
    same_head = (a % nh) == (c % nh)
    later_in_page = (same_head & ((a // nh) > (c // nh))).astype(F32)
    within = jnp.dot(lf, later_in_page, preferred_element_type=F32, precision=HIGHEST)
    totals = jnp.dot(lf, same_head.astype(F32), preferred_element_type=F32, precision=HIGHEST)
    np_pad = rows.shape[0]
    pr = lax.broadcasted_iota(jnp.int32, (np_pad, np_pad), 0)
    pc = lax.broadcasted_iota(jnp.int32, (np_pad, np_pad), 1)
    later_pages = (pc > pr).astype(F32)
    after = jnp.dot(later_pages, totals, preferred_element_type=F32, precision=HIGHEST)
    out[0] = (within + after)[:n_pages, :]


def _past_bias(cache_lf, page_table):
    n_phys, page, nh = cache_lf.shape
    nb, n_pages = page_table.shape
    w = page * nh
    np_pad = -(-n_pages // LANES) * LANES
    lf = cache_lf.reshape(n_phys, 1, w)
    in_specs = [pl.BlockSpec((1, 1, w), lambda b, pt, p=p: (pt[b, p], 0, 0)) for p in range(n_pages)]
    grid_spec = pltpu.PrefetchScalarGridSpec(
        num_scalar_prefetch=1, grid=(nb,), in_specs=in_specs,
        out_specs=pl.BlockSpec((1, n_pages, w), lambda b, pt: (b, 0, 0)),
        scratch_shapes=[pltpu.VMEM((np_pad, w), F32)])
    return pl.pallas_call(
        functools.partial(_past_bias_body, n_pages, page, nh),
        grid_spec=grid_spec, out_shape=jax.ShapeDtypeStruct((nb, n_pages, w), F32),
        compiler_params=_params(1), name="fox_past_bias",
    )(page_table, *([lf] * n_pages))


def _fox_sample_body(pps, n_steps, nh, dec_seq, *refs):
    qall, knew, vnew, fcol, frow_new, rb = refs[1:7]
    krefs = refs[7:7 + pps]
    vrefs = refs[7 + pps:7 + 2 * pps]
    o = refs[7 + 2 * pps]
    m_s, l_s, acc_s = refs[8 + 2 * pps:]
    step = pl.program_id(1)
    qv = qall[0]
    fcv = fcol[0]
    nrow = qv.shape[0]

    def update(s, vb):
        m_prev = m_s[...]
        m_new = jnp.maximum(m_prev, jnp.max(s, axis=-1, keepdims=True))
        p = jnp.exp(s - m_new)
        alpha = jnp.exp(m_prev - m_new)
        l_s[...] = alpha * l_s[...] + jnp.sum(p, axis=-1, keepdims=True)
        acc_s[...] = alpha * acc_s[...] + jnp.dot(p.astype(BF16), vb, preferred_element_type=F32)
        m_s[...] = m_new

    @pl.when(step == 0)
    def _():
        m_s[...] = jnp.full_like(m_s, NEG_BIG)
        l_s[...] = jnp.zeros_like(l_s)
        acc_s[...] = jnp.zeros_like(acc_s)
        kn = knew[0].astype(BF16)
        g = lax.dot_general(qv, kn, NT_DIMS, preferred_element_type=F32)
        r = lax.broadcasted_iota(jnp.int32, g.shape, 0)
        c = lax.broadcasted_iota(jnp.int32, g.shape, 1)
        valid = ((c % nh) == (r // dec_seq)) & ((c // nh) <= (r % dec_seq))
        s = jnp.where(valid, g + fcv - frow_new[0], NEG_BIG)
        update(s, vnew[0].astype(BF16))

    lanes = krefs[0].shape[1]
    r = lax.broadcasted_iota(jnp.int32, (nrow, lanes), 0)
    c = lax.broadcasted_iota(jnp.int32, (nrow, lanes), 1)
    head_mask = jnp.where((c % nh) == (r // dec_seq), 0.0, NEG_BIG).astype(F32)
    for i in range(pps):
        kb = krefs[i][0].astype(BF16)
        g = lax.dot_general(qv, kb, NT_DIMS, preferred_element_type=F32)
        s = g + (rb[0, i] + fcv) + head_mask
        update(s, vrefs[i][0].astype(BF16))

    @pl.when(step == n_steps - 1)
    def _():
        o[0] = (acc_s[...] / l_s[...]).astype(o.dtype)


def _fox_sample(q, k, v, fn, cache_k, cache_v, cache_lf, page_table, dec_seq):
    ms = q.shape[0]
    nb = ms // dec_seq
    nh, dh = ATT_HEADS, ATT_HEAD_DIM
    n_phys, page = cache_k.shape[0], cache_k.shape[1]
    n_pages = page_table.shape[1]
    pps = 8
    while n_pages % pps:
        pps //= 2
    n_steps = n_pages // pps
    nrow = nh * dec_seq
    rbias = _past_bias(cache_lf, page_table).reshape(nb, n_pages, 1, page * nh)
    qall = jnp.transpose(q.reshape(nb, dec_seq, nh, dh), (0, 2, 1, 3)).reshape(nb, nrow, dh)
    knew = k.reshape(nb, dec_seq * nh, dh)
    vnew = v.reshape(nb, dec_seq * nh, dh)
    fcol = jnp.transpose(fn, (0, 2, 1)).reshape(nb, nrow, 1)
    frow_new = fn.reshape(nb, 1, dec_seq * nh)
    ck = cache_k.reshape(n_phys, page * nh, dh)
    cv = cache_v.reshape(n_phys, page * nh, dh)
    per_b = lambda shape: pl.BlockSpec((1,) + shape, lambda b, s, pt: (b, 0, 0))
    in_specs = [per_b((nrow, dh)), per_b((dec_seq * nh, dh)), per_b((dec_seq * nh, dh)), per_b((nrow, 1)),
                per_b((1, dec_seq * nh)),
                pl.BlockSpec((1, pps, 1, page * nh), lambda b, s, pt: (b, s, 0, 0))]
    for _ in range(2):
        in_specs += [pl.BlockSpec((1, page * nh, dh), lambda b, s, pt, i=i: (pt[b, s * pps + i], 0, 0))
                     for i in range(pps)]
    grid_spec = pltpu.PrefetchScalarGridSpec(
        num_scalar_prefetch=1, grid=(nb, n_steps), in_specs=in_specs,
        out_specs=pl.BlockSpec((1, nrow, dh), lambda b, s, pt: (b, 0, 0)),
        scratch_shapes=[pltpu.VMEM((nrow, 1), F32), pltpu.VMEM((nrow, 1), F32), pltpu.VMEM((nrow, dh), F32)])
    o = pl.pallas_call(
        functools.partial(_fox_sample_body, pps, n_steps, nh, dec_seq),
        grid_spec=grid_spec, out_shape=jax.ShapeDtypeStruct((nb, nrow, dh), BF16),
        compiler_params=_params(2), name="fox_sample",
    )(page_table, qall, knew, vnew, fcol, frow_new, rbias, *([ck] * pps), *([cv] * pps))
    return jnp.transpose(o.reshape(nb, nh, dec_seq, dh), (0, 2, 1, 3)).reshape(ms, nh * dh)


def _ssd_body(t_in, has_init, xs_r, z_r, b_r, c_r, dt_r, a_r, d_r, gn_r, *rest):
    if has_init:
        s0_r, y_o, st_o, st = rest
    else:
        y_o, st_o, st = rest
    ci = pl.program_id(2)
    nc = pl.num_programs(2)
    L = SSM_CHUNK
    P = SSM_HEAD_DIM
    R = xs_r.shape[2] // P

    @pl.when(ci == 0)
    def _():
        if has_init:
            st[...] = s0_r[0, 0]
        else:
            st[...] = jnp.zeros_like(st)

    def rows(ref):
        v = ref[0]
        if t_in < L:
            v = jnp.concatenate([v, jnp.zeros((L - t_in, v.shape[1]), v.dtype)], axis=0)
        return v

    xs = rows(xs_r)
    bm = rows(b_r).astype(BF16)
    cm = rows(c_r).astype(BF16)
    dt = rows(dt_r)
    r = lax.broadcasted_iota(jnp.int32, (L, L), 0)
    c = lax.broadcasted_iota(jnp.int32, (L, L), 1)
    tri = r >= c
    acum = jnp.dot(tri.astype(F32), dt * a_r[...], preferred_element_type=F32, precision=HIGHEST)
    acum_t = acum.T
    cb = lax.dot_general(cm, bm, NT_DIMS, preferred_element_type=F32)
    ys = []
    for h in range(R):
        a_c = acum[:, h:h + 1]
        a_row = acum_t[h:h + 1, :]
        a_last = acum[L - 1:L, h:h + 1]
        decay = jnp.exp(jnp.where(tri, a_c - a_row, -jnp.inf))
        x_h = xs[:, h * P:(h + 1) * P]
        xdt = x_h * dt[:, h:h + 1]
        y = jnp.dot((cb * decay).astype(BF16), xdt.astype(BF16), preferred_element_type=F32)
        s_h = st[h * P:(h + 1) * P, :]
        y += lax.dot_general(cm, s_h.astype(BF16), NT_DIMS, preferred_element_type=F32) * jnp.exp(a_c)
        xw = (xdt * jnp.exp(a_last - a_c)).astype(BF16)
        new_states = lax.dot_general(xw, bm, TN_DIMS, preferred_element_type=F32)
        st[h * P:(h + 1) * P, :] = s_h * jnp.exp(a_last) + new_states
        ys.append(y)
    y = jnp.concatenate(ys, axis=1) + d_r[...] * xs
    y = y[:t_in, :] * _silu(z_r[0])
    y = y * lax.rsqrt(jnp.mean(y * y, axis=-1, keepdims=True) + EPS)
    y_o[0] = (y * gn_r[...]).astype(y_o.dtype)

    @pl.when(ci == nc - 1)
    def _():
        st_o[0, 0] = st[...]


def _ssd(xa, z, dtp, a_pad, d_exp, gn_w, init_state, nb, seq):
    G, N, P = SSM_GROUPS, D_STATE, SSM_HEAD_DIM
    inner = z.shape[1]
    gw = inner // G
    L = SSM_CHUNK
    t_in = min(seq, L)
    assert seq % t_in == 0 and (seq <= L or seq % L == 0)
    nc = seq // t_in
    xa3 = xa.reshape(nb, seq, xa.shape[1])
    z3 = z.reshape(nb, seq, inner)
    dt3 = dtp.reshape(nb, seq, G * LANES)
    has_init = init_state is not None
    in_specs = [pl.BlockSpec((1, t_in, gw), lambda b, g, c: (b, c, g)),
                pl.BlockSpec((1, t_in, gw), lambda b, g, c: (b, c, g)),
                pl.BlockSpec((1, t_in, N), lambda b, g, c: (b, c, inner // N + g)),
                pl.BlockSpec((1, t_in, N), lambda b, g, c: (b, c, inner // N + G + g)),
                pl.BlockSpec((1, t_in, LANES), lambda b, g, c: (b, c, g)),
                pl.BlockSpec((1, LANES), lambda b, g, c: (0, g)),
                pl.BlockSpec((1, gw), lambda b, g, c: (0, g)),
                pl.BlockSpec((1, gw), lambda b, g, c: (0, g))]
    args = [xa3, z3, xa3, xa3, dt3, a_pad, d_exp, gn_w]
    if has_init:
        in_specs.append(pl.BlockSpec((1, 1, gw, N), lambda b, g, c: (b, g, 0, 0)))
        args.append(init_state)
    y, st = pl.pallas_call(
        functools.partial(_ssd_body, t_in, has_init),
        grid=(nb, G, nc), in_specs=in_specs,
        out_specs=[pl.BlockSpec((1, t_in, gw), lambda b, g, c: (b, c, g)),
                   pl.BlockSpec((1, 1, gw, N), lambda b, g, c: (b, g, 0, 0))],
        out_shape=[jax.ShapeDtypeStruct((nb, seq, inner), BF16), jax.ShapeDtypeStruct((nb, G, gw, N), F32)],
        scratch_shapes=[pltpu.VMEM((gw, N), F32)],
        compiler_params=_params(3), name="ssd",
    )(*args)
    return y.reshape(nb * seq, inner), st


def _ffn(x, norm_w, w_gate, w_up, w_down):
    h = _rmsnorm_pair(x, norm_w, BF16)
    a = _mm_pair([h], [w_gate.astype(BF16), w_up.astype(BF16)], out_dtype=BF16, mode="swiglu", name="ffn_gate_up")
    return _mm_pair([a], [w_down.astype(BF16)], res=x, name="ffn_down")


def kernel(x_prompt, x_sample, cache_k, cache_v, cache_logf, page_table, state_sconv, state_mconv, state_ssm, norm_mix, norm_ffn, norm_final, w_in_even, b_forget, w_sconv, w_out_even, w_in_ssm, w_mconv, b_mconv, dt_bias, a_log, d_skip, w_gnorm, w_out_ssm, w_gate, w_up, w_down):
    bp, seq, d = x_prompt.shape
    bs, dec_seq, _ = x_sample.shape
    mp, ms = bp * seq, bs * dec_seq
    nh, dh = ATT_HEADS, ATT_HEAD_DIM
    att = nh * dh
    sc_ch = state_sconv.shape[-1]
    x = (x_prompt.reshape(mp, d), x_sample.reshape(ms, d))

    i = 0
    h = _rmsnorm_pair(x, norm_mix[0], BF16)
    w_in = w_in_even[i]
    c0 = 3 * sc_ch
    gcu = _mm_pair([h], [w_in[:, :c0].astype(BF16)], name="in_even_conv")
    q = _mm_pair([h], [w_in[:, c0:c0 + att].astype(BF16)], out_dtype=BF16, scale=dh ** -0.5, name="in_even_q")
    k = _mm_pair([h], [w_in[:, c0 + att:c0 + 2 * att].astype(BF16)], name="in_even_k")
    v = _mm_pair([h], [w_in[:, c0 + 2 * att:c0 + 3 * att].astype(BF16)], name="in_even_v")
    lf_p, fc_p, lf_s, fc_s = _forget_pair(h, w_in[:, c0 + 3 * att:], b_forget[i], bp, seq, dec_seq)
    logf_p = lf_p[:, :nh].reshape(bp, seq, nh)
    logf_s = lf_s[:, :nh].reshape(bs, dec_seq, nh)
    y_a, sconv_p, sconv_s = _sconv_pair(gcu, state_sconv[i], w_sconv[i], bp, seq, dec_seq)
    yb_p = _fox_prompt(q[0], k[0], v[0], fc_p[:, :nh].reshape(bp, seq, nh), bp, seq)
    yb_s = _fox_sample(q[1], k[1], v[1], fc_s[:, :nh].reshape(bs, dec_seq, nh),
                       cache_k[i], cache_v[i], cache_logf[i], page_table, dec_seq)
    w_out = w_out_even[i]
    x = _mm_pair([y_a, (yb_p, yb_s)], [w_out[:sc_ch].astype(BF16), w_out[sc_ch:].astype(BF16)], res=x,
                 name="out_even")
    x = _ffn(x, norm_ffn[0], w_gate[0], w_up[0], w_down[0])

    j = 0
    G = SSM_GROUPS
    n_heads = a_log.shape[1]
    rg = n_heads // G
    inner = n_heads * SSM_HEAD_DIM
    conv_dim = state_mconv.shape[-1]
    h = _rmsnorm_pair(x, norm_mix[1], BF16)
    w_in = w_in_ssm[j]
    z = _mm_pair([h], [w_in[:, :inner].astype(BF16)], name="in_ssm_z")
    xbc = _mm_pair([h], [w_in[:, inner:inner + conv_dim].astype(BF16)], name="in_ssm_xbc")
    w_dt = w_in[:, inner + conv_dim:].reshape(d, G, rg)
    w_dt = jnp.zeros((d, G, LANES), BF16).at[:, :, :rg].set(w_dt.astype(BF16)).reshape(d, G * LANES)
    pad_heads = lambda vec: jnp.zeros((G, LANES), F32).at[:, :rg].set(vec.astype(F32).reshape(G, rg)).reshape(1, G * LANES)
    dtp = _mm_pair([h], [w_dt], mode="softplus", bias=pad_heads(dt_bias[j]), name="in_ssm_dt")
    a_pad = pad_heads(-jnp.exp(a_log[j].astype(F32)))
    d_exp = jnp.repeat(d_skip[j].astype(F32), SSM_HEAD_DIM).reshape(1, inner)
    gn_w = w_gnorm[j].astype(F32).reshape(1, inner)
    xa = _mconv_pair(xbc, state_mconv[j], w_mconv[j], b_mconv[j], seq, dec_seq)
    hw = SSM_CONV - 1
    mconv_p = xbc[0].reshape(bp, seq, conv_dim)[:, seq - hw:, :]
    mconv_s = xbc[1].reshape(bs, dec_seq, conv_dim)[:, dec_seq - hw:, :]
    yp, ssm_p = _ssd(xa[0], z[0], dtp[0], a_pad, d_exp, gn_w, None, bp, seq)
    init = state_ssm[j].astype(F32).reshape(bs, G, rg * SSM_HEAD_DIM, D_STATE)
    ysm, ssm_s = _ssd(xa[1], z[1], dtp[1], a_pad, d_exp, gn_w, init, bs, dec_seq)
    x = _mm_pair([(yp, ysm)], [w_out_ssm[j].astype(BF16)], res=x, name="out_ssm")
    x = _ffn(x, norm_ffn[1], w_gate[1], w_up[1], w_down[1])

    y = _rmsnorm_pair(x, norm_final, F32)
    st_shape = (n_heads, SSM_HEAD_DIM, D_STATE)
    return (y[0].reshape(bp, seq, d), y[1].reshape(bs, dec_seq, d),
            k[0].reshape(1, bp, seq, nh, dh), v[0].reshape(1, bp, seq, nh, dh), logf_p[None],
            sconv_p[None], mconv_p[None], ssm_p.reshape((1, bp) + st_shape),
            k[1].reshape(1, bs, dec_seq, nh, dh), v[1].reshape(1, bs, dec_seq, nh, dh), logf_s[None],
            sconv_s[None], mconv_s[None], ssm_s.reshape((1, bs) + st_shape))
```

```python
import functools

import jax
import jax.numpy as jnp
from jax import lax
from jax.experimental import pallas as pl
from jax.experimental.pallas import tpu as pltpu

F32 = jnp.float32
BF16 = jnp.bfloat16

SC_WIDTH = 3
ATT_HEADS = 8
ATT_HEAD_DIM = 128
SSM_HEAD_DIM = 64
SSM_GROUPS = 8
D_STATE = 128
SSM_CONV = 4
SSM_CHUNK = 128
EPS = 1e-6

LANES = 128
SUBLANES = 8
VMEM_LIMIT_BYTES = 56 * 1024 * 1024
NEG_BIG = -1e30

HIGHEST = lax.Precision.HIGHEST
NT_DIMS = (((1,), (1,)), ((), ()))
TN_DIMS = (((0,), (0,)), ((), ()))


def _params(n_grid_dims):
    return pltpu.CompilerParams(dimension_semantics=("arbitrary",) * n_grid_dims,
                                vmem_limit_bytes=VMEM_LIMIT_BYTES)


def _row_tile(mp, pref):
    t = pref
    while mp % t:
        t //= 2
    assert t >= SUBLANES, (mp, pref)
    return t


def _rmsnorm_body(nt, xp, xs, w, op, os):
    i = pl.program_id(0)

    def norm(x):
        y = x * lax.rsqrt(jnp.mean(x * x, axis=-1, keepdims=True) + EPS)
        return (y * w[...]).astype(op.dtype)

    @pl.when(i < nt)
    def _():
        op[...] = norm(xp[...])

    @pl.when(i == nt)
    def _():
        os[...] = norm(xs[...])


def _rmsnorm_pair(x, w, out_dtype):
    xp, xs = x
    mp, d = xp.shape
    ms = xs.shape[0]
    tm = _row_tile(mp, 512)
    nt = mp // tm
    pidx = lambda i: (jnp.minimum(i, nt - 1), 0)
    return pl.pallas_call(
        functools.partial(_rmsnorm_body, nt),
        grid=(nt + 1,),
        in_specs=[pl.BlockSpec((tm, d), pidx), pl.BlockSpec((ms, d), lambda i: (0, 0)),
                  pl.BlockSpec((1, d), lambda i: (0, 0))],
        out_specs=[pl.BlockSpec((tm, d), pidx), pl.BlockSpec((ms, d), lambda i: (0, 0))],
        out_shape=[jax.ShapeDtypeStruct((mp, d), out_dtype), jax.ShapeDtypeStruct((ms, d), out_dtype)],
        compiler_params=_params(1), name="rmsnorm",
    )(xp, xs, w.reshape(1, d).astype(F32))


def _softplus(x):
    return jnp.maximum(x, 0.0) + jnp.log1p(jnp.exp(-jnp.abs(x)))


def _silu(x):
    return x * (1.0 / (1.0 + jnp.exp(-x)))


def _split3(v):
    hi = v.astype(BF16)
    r1 = v - hi.astype(F32)
    mid = r1.astype(BF16)
    lo = (r1 - mid.astype(F32)).astype(BF16)
    return hi, mid, lo


class _W:
    def __init__(self, arr, layer=None, row0=0, k=None, col0=0, n=None):
        self.arr, self.layer, self.row0, self.col0 = arr, layer, row0, col0
        self.k = arr.shape[-2] if k is None else k
        self.n = arr.shape[-1] if n is None else n

    def spec(self, tn):
        assert self.row0 % self.k == 0 and self.col0 % tn == 0
        rb, cb = self.row0 // self.k, self.col0 // tn
        if self.layer is None:
            return pl.BlockSpec((self.k, tn), lambda j, i: (rb, cb + j))
        layer = self.layer
        return pl.BlockSpec((None, self.k, tn), lambda j, i: (layer, rb, cb + j))


def _mm_body(nt, n_lhs, mode, has_res, has_bias, scale, w_is_f32, *refs):
    i = pl.program_id(1)
    xrefs = refs[:2 * n_lhs]
    n_w = len(w_is_f32)
    wrefs = list(refs[2 * n_lhs:2 * n_lhs + n_w])
    pos = 2 * n_lhs + n_w
    rp = rs = bias = None
    if has_res:
        rp, rs = refs[pos], refs[pos + 1]
        pos += 2
    if has_bias:
        bias = refs[pos]
        pos += 1
    op, os = refs[pos], refs[pos + 1]
    scratch = list(refs[pos + 2:])

    for idx in range(n_w):
        if w_is_f32[idx]:
            wsrc, wdst = wrefs[idx], scratch.pop(0)

            @pl.when(i == 0)
            def _(wsrc=wsrc, wdst=wdst):
                wdst[...] = wsrc[...].astype(BF16)

            wrefs[idx] = wdst

    def compute(which, out, res):
        if mode == "swiglu":
            x = xrefs[which][...]
            g = jnp.dot(x, wrefs[0][...], preferred_element_type=F32)
            u = jnp.dot(x, wrefs[1][...], preferred_element_type=F32)
            acc = _silu(g) * u
        else:
            acc = jnp.dot(xrefs[which][...], wrefs[0][...], preferred_element_type=F32)
            for k in range(1, n_lhs):
                acc += jnp.dot(xrefs[2 * k + which][...], wrefs[k][...], preferred_element_type=F32)
        if scale is not None:
            acc = acc * scale
        if has_bias:
            acc = acc + bias[...]
        if mode == "softplus":
            acc = _softplus(acc)
        if has_res:
            acc = res[...] + acc
        out[...] = acc.astype(out.dtype)

    @pl.when(i < nt)
    def _():
        compute(0, op, rp)

    @pl.when(i == nt)
    def _():
        compute(1, os, rs)


def _mm_pair(xs_list, ws_list, *, out_dtype=F32, mode="plain", res=None, bias=None, scale=None, name="matmul"):
    n_lhs = len(xs_list)
    mp = xs_list[0][0].shape[0]
    ms = xs_list[0][1].shape[0]
    n = ws_list[0].n
    kmax = max(w.k for w in ws_list)
    tm = _row_tile(mp, 1024 if kmax <= 4096 else 512)
    nt = mp // tm
    if n % 1024 == 0 and kmax <= 2048 and mode != "swiglu":
        tn = 1024
    elif n % 512 == 0:
        tn = 512
    else:
        assert n % LANES == 0 and n <= 512, n
        tn = n
    pidx_x = lambda j, i: (jnp.minimum(i, nt - 1), 0)
    pidx_o = lambda j, i: (jnp.minimum(i, nt - 1), j)
    in_specs, args = [], []
    for xp, xs in xs_list:
        kk = xp.shape[1]
        in_specs += [pl.BlockSpec((tm, kk), pidx_x), pl.BlockSpec((ms, kk), lambda j, i: (0, 0))]
        args += [xp, xs]
    w_is_f32, scratch = [], []
    for w in ws_list:
        in_specs.append(w.spec(tn))
        args.append(w.arr)
        w_is_f32.append(w.arr.dtype != BF16)
        if w_is_f32[-1]:
            scratch.append(pltpu.VMEM((w.k, tn), BF16))
    if res is not None:
        in_specs += [pl.BlockSpec((tm, tn), pidx_o), pl.BlockSpec((ms, tn), lambda j, i: (0, j))]
        args += [res[0], res[1]]
    if bias is not None:
        in_specs.append(pl.BlockSpec((1, tn), lambda j, i: (0, j)))
        args.append(bias.reshape(1, n).astype(F32))
    outs = pl.pallas_call(
        functools.partial(_mm_body, nt, n_lhs, mode, res is not None, bias is not None, scale, tuple(w_is_f32)),
        grid=(n // tn, nt + 1),
        in_specs=in_specs,
        out_specs=[pl.BlockSpec((tm, tn), pidx_o), pl.BlockSpec((ms, tn), lambda j, i: (0, j))],
        out_shape=[jax.ShapeDtypeStruct((mp, n), out_dtype), jax.ShapeDtypeStruct((ms, n), out_dtype)],
        scratch_shapes=scratch,
        compiler_params=_params(2), name=name,
    )(*args)
    return outs[0], outs[1]


def _log_sigmoid(x):
    return -_softplus(-x)


def _forget_body(nb, seq, dec_seq, ck, xp, xs, w, b, lp, fp, ls, fs):
    i = pl.program_id(0)

    @pl.when(i < nb)
    def _():
        r = lax.broadcasted_iota(jnp.int32, (ck, ck), 0)
        c = lax.broadcasted_iota(jnp.int32, (ck, ck), 1)
        tri = (r >= c).astype(F32)
        carry = jnp.zeros((1, LANES), F32)
        for t in range(seq // ck):
            f = jnp.dot(xp[t * ck:(t + 1) * ck, :], w[...], preferred_element_type=F32) + b[...]
            lf = _log_sigmoid(f)
            lp[t * ck:(t + 1) * ck, :] = lf
            cs = jnp.dot(tri, lf, preferred_element_type=F32, precision=HIGHEST) + carry
            fp[t * ck:(t + 1) * ck, :] = cs
            carry = cs[ck - 1:ck, :]

    @pl.when(i == nb)
    def _():
        ms = xs.shape[0]
        r = lax.broadcasted_iota(jnp.int32, (ms, ms), 0)
        c = lax.broadcasted_iota(jnp.int32, (ms, ms), 1)
        tri = ((r // dec_seq == c // dec_seq) & (r >= c)).astype(F32)
        f = jnp.dot(xs[...], w[...], preferred_element_type=F32) + b[...]
        lf = _log_sigmoid(f)
        ls[...] = lf
        fs[...] = jnp.dot(tri, lf, preferred_element_type=F32, precision=HIGHEST)


def _forget_pair(h, w_f, b_f, nb, seq, dec_seq):
    hp, hs = h
    mp, d = hp.shape
    ms = hs.shape[0]
    assert ms <= 512 and ms % SUBLANES == 0
    ck = _row_tile(seq, 256)
    n_h = w_f.shape[1]
    wpad = jnp.zeros((d, LANES), BF16).at[:, :n_h].set(w_f.astype(BF16))
    bpad = jnp.zeros((1, LANES), F32).at[0, :n_h].set(b_f.astype(F32))
    pidx = lambda i: (jnp.minimum(i, nb - 1), 0)
    cidx = lambda i: (0, 0)
    return pl.pallas_call(
        functools.partial(_forget_body, nb, seq, dec_seq, ck),
        grid=(nb + 1,),
        in_specs=[pl.BlockSpec((seq, d), pidx), pl.BlockSpec((ms, d), cidx),
                  pl.BlockSpec((d, LANES), cidx), pl.BlockSpec((1, LANES), cidx)],
        out_specs=[pl.BlockSpec((seq, LANES), pidx), pl.BlockSpec((seq, LANES), pidx),
                   pl.BlockSpec((ms, LANES), cidx), pl.BlockSpec((ms, LANES), cidx)],
        out_shape=[jax.ShapeDtypeStruct((mp, LANES), F32), jax.ShapeDtypeStruct((mp, LANES), F32),
                   jax.ShapeDtypeStruct((ms, LANES), F32), jax.ShapeDtypeStruct((ms, LANES), F32)],
        compiler_params=_params(1), name="forget_gates",
    )(hp, hs, wpad, bpad)


def _shifted(x, hist, k, width, t_in_seq):
    n = x.shape[0]
    xk = pltpu.roll(x, k, axis=0)
    if hist is None:
        return xk
    back = (width - 1 - k) % n
    hk = hist if back == 0 else pltpu.roll(hist, n - back, axis=0)
    return jnp.where(t_in_seq >= k, xk, hk)


def _conv_rows(x, hist, wref, width, t_in_seq):
    acc = x * wref[width - 1:width, :]
    for k in range(1, width):
        acc += _shifted(x, hist, k, width, t_in_seq) * wref[width - 1 - k:width - k, :]
    return acc


def _conv_body(nt, tiles_per_seq, width, kind, dec_seq, *refs):
    if kind == "sconv":
        (gbp, gcp, up, gbs, gcs, us, hs, w, yp, ys, zp_tail, zs, carry) = refs
    else:
        (xp, xs, hs, w, b, yp, ys, carry) = refs
    i = pl.program_id(1)

    def finish(conv, gate):
        if kind == "sconv":
            return gate * conv
        return _silu(conv + b[...])

    @pl.when(i < nt)
    def _():
        t0 = i % tiles_per_seq
        x = gcp[...] * up[...] if kind == "sconv" else xp[...]
        ts = x.shape[0]

        @pl.when(t0 == 0)
        def _():
            carry[...] = jnp.zeros_like(carry)

        y = _conv_rows(x, None, w, width, None)
        gate = gbp[...] if kind == "sconv" else None
        yp[...] = finish(y, gate).astype(yp.dtype)
        head = x[:SUBLANES, :]
        t8 = lax.broadcasted_iota(jnp.int32, head.shape, 0)
        y8 = _conv_rows(head, carry[...], w, width, t8)
        yp[:SUBLANES, :] = finish(y8, gate[:SUBLANES, :] if kind == "sconv" else None).astype(yp.dtype)
        tail = x[ts - SUBLANES:, :]
        carry[...] = pltpu.roll(tail, width - 1, axis=0)
        if kind == "sconv":
            zp_tail[0] = tail

    @pl.when(i == nt)
    def _():
        x = gcs[...] * us[...] if kind == "sconv" else xs[...]
        t = lax.broadcasted_iota(jnp.int32, x.shape, 0) % dec_seq
        y = _conv_rows(x, hs[...], w, width, t)
        ys[...] = finish(y, gbs[...] if kind == "sconv" else None).astype(ys.dtype)
        if kind == "sconv":
            zs[...] = x


def _pad_hist(state, dec_seq):
    b, hw, c = state.shape
    assert dec_seq >= hw
    return jnp.concatenate([state.astype(F32), jnp.zeros((b, dec_seq - hw, c), F32)], axis=1).reshape(b * dec_seq, c)


def _sconv_pair(gcu, state, w, nb, seq, dec_seq):
    gp, gs = gcu
    mp = gp.shape[0]
    ms = gs.shape[0]
    ch = gp.shape[1] // 3
    tc = 512
    ncb = ch // tc
    ts = _row_tile(seq, 512)
    tps = seq // ts
    nt = mp // ts
    pr = lambda j, i: jnp.minimum(i, nt - 1)
    in_specs = []
    for part in range(3):
        in_specs.append(pl.BlockSpec((ts, tc), lambda j, i, part=part: (pr(j, i), part * ncb + j)))
    for part in range(3):
        in_specs.append(pl.BlockSpec((ms, tc), lambda j, i, part=part: (0, part * ncb + j)))
    in_specs += [pl.BlockSpec((ms, tc), lambda j, i: (0, j)), pl.BlockSpec((SC_WIDTH, tc), lambda j, i: (0, j))]
    out_specs = [pl.BlockSpec((ts, tc), lambda j, i: (pr(j, i), j)),
                 pl.BlockSpec((ms, tc), lambda j, i: (0, j)),
                 pl.BlockSpec((1, SUBLANES, tc), lambda j, i: (pr(j, i) // tps, 0, j)),
                 pl.BlockSpec((ms, tc), lambda j, i: (0, j))]
    out_shape = [jax.ShapeDtypeStruct((mp, ch), BF16), jax.ShapeDtypeStruct((ms, ch), BF16),
                 jax.ShapeDtypeStruct((nb, SUBLANES, ch), F32), jax.ShapeDtypeStruct((ms, ch), F32)]
    yp, ys, zp_tail, zs = pl.pallas_call(
        functools.partial(_conv_body, nt, tps, SC_WIDTH, "sconv", dec_seq),
        grid=(ncb, nt + 1), in_specs=in_specs, out_specs=out_specs, out_shape=out_shape,
        scratch_shapes=[pltpu.VMEM((SUBLANES, tc), F32)],
        compiler_params=_params(2), name="short_conv",
    )(gp, gp, gp, gs, gs, gs, _pad_hist(state, dec_seq), w.astype(F32))
    hw = SC_WIDTH - 1
    new_p = zp_tail[:, SUBLANES - hw:, :]
    new_s = zs.reshape(ms // dec_seq, dec_seq, ch)[:, dec_seq - hw:, :]
    return (yp, ys), new_p, new_s


def _mconv_pair(xbc, state, w, b, seq, dec_seq):
    xp, xs = xbc
    mp, ch = xp.shape
    ms = xs.shape[0]
    tc = 512
    ncb = ch // tc
    ts = _row_tile(seq, 512)
    tps = seq // ts
    nt = mp // ts
    pr = lambda j, i: jnp.minimum(i, nt - 1)
    in_specs = [pl.BlockSpec((ts, tc), lambda j, i: (pr(j, i), j)),
                pl.BlockSpec((ms, tc), lambda j, i: (0, j)),
                pl.BlockSpec((ms, tc), lambda j, i: (0, j)),
                pl.BlockSpec((SSM_CONV, tc), lambda j, i: (0, j)),
                pl.BlockSpec((1, tc), lambda j, i: (0, j))]
    out_specs = [pl.BlockSpec((ts, tc), lambda j, i: (pr(j, i), j)), pl.BlockSpec((ms, tc), lambda j, i: (0, j))]
    out_shape = [jax.ShapeDtypeStruct((mp, ch), F32), jax.ShapeDtypeStruct((ms, ch), F32)]
    yp, ys = pl.pallas_call(
        functools.partial(_conv_body, nt, tps, SSM_CONV, "mconv", dec_seq),
        grid=(ncb, nt + 1), in_specs=in_specs, out_specs=out_specs, out_shape=out_shape,
        scratch_shapes=[pltpu.VMEM((SUBLANES, tc), F32)],
        compiler_params=_params(2), name="mamba_conv",
    )(xp, xs, _pad_hist(state, dec_seq), w.astype(F32), b.reshape(1, ch).astype(F32))
    return yp, ys


def _fox_prompt_body(tq, hps, q, k, v, fc, fr, o, m_s, l_s, acc_s):
    qi = pl.program_id(2)
    dh = ATT_HEAD_DIM
    m_s[...] = jnp.full_like(m_s, NEG_BIG)
    l_s[...] = jnp.zeros_like(l_s)
    acc_s[...] = jnp.zeros_like(acc_s)

    def step(ki, masked):
        start = pl.multiple_of(ki * tq, tq)
        for hh in range(hps):
            cols = slice(hh * dh, (hh + 1) * dh)
            kb = k[pl.ds(start, tq), cols].astype(BF16)
            vb = v[pl.ds(start, tq), cols].astype(BF16)
            s = lax.dot_general(q[:, cols], kb, NT_DIMS, preferred_element_type=F32)
            s = s + fc[0, hh] - fr[0, hh, ki]
            if masked:
                r = lax.broadcasted_iota(jnp.int32, s.shape, 0)
                c = lax.broadcasted_iota(jnp.int32, s.shape, 1)
                s = jnp.where(r >= c, s, NEG_BIG)
            m_prev = m_s[hh]
            m_new = jnp.maximum(m_prev, jnp.max(s, axis=-1, keepdims=True))
            p = jnp.exp(s - m_new)
            alpha = jnp.exp(m_prev - m_new)
            l_s[hh] = alpha * l_s[hh] + jnp.sum(p, axis=-1, keepdims=True)
            acc_s[hh] = alpha * acc_s[hh] + jnp.dot(p.astype(BF16), vb, preferred_element_type=F32)
            m_s[hh] = m_new

    def loop_body(ki, carry):
        step(ki, False)
        return carry

    lax.fori_loop(0, qi, loop_body, 0)
    step(qi, True)
    for hh in range(hps):
        o[:, hh * dh:(hh + 1) * dh] = (acc_s[hh] / l_s[hh]).astype(o.dtype)


def _fox_prompt(q, k, v, fcum, nb, seq):
    mp = q.shape[0]
    dh = ATT_HEAD_DIM
    nh = q.shape[1] // dh
    tq = _row_tile(seq, 512)
    nq = seq // tq
    ft = jnp.transpose(fcum, (0, 2, 1))
    fcol = ft[..., None]
    frow = ft.reshape(nb, nh, nq, 1, tq)
    hps = 2 if nh % 2 == 0 else 1
    gw = hps * dh
    return pl.pallas_call(
        functools.partial(_fox_prompt_body, tq, hps),
        grid=(nb, nh // hps, nq),
        in_specs=[pl.BlockSpec((tq, gw), lambda b, h, i: (b * nq + i, h)),
                  pl.BlockSpec((seq, gw), lambda b, h, i: (b, h)),
                  pl.BlockSpec((seq, gw), lambda b, h, i: (b, h)),
                  pl.BlockSpec((1, hps, tq, 1), lambda b, h, i: (b, h, i, 0)),
                  pl.BlockSpec((1, hps, nq, 1, tq), lambda b, h, i: (b, h, 0, 0, 0))],
        out_specs=pl.BlockSpec((tq, gw), lambda b, h, i: (b * nq + i, h)),
        out_shape=jax.ShapeDtypeStruct((mp, nh * dh), BF16),
        scratch_shapes=[pltpu.VMEM((hps, tq, 1), F32), pltpu.VMEM((hps, tq, 1), F32),
                        pltpu.VMEM((hps, tq, dh), F32)],
        compiler_params=_params(3), name="fox_prompt",
    )(q, k, v, fcol, frow)


def _past_bias_body(n_pages, page, nh, *refs):
    lf_refs = refs[1:1 + n_pages]
    out = refs[1 + n_pages]
    rows, later_in_page, same_head = refs[2 + n_pages:]
    w = page * nh
    npp = rows.shape[0]

    @pl.when(pl.program_id(0) == 0)
    def _():
        a = lax.broadcasted_iota(jnp.int32, (w, w), 0)
        c = lax.broadcasted_iota(jnp.int32, (w, w), 1)
        same = (a % nh) == (c % nh)
        same_head[...] = same.astype(BF16)
        later_in_page[...] = (same & ((a // nh) > (c // nh))).astype(BF16)
        rows[...] = jnp.zeros_like(rows)

    for p in range(n_pages):
        rows[p:p + 1, :] = lf_refs[p][0]
    lf3 = jnp.concatenate(_split3(rows[...]), axis=0)
    within3 = jnp.dot(lf3, later_in_page[...], preferred_element_type=F32)
    totals3 = jnp.dot(lf3, same_head[...], preferred_element_type=F32)
    within = within3[:npp] + within3[npp:2 * npp] + within3[2 * npp:]
    totals = totals3[:npp] + totals3[npp:2 * npp] + totals3[2 * npp:]
    pr = lax.broadcasted_iota(jnp.int32, (npp, npp), 0)
    pc = lax.broadcasted_iota(jnp.int32, (npp, npp), 1)
    later_pages = (pc > pr).astype(BF16)
    after3 = jnp.dot(later_pages, jnp.concatenate(_split3(totals), axis=1), preferred_element_type=F32)
    after = after3[:, :w] + after3[:, w:2 * w] + after3[:, 2 * w:]
    out[0] = (within + after)[:n_pages, :]


def _past_bias(cache_lf, page_table):
    n_phys, page, nh = cache_lf.shape
    nb, n_pages = page_table.shape
    w = page * nh
    np_pad = -(-n_pages // LANES) * LANES
    lf = cache_lf.reshape(n_phys, 1, w)
    in_specs = [pl.BlockSpec((1, 1, w), lambda b, pt, p=p: (pt[b, p], 0, 0)) for p in range(n_pages)]
    grid_spec = pltpu.PrefetchScalarGridSpec(
        num_scalar_prefetch=1, grid=(nb,), in_specs=in_specs,
        out_specs=pl.BlockSpec((1, n_pages, w), lambda b, pt: (b, 0, 0)),
        scratch_shapes=[pltpu.VMEM((np_pad, w), F32), pltpu.VMEM((w, w), BF16), pltpu.VMEM((w, w), BF16)])
    return pl.pallas_call(
        functools.partial(_past_bias_body, n_pages, page, nh),
        grid_spec=grid_spec, out_shape=jax.ShapeDtypeStruct((nb, n_pages, w), F32),
        compiler_params=_params(1), name="fox_past_bias",
    )(page_table, *([lf] * n_pages))


def _fox_sample_body(pps, n_steps, nh, dec_seq, *refs):
    qall, knew, vnew, frow_new, rb = refs[1:6]
    krefs = refs[6:6 + pps]
    vrefs = refs[6 + pps:6 + 2 * pps]
    o = refs[6 + 2 * pps]
    m_s, l_s, acc_s, s_scr = refs[7 + 2 * pps:]
    step = pl.program_id(1)
    qv = qall[0]
    nrow = qv.shape[0]

    @pl.when(step == 0)
    def _():
        kn = knew[0].astype(BF16)
        g = lax.dot_general(qv, kn, NT_DIMS, preferred_element_type=F32)
        r = lax.broadcasted_iota(jnp.int32, g.shape, 0)
        c = lax.broadcasted_iota(jnp.int32, g.shape, 1)
        valid = ((c % nh) == (r // dec_seq)) & ((c // nh) <= (r % dec_seq))
        s = jnp.where(valid, g - frow_new[0], NEG_BIG)
        m = jnp.max(s, axis=-1, keepdims=True)
        p = jnp.exp(s - m)
        m_s[...] = m
        l_s[...] = jnp.sum(p, axis=-1, keepdims=True)
        acc_s[...] = jnp.dot(p.astype(BF16), vnew[0].astype(BF16), preferred_element_type=F32)

    lanes = krefs[0].shape[1]
    r = lax.broadcasted_iota(jnp.int32, (nrow, lanes), 0)
    c = lax.broadcasted_iota(jnp.int32, (nrow, lanes), 1)
    head_ok = (c % nh) == (r // dec_seq)
    smax = None
    for i in range(pps):
        g = lax.dot_general(qv, krefs[i][0].astype(BF16), NT_DIMS, preferred_element_type=F32)
        s = jnp.where(head_ok, g + rb[0, i], NEG_BIG)
        s_scr[i] = s
        smax = s if smax is None else jnp.maximum(smax, s)
    m_prev = m_s[...]
    m_new = jnp.maximum(m_prev, jnp.max(smax, axis=-1, keepdims=True))
    alpha = jnp.exp(m_prev - m_new)
    psum = acc = None
    for i in range(pps):
        p = jnp.exp(s_scr[i] - m_new)
        pv = jnp.dot(p.astype(BF16), vrefs[i][0].astype(BF16), preferred_element_type=F32)
        psum = p if psum is None else psum + p
        acc = pv if acc is None else acc + pv
    l_s[...] = alpha * l_s[...] + jnp.sum(psum, axis=-1, keepdims=True)
    acc_s[...] = alpha * acc_s[...] + acc
    m_s[...] = m_new

    @pl.when(step == n_steps - 1)
    def _():
        o[0] = (acc_s[...] / l_s[...]).astype(o.dtype)


def _fox_sample(q, k, v, fn, cache_k, cache_v, cache_lf, page_table, dec_seq):
    ms = q.shape[0]
    nb = ms // dec_seq
    nh, dh = ATT_HEADS, ATT_HEAD_DIM
    n_phys, page = cache_k.shape[0], cache_k.shape[1]
    n_pages = page_table.shape[1]
    pps = 8
    while n_pages % pps:
        pps //= 2
    n_steps = n_pages // pps
    nrow = nh * dec_seq
    rbias = _past_bias(cache_lf, page_table).reshape(nb, n_pages, 1, page * nh)
    qall = jnp.transpose(q.reshape(nb, dec_seq, nh, dh), (0, 2, 1, 3)).reshape(nb, nrow, dh)
    knew = k.reshape(nb, dec_seq * nh, dh)
    vnew = v.reshape(nb, dec_seq * nh, dh)
    frow_new = fn.reshape(nb, 1, dec_seq * nh)
    ck = cache_k.reshape(n_phys, page * nh, dh)
    cv = cache_v.reshape(n_phys, page * nh, dh)
    per_b = lambda shape: pl.BlockSpec((1,) + shape, lambda b, s, pt: (b, 0, 0))
    in_specs = [per_b((nrow, dh)), per_b((dec_seq * nh, dh)), per_b((dec_seq * nh, dh)),
                per_b((1, dec_seq * nh)),
                pl.BlockSpec((1, pps, 1, page * nh), lambda b, s, pt: (b, s, 0, 0))]
    for _ in range(2):
        in_specs += [pl.BlockSpec((1, page * nh, dh), lambda b, s, pt, i=i: (pt[b, s * pps + i], 0, 0))
                     for i in range(pps)]
    grid_spec = pltpu.PrefetchScalarGridSpec(
        num_scalar_prefetch=1, grid=(nb, n_steps), in_specs=in_specs,
        out_specs=pl.BlockSpec((1, nrow, dh), lambda b, s, pt: (b, 0, 0)),
        scratch_shapes=[pltpu.VMEM((nrow, 1), F32), pltpu.VMEM((nrow, 1), F32), pltpu.VMEM((nrow, dh), F32),
                        pltpu.VMEM((pps, nrow, page * nh), F32)])
    o = pl.pallas_call(
        functools.partial(_fox_sample_body, pps, n_steps, nh, dec_seq),
        grid_spec=grid_spec, out_shape=jax.ShapeDtypeStruct((nb, nrow, dh), BF16),
        compiler_params=_params(2), name="fox_sample",
    )(page_table, qall, knew, vnew, frow_new, rbias, *([ck] * pps), *([cv] * pps))
    return jnp.transpose(o.reshape(nb, nh, dec_seq, dh), (0, 2, 1, 3)).reshape(ms, nh * dh)


def _ssd_body(t_in, has_init, xs_r, z_r, b_r, c_r, dt_r, a_r, d_r, gn_r, ex_r, *rest):
    if has_init:
        s0_r, y_o, st_o, st = rest
    else:
        y_o, st_o, st = rest
    ci = pl.program_id(2)
    nc = pl.num_programs(2)
    L = SSM_CHUNK
    P = SSM_HEAD_DIM
    R = xs_r.shape[2] // P

    @pl.when(ci == 0)
    def _():
        if has_init:
            st[...] = s0_r[0, 0]
        else:
            st[...] = jnp.zeros_like(st)

    def rows(ref):
        v = ref[0]
        if t_in < L:
            v = jnp.concatenate([v, jnp.zeros((L - t_in, v.shape[1]), v.dtype)], axis=0)
        return v

    xs = rows(xs_r)
    bm = rows(b_r).astype(BF16)
    cm = rows(c_r).astype(BF16)
    dt = rows(dt_r)
    r = lax.broadcasted_iota(jnp.int32, (L, L), 0)
    c = lax.broadcasted_iota(jnp.int32, (L, L), 1)
    tri = r >= c
    acum = jnp.dot(tri.astype(F32), dt * a_r[...], preferred_element_type=F32, precision=HIGHEST)
    acum_t = acum.T
    dt_t = dt.T
    a_last = acum[L - 1:L, :]
    cols = jnp.concatenate([dt * jnp.exp(a_last - acum), jnp.exp(acum)], axis=0)
    wide = jnp.dot(jnp.concatenate(_split3(cols), axis=1), ex_r[...], preferred_element_type=F32)
    w_state, w_off = wide[:L], wide[L:]
    cb = lax.dot_general(cm, bm, NT_DIMS, preferred_element_type=F32)
    s_prev = st[...]
    y_off = lax.dot_general(cm, s_prev.astype(BF16), NT_DIMS, preferred_element_type=F32) * w_off
    xb = xs.astype(BF16)
    ys = []
    for h in range(R):
        decay = jnp.exp(jnp.where(tri, acum[:, h:h + 1] - acum_t[h:h + 1, :], -jnp.inf))
        m = (cb * decay * dt_t[h:h + 1, :]).astype(BF16)
        ys.append(jnp.dot(m, xb[:, h * P:(h + 1) * P], preferred_element_type=F32))
    new_states = lax.dot_general((xs * w_state).astype(BF16), bm, TN_DIMS, preferred_element_type=F32)
    for h in range(R):
        st[h * P:(h + 1) * P, :] = (s_prev[h * P:(h + 1) * P, :] * jnp.exp(acum_t[h:h + 1, L - 1:L])
                                    + new_states[h * P:(h + 1) * P, :])
    y = jnp.concatenate(ys, axis=1) + y_off + d_r[...] * xs
    y = y[:t_in, :] * _silu(z_r[0])
    y = y * lax.rsqrt(jnp.mean(y * y, axis=-1, keepdims=True) + EPS)
    y_o[0] = (y * gn_r[...]).astype(y_o.dtype)

    @pl.when(ci == nc - 1)
    def _():
        st_o[0, 0] = st[...]


def _ssd(xa, z, dtp, a_pad, d_exp, gn_w, init_state, nb, seq):
    G, N, P = SSM_GROUPS, D_STATE, SSM_HEAD_DIM
    inner = z.shape[1]
    gw = inner // G
    L = SSM_CHUNK
    t_in = min(seq, L)
    assert seq % t_in == 0 and (seq <= L or seq % L == 0)
    nc = seq // t_in
    xa3 = xa.reshape(nb, seq, xa.shape[1])
    z3 = z.reshape(nb, seq, inner)
    dt3 = dtp.reshape(nb, seq, G * LANES)
    has_init = init_state is not None
    in_specs = [pl.BlockSpec((1, t_in, gw), lambda b, g, c: (b, c, g)),
                pl.BlockSpec((1, t_in, gw), lambda b, g, c: (b, c, g)),
                pl.BlockSpec((1, t_in, N), lambda b, g, c: (b, c, inner // N + g)),
                pl.BlockSpec((1, t_in, N), lambda b, g, c: (b, c, inner // N + G + g)),
                pl.BlockSpec((1, t_in, LANES), lambda b, g, c: (b, c, g)),
                pl.BlockSpec((1, LANES), lambda b, g, c: (0, g)),
                pl.BlockSpec((1, gw), lambda b, g, c: (0, g)),
                pl.BlockSpec((1, gw), lambda b, g, c: (0, g))]
    lane_head = jnp.arange(3 * LANES)[:, None] % LANES
    widen = (lane_head == jnp.arange(gw)[None, :] // P).astype(BF16)
    in_specs.append(pl.BlockSpec((3 * LANES, gw), lambda b, g, c: (0, 0)))
    args = [xa3, z3, xa3, xa3, dt3, a_pad, d_exp, gn_w, widen]
    if has_init:
        in_specs.append(pl.BlockSpec((1, 1, gw, N), lambda b, g, c: (b, g, 0, 0)))
        args.append(init_state)
    y, st = pl.pallas_call(
        functools.partial(_ssd_body, t_in, has_init),
        grid=(nb, G, nc), in_specs=in_specs,
        out_specs=[pl.BlockSpec((1, t_in, gw), lambda b, g, c: (b, c, g)),
                   pl.BlockSpec((1, 1, gw, N), lambda b, g, c: (b, g, 0, 0))],
        out_shape=[jax.ShapeDtypeStruct((nb, seq, inner), BF16), jax.ShapeDtypeStruct((nb, G, gw, N), F32)],
        scratch_shapes=[pltpu.VMEM((gw, N), F32)],
        compiler_params=_params(3), name="ssd",
    )(*args)
    return y.reshape(nb * seq, inner), st


def _ffn(x, norm_w, w_gate, w_up, w_down, layer):
    h = _rmsnorm_pair(x, norm_w, BF16)
    a = _mm_pair([h], [_W(w_gate, layer), _W(w_up, layer)], out_dtype=BF16, mode="swiglu", name="ffn_gate_up")
    return _mm_pair([a], [_W(w_down, layer)], res=x, name="ffn_down")


def kernel(x_prompt, x_sample, cache_k, cache_v, cache_logf, page_table, state_sconv, state_mconv, state_ssm, norm_mix, norm_ffn, norm_final, w_in_even, b_forget, w_sconv, w_out_even, w_in_ssm, w_mconv, b_mconv, dt_bias, a_log, d_skip, w_gnorm, w_out_ssm, w_gate, w_up, w_down):
    bp, seq, d = x_prompt.shape
    bs, dec_seq, _ = x_sample.shape
    mp, ms = bp * seq, bs * dec_seq
    nh, dh = ATT_HEADS, ATT_HEAD_DIM
    att = nh * dh
    sc_ch = state_sconv.shape[-1]
    x = (x_prompt.reshape(mp, d), x_sample.reshape(ms, d))

    i = 0
    h = _rmsnorm_pair(x, norm_mix[0], BF16)
    c0 = 3 * sc_ch
    gcu = _mm_pair([h], [_W(w_in_even, i, n=c0)], name="in_even_conv")
    q = _mm_pair([h], [_W(w_in_even, i, col0=c0, n=att)], out_dtype=BF16, scale=dh ** -0.5, name="in_even_q")
    k = _mm_pair([h], [_W(w_in_even, i, col0=c0 + att, n=att)], name="in_even_k")
    v = _mm_pair([h], [_W(w_in_even, i, col0=c0 + 2 * att, n=att)], name="in_even_v")
    lf_p, fc_p, lf_s, fc_s = _forget_pair(h, w_in_even[i, :, c0 + 3 * att:], b_forget[i], bp, seq, dec_seq)
    logf_p = lf_p[:, :nh].reshape(bp, seq, nh)
    logf_s = lf_s[:, :nh].reshape(bs, dec_seq, nh)
    y_a, sconv_p, sconv_s = _sconv_pair(gcu, state_sconv[i], w_sconv[i], bp, seq, dec_seq)
    yb_p = _fox_prompt(q[0], k[0], v[0], fc_p[:, :nh].reshape(bp, seq, nh), bp, seq)
    yb_s = _fox_sample(q[1], k[1], v[1], fc_s[:, :nh].reshape(bs, dec_seq, nh),
                       cache_k[i], cache_v[i], cache_logf[i], page_table, dec_seq)
    x = _mm_pair([y_a, (yb_p, yb_s)],
                 [_W(w_out_even, i, row0=0, k=sc_ch), _W(w_out_even, i, row0=sc_ch, k=att)], res=x, name="out_even")
    x = _ffn(x, norm_ffn[0], w_gate, w_up, w_down, 0)

    j = 0
    G = SSM_GROUPS
    n_heads = a_log.shape[1]
    rg = n_heads // G
    inner = n_heads * SSM_HEAD_DIM
    conv_dim = state_mconv.shape[-1]
    h = _rmsnorm_pair(x, norm_mix[1], BF16)
    z = _mm_pair([h], [_W(w_in_ssm, j, n=inner)], name="in_ssm_z")
    xbc = _mm_pair([h], [_W(w_in_ssm, j, col0=inner, n=conv_dim)], name="in_ssm_xbc")
    w_dt = w_in_ssm[j, :, inner + conv_dim:].reshape(d, G, rg)
    w_dt = jnp.zeros((d, G, LANES), BF16).at[:, :, :rg].set(w_dt.astype(BF16)).reshape(d, G * LANES)
    pad_heads = lambda vec: jnp.zeros((G, LANES), F32).at[:, :rg].set(vec.astype(F32).reshape(G, rg)).reshape(1, G * LANES)
    dtp = _mm_pair([h], [_W(w_dt)], mode="softplus", bias=pad_heads(dt_bias[j]), name="in_ssm_dt")
    a_pad = pad_heads(-jnp.exp(a_log[j].astype(F32)))
    d_exp = jnp.repeat(d_skip[j].astype(F32), SSM_HEAD_DIM).reshape(1, inner)
    gn_w = w_gnorm[j].astype(F32).reshape(1, inner)
    xa = _mconv_pair(xbc, state_mconv[j], w_mconv[j], b_mconv[j], seq, dec_seq)
    hw = SSM_CONV - 1
    mconv_p = xbc[0].reshape(bp, seq, conv_dim)[:, seq - hw:, :]
    mconv_s = xbc[1].reshape(bs, dec_seq, conv_dim)[:, dec_seq - hw:, :]
    yp, ssm_p = _ssd(xa[0], z[0], dtp[0], a_pad, d_exp, gn_w, None, bp, seq)
    init = state_ssm[j].astype(F32).reshape(bs, G, rg * SSM_HEAD_DIM, D_STATE)
    ysm, ssm_s = _ssd(xa[1], z[1], dtp[1], a_pad, d_exp, gn_w, init, bs, dec_seq)
    x = _mm_pair([(yp, ysm)], [_W(w_out_ssm, j)], res=x, name="out_ssm")
    x = _ffn(x, norm_ffn[1], w_gate, w_up, w_down, 1)

    y = _rmsnorm_pair(x, norm_final, F32)
    st_shape = (n_heads, SSM_HEAD_DIM, D_STATE)
    return (y[0].reshape(bp, seq, d), y[1].reshape(bs, dec_seq, d),
            k[0].reshape(1, bp, seq, nh, dh), v[0].reshape(1, bp, seq, nh, dh), logf_p[None],
            sconv_p[None], mconv_p[None], ssm_p.reshape((1, bp) + st_shape),
            k[1].reshape(1, bs, dec_seq, nh, dh), v[1].reshape(1, bs, dec_seq, nh, dh), logf_s[None],
            sconv_s[None], mconv_s[None], ssm_s.reshape((1, bs) + st_shape))
```

```python
import functools

import jax
import jax.numpy as jnp
from jax import lax
from jax.experimental import pallas as pl
from jax.experimental.pallas import tpu as pltpu

F32 = jnp.float32
BF16 = jnp.bfloat16

SC_WIDTH = 3
ATT_HEADS = 8
ATT_HEAD_DIM = 128
SSM_HEAD_DIM = 64
SSM_GROUPS = 8
D_STATE = 128
SSM_CONV = 4
SSM_CHUNK = 128
EPS = 1e-6

LANES = 128
SUBLANES = 8
VMEM_LIMIT_BYTES = 56 * 1024 * 1024
NEG_BIG = -1e30

HIGHEST = lax.Precision.HIGHEST
NT_DIMS = (((1,), (1,)), ((), ()))
TN_DIMS = (((0,), (0,)), ((), ()))


def _params(n_grid_dims):
    return pltpu.CompilerParams(dimension_semantics=("arbitrary",) * n_grid_dims,
                                vmem_limit_bytes=VMEM_LIMIT_BYTES)


def _row_tile(mp, pref):
    t = pref
    while mp % t:
        t //= 2
    assert t >= SUBLANES, (mp, pref)
    return t


def _rmsnorm_body(nt, xp, xs, w, op, os):
    i = pl.program_id(0)

    def norm(x):
        y = x * lax.rsqrt(jnp.mean(x * x, axis=-1, keepdims=True) + EPS)
        return (y * w[...]).astype(op.dtype)

    @pl.when(i < nt)
    def _():
        op[...] = norm(xp[...])

    @pl.when(i == nt)
    def _():
        os[...] = norm(xs[...])


def _rmsnorm_pair(x, w, out_dtype):
    xp, xs = x
    mp, d = xp.shape
    ms = xs.shape[0]
    tm = _row_tile(mp, 512)
    nt = mp // tm
    pidx = lambda i: (jnp.minimum(i, nt - 1), 0)
    return pl.pallas_call(
        functools.partial(_rmsnorm_body, nt),
        grid=(nt + 1,),
        in_specs=[pl.BlockSpec((tm, d), pidx), pl.BlockSpec((ms, d), lambda i: (0, 0)),
                  pl.BlockSpec((1, d), lambda i: (0, 0))],
        out_specs=[pl.BlockSpec((tm, d), pidx), pl.BlockSpec((ms, d), lambda i: (0, 0))],
        out_shape=[jax.ShapeDtypeStruct((mp, d), out_dtype), jax.ShapeDtypeStruct((ms, d), out_dtype)],
        compiler_params=_params(1), name="rmsnorm",
    )(xp, xs, w.reshape(1, d).astype(F32))


def _softplus(x):
    return jnp.maximum(x, 0.0) + jnp.log1p(jnp.exp(-jnp.abs(x)))


def _silu(x):
    return x * (1.0 / (1.0 + jnp.exp(-x)))


def _split3(v):
    hi = v.astype(BF16)
    r1 = v - hi.astype(F32)
    mid = r1.astype(BF16)
    lo = (r1 - mid.astype(F32)).astype(BF16)
    return hi, mid, lo


class _W:
    def __init__(self, arr, layer=None, row0=0, k=None, col0=0, n=None, transposed=False):
        self.arr, self.layer, self.row0, self.col0, self.transposed = arr, layer, row0, col0, transposed
        kdim, ndim = (-1, -2) if transposed else (-2, -1)
        self.k = arr.shape[kdim] if k is None else k
        self.n = arr.shape[ndim] if n is None else n

    def block(self, tn):
        return (tn, self.k) if self.transposed else (self.k, tn)

    def spec(self, tn):
        assert self.row0 % self.k == 0 and self.col0 % tn == 0
        rb, cb = self.row0 // self.k, self.col0 // tn
        if self.transposed:
            idx = lambda j, i: (cb + j, rb)
        else:
            idx = lambda j, i: (rb, cb + j)
        if self.layer is None:
            return pl.BlockSpec(self.block(tn), idx)
        layer = self.layer
        return pl.BlockSpec((None,) + self.block(tn), lambda j, i: (layer,) + idx(j, i))


def _mm_body(nt, n_lhs, mode, has_res, has_bias, scale, w_is_f32, w_transposed, *refs):
    i = pl.program_id(1)

    def mm(x, idx):
        dims = NT_DIMS if w_transposed[idx] else (((1,), (0,)), ((), ()))
        return lax.dot_general(x, wrefs[idx][...], dims, preferred_element_type=F32)

    xrefs = refs[:2 * n_lhs]
    n_w = len(w_is_f32)
    wrefs = list(refs[2 * n_lhs:2 * n_lhs + n_w])
    pos = 2 * n_lhs + n_w
    rp = rs = bias = None
    if has_res:
        rp, rs = refs[pos], refs[pos + 1]
        pos += 2
    if has_bias:
        bias = refs[pos]
        pos += 1
    op, os = refs[pos], refs[pos + 1]
    scratch = list(refs[pos + 2:])

    for idx in range(n_w):
        if w_is_f32[idx]:
            wsrc, wdst = wrefs[idx], scratch.pop(0)

            @pl.when(i == 0)
            def _(wsrc=wsrc, wdst=wdst):
                wdst[...] = wsrc[...].astype(BF16)

            wrefs[idx] = wdst

    def compute(which, out, res):
        if mode == "swiglu":
            x = xrefs[which][...]
            acc = _silu(mm(x, 0)) * mm(x, 1)
        else:
            acc = mm(xrefs[which][...], 0)
            for k in range(1, n_lhs):
                acc += mm(xrefs[2 * k + which][...], k)
        if scale is not None:
            acc = acc * scale
        if has_bias:
            acc = acc + bias[...]
        if mode == "softplus":
            acc = _softplus(acc)
        if has_res:
            acc = res[...] + acc
        out[...] = acc.astype(out.dtype)

    @pl.when(i < nt)
    def _():
        compute(0, op, rp)

    @pl.when(i == nt)
    def _():
        compute(1, os, rs)


def _mm_pair(xs_list, ws_list, *, out_dtype=F32, mode="plain", res=None, bias=None, scale=None, name="matmul"):
    n_lhs = len(xs_list)
    mp = xs_list[0][0].shape[0]
    ms = xs_list[0][1].shape[0]
    n = ws_list[0].n
    kmax = max(w.k for w in ws_list)
    tm = _row_tile(mp, 1024 if kmax <= 4096 else 512)
    nt = mp // tm
    if n % 1024 == 0 and kmax <= 2048 and mode != "swiglu":
        tn = 1024
    elif n % 512 == 0:
        tn = 512
    else:
        assert n % LANES == 0 and n <= 512, n
        tn = n
    pidx_x = lambda j, i: (jnp.minimum(i, nt - 1), 0)
    pidx_o = lambda j, i: (jnp.minimum(i, nt - 1), j)
    in_specs, args = [], []
    for xp, xs in xs_list:
        kk = xp.shape[1]
        in_specs += [pl.BlockSpec((tm, kk), pidx_x), pl.BlockSpec((ms, kk), lambda j, i: (0, 0))]
        args += [xp, xs]
    w_is_f32, scratch = [], []
    for w in ws_list:
        in_specs.append(w.spec(tn))
        args.append(w.arr)
        w_is_f32.append(w.arr.dtype != BF16)
        if w_is_f32[-1]:
            scratch.append(pltpu.VMEM(w.block(tn), BF16))
    if res is not None:
        in_specs += [pl.BlockSpec((tm, tn), pidx_o), pl.BlockSpec((ms, tn), lambda j, i: (0, j))]
        args += [res[0], res[1]]
    if bias is not None:
        in_specs.append(pl.BlockSpec((1, tn), lambda j, i: (0, j)))
        args.append(bias.reshape(1, n).astype(F32))
    outs = pl.pallas_call(
        functools.partial(_mm_body, nt, n_lhs, mode, res is not None, bias is not None, scale, tuple(w_is_f32),
                          tuple(w.transposed for w in ws_list)),
        grid=(n // tn, nt + 1),
        in_specs=in_specs,
        out_specs=[pl.BlockSpec((tm, tn), pidx_o), pl.BlockSpec((ms, tn), lambda j, i: (0, j))],
        out_shape=[jax.ShapeDtypeStruct((mp, n), out_dtype), jax.ShapeDtypeStruct((ms, n), out_dtype)],
        scratch_shapes=scratch,
        compiler_params=_params(2), name=name,
    )(*args)
    return outs[0], outs[1]


def _log_sigmoid(x):
    return -_softplus(-x)


def _forget_body(nb, seq, dec_seq, ck, xp, xs, w, b, lp, fp, ls, fs):
    i = pl.program_id(0)

    @pl.when(i < nb)
    def _():
        r = lax.broadcasted_iota(jnp.int32, (ck, ck), 0)
        c = lax.broadcasted_iota(jnp.int32, (ck, ck), 1)
        tri = (r >= c).astype(F32)
        carry = jnp.zeros((1, LANES), F32)
        for t in range(seq // ck):
            f = jnp.dot(xp[t * ck:(t + 1) * ck, :], w[...], preferred_element_type=F32) + b[...]
            lf = _log_sigmoid(f)
            lp[t * ck:(t + 1) * ck, :] = lf
            cs = jnp.dot(tri, lf, preferred_element_type=F32, precision=HIGHEST) + carry
            fp[t * ck:(t + 1) * ck, :] = cs
            carry = cs[ck - 1:ck, :]

    @pl.when(i == nb)
    def _():
        ms = xs.shape[0]
        r = lax.broadcasted_iota(jnp.int32, (ms, ms), 0)
        c = lax.broadcasted_iota(jnp.int32, (ms, ms), 1)
        tri = ((r // dec_seq == c // dec_seq) & (r >= c)).astype(F32)
        f = jnp.dot(xs[...], w[...], preferred_element_type=F32) + b[...]
        lf = _log_sigmoid(f)
        ls[...] = lf
        fs[...] = jnp.dot(tri, lf, preferred_element_type=F32, precision=HIGHEST)


def _forget_pair(h, w_f, b_f, nb, seq, dec_seq):
    hp, hs = h
    mp, d = hp.shape
    ms = hs.shape[0]
    assert ms <= 512 and ms % SUBLANES == 0
    ck = _row_tile(seq, 256)
    n_h = w_f.shape[1]
    wpad = jnp.zeros((d, LANES), BF16).at[:, :n_h].set(w_f.astype(BF16))
    bpad = jnp.zeros((1, LANES), F32).at[0, :n_h].set(b_f.astype(F32))
    pidx = lambda i: (jnp.minimum(i, nb - 1), 0)
    cidx = lambda i: (0, 0)
    return pl.pallas_call(
        functools.partial(_forget_body, nb, seq, dec_seq, ck),
        grid=(nb + 1,),
        in_specs=[pl.BlockSpec((seq, d), pidx), pl.BlockSpec((ms, d), cidx),
                  pl.BlockSpec((d, LANES), cidx), pl.BlockSpec((1, LANES), cidx)],
        out_specs=[pl.BlockSpec((seq, LANES), pidx), pl.BlockSpec((seq, LANES), pidx),
                   pl.BlockSpec((ms, LANES), cidx), pl.BlockSpec((ms, LANES), cidx)],
        out_shape=[jax.ShapeDtypeStruct((mp, LANES), F32), jax.ShapeDtypeStruct((mp, LANES), F32),
                   jax.ShapeDtypeStruct((ms, LANES), F32), jax.ShapeDtypeStruct((ms, LANES), F32)],
        compiler_params=_params(1), name="forget_gates",
    )(hp, hs, wpad, bpad)


def _shifted(x, hist, k, width, t_in_seq):
    n = x.shape[0]
    xk = pltpu.roll(x, k, axis=0)
    if hist is None:
        return xk
    back = (width - 1 - k) % n
    hk = hist if back == 0 else pltpu.roll(hist, n - back, axis=0)
    return jnp.where(t_in_seq >= k, xk, hk)


def _conv_rows(x, hist, wref, width, t_in_seq):
    acc = x * wref[width - 1:width, :]
    for k in range(1, width):
        acc += _shifted(x, hist, k, width, t_in_seq) * wref[width - 1 - k:width - k, :]
    return acc


def _sconv_body(nt, tiles_per_seq, width, dec_seq, gbp, gcp, up, gbs, gcs, us, hs, w, yp, ys, zp_tail, zs, carry):
    i = pl.program_id(1)

    @pl.when(i < nt)
    def _():
        t0 = i % tiles_per_seq
        x = gcp[...] * up[...]
        ts = x.shape[0]

        @pl.when(t0 == 0)
        def _():
            carry[...] = jnp.zeros_like(carry)

        yp[...] = (gbp[...] * _conv_rows(x, None, w, width, None)).astype(yp.dtype)
        head = x[:SUBLANES, :]
        t8 = lax.broadcasted_iota(jnp.int32, head.shape, 0)
        yp[:SUBLANES, :] = (gbp[:SUBLANES, :] * _conv_rows(head, carry[...], w, width, t8)).astype(yp.dtype)
        tail = x[ts - SUBLANES:, :]
        carry[...] = pltpu.roll(tail, width - 1, axis=0)
        zp_tail[0] = tail

    @pl.when(i == nt)
    def _():
        x = gcs[...] * us[...]
        t = lax.broadcasted_iota(jnp.int32, x.shape, 0) % dec_seq
        ys[...] = (gbs[...] * _conv_rows(x, hs[...], w, width, t)).astype(ys.dtype)
        zs[...] = x


def _pad_hist(state, dec_seq):
    b, hw, c = state.shape
    assert dec_seq >= hw
    return jnp.concatenate([state.astype(F32), jnp.zeros((b, dec_seq - hw, c), F32)], axis=1).reshape(b * dec_seq, c)


def _sconv_pair(gcu, state, w, nb, seq, dec_seq):
    gp, gs = gcu
    mp = gp.shape[0]
    ms = gs.shape[0]
    ch = gp.shape[1] // 3
    tc = 512
    ncb = ch // tc
    ts = _row_tile(seq, 512)
    tps = seq // ts
    nt = mp // ts
    pr = lambda j, i: jnp.minimum(i, nt - 1)
    in_specs = []
    for part in range(3):
        in_specs.append(pl.BlockSpec((ts, tc), lambda j, i, part=part: (pr(j, i), part * ncb + j)))
    for part in range(3):
        in_specs.append(pl.BlockSpec((ms, tc), lambda j, i, part=part: (0, part * ncb + j)))
    in_specs += [pl.BlockSpec((ms, tc), lambda j, i: (0, j)), pl.BlockSpec((SC_WIDTH, tc), lambda j, i: (0, j))]
    out_specs = [pl.BlockSpec((ts, tc), lambda j, i: (pr(j, i), j)),
                 pl.BlockSpec((ms, tc), lambda j, i: (0, j)),
                 pl.BlockSpec((1, SUBLANES, tc), lambda j, i: (pr(j, i) // tps, 0, j)),
                 pl.BlockSpec((ms, tc), lambda j, i: (0, j))]
    out_shape = [jax.ShapeDtypeStruct((mp, ch), BF16), jax.ShapeDtypeStruct((ms, ch), BF16),
                 jax.ShapeDtypeStruct((nb, SUBLANES, ch), F32), jax.ShapeDtypeStruct((ms, ch), F32)]
    yp, ys, zp_tail, zs = pl.pallas_call(
        functools.partial(_sconv_body, nt, tps, SC_WIDTH, dec_seq),
        grid=(ncb, nt + 1), in_specs=in_specs, out_specs=out_specs, out_shape=out_shape,
        scratch_shapes=[pltpu.VMEM((SUBLANES, tc), F32)],
        compiler_params=_params(2), name="short_conv",
    )(gp, gp, gp, gs, gs, gs, _pad_hist(state, dec_seq), w.astype(F32))
    hw = SC_WIDTH - 1
    new_p = zp_tail[:, SUBLANES - hw:, :]
    new_s = zs.reshape(ms // dec_seq, dec_seq, ch)[:, dec_seq - hw:, :]
    return (yp, ys), new_p, new_s


def _fox_prompt_body(tq, hps, q, k, v, fc, fr, o, m_s, l_s, acc_s):
    qi = pl.program_id(2)
    dh = ATT_HEAD_DIM
    m_s[...] = jnp.full_like(m_s, NEG_BIG)
    l_s[...] = jnp.zeros_like(l_s)
    acc_s[...] = jnp.zeros_like(acc_s)

    def step(ki, masked):
        start = pl.multiple_of(ki * tq, tq)
        for hh in range(hps):
            cols = slice(hh * dh, (hh + 1) * dh)
            kb = k[pl.ds(start, tq), cols].astype(BF16)
            vb = v[pl.ds(start, tq), cols].astype(BF16)
            s = lax.dot_general(q[:, cols], kb, NT_DIMS, preferred_element_type=F32)
            s = s + fc[0, hh] - fr[0, hh, ki]
            if masked:
                r = lax.broadcasted_iota(jnp.int32, s.shape, 0)
                c = lax.broadcasted_iota(jnp.int32, s.shape, 1)
                s = jnp.where(r >= c, s, NEG_BIG)
            m_prev = m_s[hh]
            m_new = jnp.maximum(m_prev, jnp.max(s, axis=-1, keepdims=True))
            p = jnp.exp(s - m_new)
            alpha = jnp.exp(m_prev - m_new)
            l_s[hh] = alpha * l_s[hh] + jnp.sum(p, axis=-1, keepdims=True)
            acc_s[hh] = alpha * acc_s[hh] + jnp.dot(p.astype(BF16), vb, preferred_element_type=F32)
            m_s[hh] = m_new

    def loop_body(ki, carry):
        step(ki, False)
        return carry

    lax.fori_loop(0, qi, loop_body, 0)
    step(qi, True)
    for hh in range(hps):
        o[:, hh * dh:(hh + 1) * dh] = (acc_s[hh] / l_s[hh]).astype(o.dtype)


def _fox_prompt(q, k, v, fcum, nb, seq):
    mp = q.shape[0]
    dh = ATT_HEAD_DIM
    nh = q.shape[1] // dh
    tq = _row_tile(seq, 512)
    nq = seq // tq
    ft = jnp.transpose(fcum, (0, 2, 1))
    fcol = ft[..., None]
    frow = ft.reshape(nb, nh, nq, 1, tq)
    hps = 2 if nh % 2 == 0 else 1
    gw = hps * dh
    return pl.pallas_call(
        functools.partial(_fox_prompt_body, tq, hps),
        grid=(nb, nh // hps, nq),
        in_specs=[pl.BlockSpec((tq, gw), lambda b, h, i: (b * nq + i, h)),
                  pl.BlockSpec((seq, gw), lambda b, h, i: (b, h)),
                  pl.BlockSpec((seq, gw), lambda b, h, i: (b, h)),
                  pl.BlockSpec((1, hps, tq, 1), lambda b, h, i: (b, h, i, 0)),
                  pl.BlockSpec((1, hps, nq, 1, tq), lambda b, h, i: (b, h, 0, 0, 0))],
        out_specs=pl.BlockSpec((tq, gw), lambda b, h, i: (b * nq + i, h)),
        out_shape=jax.ShapeDtypeStruct((mp, nh * dh), BF16),
        scratch_shapes=[pltpu.VMEM((hps, tq, 1), F32), pltpu.VMEM((hps, tq, 1), F32),
                        pltpu.VMEM((hps, tq, dh), F32)],
        compiler_params=_params(3), name="fox_prompt",
    )(q, k, v, fcol, frow)


def _past_bias_body(n_pages, page, nh, *refs):
    lf_refs = refs[1:1 + n_pages]
    out = refs[1 + n_pages]
    rows, later_in_page, same_head = refs[2 + n_pages:]
    w = page * nh
    npp = rows.shape[0]

    @pl.when(pl.program_id(0) == 0)
    def _():
        a = lax.broadcasted_iota(jnp.int32, (w, w), 0)
        c = lax.broadcasted_iota(jnp.int32, (w, w), 1)
        same = (a % nh) == (c % nh)
        same_head[...] = same.astype(BF16)
        later_in_page[...] = (same & ((a // nh) > (c // nh))).astype(BF16)
        rows[...] = jnp.zeros_like(rows)

    for p in range(n_pages):
        rows[p:p + 1, :] = lf_refs[p][0]
    lf3 = jnp.concatenate(_split3(rows[...]), axis=0)
    within3 = jnp.dot(lf3, later_in_page[...], preferred_element_type=F32)
    totals3 = jnp.dot(lf3, same_head[...], preferred_element_type=F32)
    within = within3[:npp] + within3[npp:2 * npp] + within3[2 * npp:]
    totals = totals3[:npp] + totals3[npp:2 * npp] + totals3[2 * npp:]
    pr = lax.broadcasted_iota(jnp.int32, (npp, npp), 0)
    pc = lax.broadcasted_iota(jnp.int32, (npp, npp), 1)
    later_pages = (pc > pr).astype(BF16)
    after3 = jnp.dot(later_pages, jnp.concatenate(_split3(totals), axis=1), preferred_element_type=F32)
    after = after3[:, :w] + after3[:, w:2 * w] + after3[:, 2 * w:]
    out[0] = (within + after)[:n_pages, :]


def _past_bias(cache_lf, page_table):
    n_phys, page, nh = cache_lf.shape
    nb, n_pages = page_table.shape
    w = page * nh
    np_pad = -(-n_pages // LANES) * LANES
    lf = cache_lf.reshape(n_phys, 1, w)
    in_specs = [pl.BlockSpec((1, 1, w), lambda b, pt, p=p: (pt[b, p], 0, 0)) for p in range(n_pages)]
    grid_spec = pltpu.PrefetchScalarGridSpec(
        num_scalar_prefetch=1, grid=(nb,), in_specs=in_specs,
        out_specs=pl.BlockSpec((1, n_pages, w), lambda b, pt: (b, 0, 0)),
        scratch_shapes=[pltpu.VMEM((np_pad, w), F32), pltpu.VMEM((w, w), BF16), pltpu.VMEM((w, w), BF16)])
    return pl.pallas_call(
        functools.partial(_past_bias_body, n_pages, page, nh),
        grid_spec=grid_spec, out_shape=jax.ShapeDtypeStruct((nb, n_pages, w), F32),
        compiler_params=_params(1), name="fox_past_bias",
    )(page_table, *([lf] * n_pages))


def _fox_sample_body(pps, n_steps, nh, dec_seq, *refs):
    qall, knew, vnew, frow_new, rb = refs[1:6]
    krefs = refs[6:6 + pps]
    vrefs = refs[6 + pps:6 + 2 * pps]
    o = refs[6 + 2 * pps]
    m_s, l_s, acc_s, s_scr = refs[7 + 2 * pps:]
    step = pl.program_id(1)
    qv = qall[0]
    nrow = qv.shape[0]

    @pl.when(step == 0)
    def _():
        kn = knew[0].astype(BF16)
        g = lax.dot_general(qv, kn, NT_DIMS, preferred_element_type=F32)
        r = lax.broadcasted_iota(jnp.int32, g.shape, 0)
        c = lax.broadcasted_iota(jnp.int32, g.shape, 1)
        valid = ((c % nh) == (r // dec_seq)) & ((c // nh) <= (r % dec_seq))
        s = jnp.where(valid, g - frow_new[0], NEG_BIG)
        m = jnp.max(s, axis=-1, keepdims=True)
        p = jnp.exp(s - m)
        m_s[...] = m
        l_s[...] = jnp.sum(p, axis=-1, keepdims=True)
        acc_s[...] = jnp.dot(p.astype(BF16), vnew[0].astype(BF16), preferred_element_type=F32)

    lanes = krefs[0].shape[1]
    r = lax.broadcasted_iota(jnp.int32, (nrow, lanes), 0)
    c = lax.broadcasted_iota(jnp.int32, (nrow, lanes), 1)
    head_ok = (c % nh) == (r // dec_seq)
    smax = None
    for i in range(pps):
        g = lax.dot_general(qv, krefs[i][0].astype(BF16), NT_DIMS, preferred_element_type=F32)
        s = jnp.where(head_ok, g + rb[0, i], NEG_BIG)
        s_scr[i] = s
        smax = s if smax is None else jnp.maximum(smax, s)
    m_prev = m_s[...]
    m_new = jnp.maximum(m_prev, jnp.max(smax, axis=-1, keepdims=True))
    alpha = jnp.exp(m_prev - m_new)
    psum = acc = None
    for i in range(pps):
        p = jnp.exp(s_scr[i] - m_new)
        pv = jnp.dot(p.astype(BF16), vrefs[i][0].astype(BF16), preferred_element_type=F32)
        psum = p if psum is None else psum + p
        acc = pv if acc is None else acc + pv
    l_s[...] = alpha * l_s[...] + jnp.sum(psum, axis=-1, keepdims=True)
    acc_s[...] = alpha * acc_s[...] + acc
    m_s[...] = m_new

    @pl.when(step == n_steps - 1)
    def _():
        o[0] = (acc_s[...] / l_s[...]).astype(o.dtype)


def _fox_sample(q, k, v, fn, cache_k, cache_v, cache_lf, page_table, dec_seq):
    ms = q.shape[0]
    nb = ms // dec_seq
    nh, dh = ATT_HEADS, ATT_HEAD_DIM
    n_phys, page = cache_k.shape[0], cache_k.shape[1]
    n_pages = page_table.shape[1]
    pps = 8
    while n_pages % pps:
        pps //= 2
    n_steps = n_pages // pps
    nrow = nh * dec_seq
    rbias = _past_bias(cache_lf, page_table).reshape(nb, n_pages, 1, page * nh)
    qall = jnp.transpose(q.reshape(nb, dec_seq, nh, dh), (0, 2, 1, 3)).reshape(nb, nrow, dh)
    knew = k.reshape(nb, dec_seq * nh, dh)
    vnew = v.reshape(nb, dec_seq * nh, dh)
    frow_new = fn.reshape(nb, 1, dec_seq * nh)
    ck = cache_k.reshape(n_phys, page * nh, dh)
    cv = cache_v.reshape(n_phys, page * nh, dh)
    per_b = lambda shape: pl.BlockSpec((1,) + shape, lambda b, s, pt: (b, 0, 0))
    in_specs = [per_b((nrow, dh)), per_b((dec_seq * nh, dh)), per_b((dec_seq * nh, dh)),
                per_b((1, dec_seq * nh)),
                pl.BlockSpec((1, pps, 1, page * nh), lambda b, s, pt: (b, s, 0, 0))]
    for _ in range(2):
        in_specs += [pl.BlockSpec((1, page * nh, dh), lambda b, s, pt, i=i: (pt[b, s * pps + i], 0, 0))
                     for i in range(pps)]
    grid_spec = pltpu.PrefetchScalarGridSpec(
        num_scalar_prefetch=1, grid=(nb, n_steps), in_specs=in_specs,
        out_specs=pl.BlockSpec((1, nrow, dh), lambda b, s, pt: (b, 0, 0)),
        scratch_shapes=[pltpu.VMEM((nrow, 1), F32), pltpu.VMEM((nrow, 1), F32), pltpu.VMEM((nrow, dh), F32),
                        pltpu.VMEM((pps, nrow, page * nh), F32)])
    o = pl.pallas_call(
        functools.partial(_fox_sample_body, pps, n_steps, nh, dec_seq),
        grid_spec=grid_spec, out_shape=jax.ShapeDtypeStruct((nb, nrow, dh), BF16),
        compiler_params=_params(2), name="fox_sample",
    )(page_table, qall, knew, vnew, frow_new, rbias, *([ck] * pps), *([cv] * pps))
    return jnp.transpose(o.reshape(nb, nh, dec_seq, dh), (0, 2, 1, 3)).reshape(ms, nh * dh)


def _ssd_body(t_in, nseq, carried, xs_r, z_r, b_r, c_r, dt_r, wx_r, wb_r, wc_r, bx_r, bb_r, bc_r, a_r, d_r, gn_r,
              sel_r, ex_r, *rest):
    if carried:
        hx_r, hb_r, hc_r, s0_r, y_o, st_o, st, carry = rest
    else:
        hx_r = hb_r = hc_r = s0_r = None
        y_o, st_o, st, carry = rest
    ci = pl.program_id(2)
    nc = pl.num_programs(2)
    L = t_in * nseq
    P, N = SSM_HEAD_DIM, D_STATE
    gw = xs_r.shape[1]
    R = gw // P

    @pl.when(ci == 0)
    def _():
        if carried:
            for s in range(nseq):
                st[:, s * N:(s + 1) * N] = s0_r[s, 0]
        else:
            st[...] = jnp.zeros_like(st)
            carry[...] = jnp.zeros_like(carry)

    def conv_act(raw_r, hist_r, w_r, bias_r, lo, hi):
        x = raw_r[...]
        if carried:
            t = lax.broadcasted_iota(jnp.int32, x.shape, 0) % t_in
            y = _conv_rows(x, hist_r[...], w_r, SSM_CONV, t)
        else:
            y = _conv_rows(x, None, w_r, SSM_CONV, None)
            head = x[:SUBLANES, :]
            t8 = lax.broadcasted_iota(jnp.int32, head.shape, 0)
            y8 = _conv_rows(head, carry[:, lo:hi], w_r, SSM_CONV, t8)
            y = jnp.concatenate([y8, y[SUBLANES:, :]], axis=0)
            carry[:, lo:hi] = pltpu.roll(x[L - SUBLANES:, :], SSM_CONV - 1, axis=0)
        return _silu(y + bias_r[...])

    xs = conv_act(xs_r, hx_r, wx_r, bx_r, 0, gw)
    bmf = conv_act(b_r, hb_r, wb_r, bb_r, gw, gw + N)
    cmf = conv_act(c_r, hc_r, wc_r, bc_r, gw + N, gw + 2 * N)
    bm, cm = bmf.astype(BF16), cmf.astype(BF16)
    dt = jnp.dot(jnp.concatenate(_split3(dt_r[...]), axis=1), sel_r[...], preferred_element_type=F32)
    r = lax.broadcasted_iota(jnp.int32, (L, L), 0)
    c = lax.broadcasted_iota(jnp.int32, (L, L), 1)
    if nseq == 1:
        tri = r >= c
        acum = jnp.dot(tri.astype(F32), dt * a_r[...], preferred_element_type=F32, precision=HIGHEST)
        a_tot = acum[L - 1:L, :]
    else:
        same = (r // t_in) == (c // t_in)
        tri = same & (r >= c)
        sums = jnp.dot(jnp.concatenate([tri.astype(F32), same.astype(F32)], axis=0), dt * a_r[...],
                       preferred_element_type=F32, precision=HIGHEST)
        acum, a_tot = sums[:L], sums[L:]
    acum_t = acum.T
    dt_t = dt.T
    cols = jnp.concatenate([dt * jnp.exp(a_tot - acum), jnp.exp(acum)], axis=0)
    wide = jnp.dot(jnp.concatenate(_split3(cols), axis=1), ex_r[...], preferred_element_type=F32)
    w_state, w_off = wide[:L], wide[L:]
    cb = lax.dot_general(cm, bm, NT_DIMS, preferred_element_type=F32)
    if nseq == 1:
        cblk, bblk = cm, bm
        decay_t = acum_t
    else:
        seq_of_row = lax.broadcasted_iota(jnp.int32, (L, N), 0) // t_in
        cblk = jnp.concatenate([jnp.where(seq_of_row == s, cmf, 0.0) for s in range(nseq)], axis=1).astype(BF16)
        bblk = jnp.concatenate([jnp.where(seq_of_row == s, bmf, 0.0) for s in range(nseq)], axis=1).astype(BF16)
        decay_t = a_tot.T
    s_prev = st[...]
    y_off = lax.dot_general(cblk, s_prev.astype(BF16), NT_DIMS, preferred_element_type=F32) * w_off
    xb = xs.astype(BF16)
    ys = []
    for h in range(R):
        decay = jnp.exp(jnp.where(tri, acum[:, h:h + 1] - acum_t[h:h + 1, :], -jnp.inf))
        m = (cb * decay * dt_t[h:h + 1, :]).astype(BF16)
        ys.append(jnp.dot(m, xb[:, h * P:(h + 1) * P], preferred_element_type=F32))
    new_states = lax.dot_general((xs * w_state).astype(BF16), bblk, TN_DIMS, preferred_element_type=F32)
    for s in range(nseq):
        last = (s + 1) * t_in - 1
        for h in range(R):
            blk = (slice(h * P, (h + 1) * P), slice(s * N, (s + 1) * N))
            st[blk] = s_prev[blk] * jnp.exp(decay_t[h:h + 1, last:last + 1]) + new_states[blk]
    y = jnp.concatenate(ys, axis=1) + y_off + d_r[...] * xs
    y = y * _silu(z_r[...])
    y = y * lax.rsqrt(jnp.mean(y * y, axis=-1, keepdims=True) + EPS)
    y_o[...] = (y * gn_r[...]).astype(y_o.dtype)

    @pl.when(ci == nc - 1)
    def _():
        for s in range(nseq):
            st_o[s, 0] = st[:, s * N:(s + 1) * N]


def _ssd(xbc, z, dt, w_conv, b_conv, a_pad, d_exp, gn_w, hist, init_state, nb, seq):
    G, N, P = SSM_GROUPS, D_STATE, SSM_HEAD_DIM
    inner = z.shape[1]
    gw = inner // G
    rg = gw // P
    L = SSM_CHUNK
    carried = init_state is not None
    if carried:
        t_in, nc = seq, 1
        assert L % seq == 0 and nb % (L // seq) == 0, (nb, seq)
    else:
        t_in, nc = L, seq // L
        assert seq % L == 0, seq
    nseq = L // t_in
    nbb = nb // nseq
    nx, nbc = inner // gw, inner // N
    row = lambda b, g, c: b * nc + c
    xcol = lambda b, g, c: g
    bcol = lambda b, g, c: nbc + g
    ccol = lambda b, g, c: nbc + G + g

    def triple(rows_, row_idx):
        return [pl.BlockSpec((rows_, gw), lambda b, g, c: (row_idx(b, g, c), xcol(b, g, c))),
                pl.BlockSpec((rows_, N), lambda b, g, c: (row_idx(b, g, c), bcol(b, g, c))),
                pl.BlockSpec((rows_, N), lambda b, g, c: (row_idx(b, g, c), ccol(b, g, c)))]

    zero_row = lambda b, g, c: 0
    x_spec, b_spec, c_spec = triple(L, row)
    in_specs = [x_spec, pl.BlockSpec((L, gw), lambda b, g, c: (row(b, g, c), g)), b_spec, c_spec,
                pl.BlockSpec((L, LANES), lambda b, g, c: (row(b, g, c), 0))]
    in_specs += triple(SSM_CONV, zero_row) + triple(1, zero_row)
    in_specs += [pl.BlockSpec((1, LANES), lambda b, g, c: (0, g)),
                 pl.BlockSpec((1, gw), lambda b, g, c: (0, g)),
                 pl.BlockSpec((1, gw), lambda b, g, c: (0, g)),
                 pl.BlockSpec((None, 3 * LANES, LANES), lambda b, g, c: (g, 0, 0)),
                 pl.BlockSpec((3 * LANES, gw), lambda b, g, c: (0, 0))]
    lane_head = jnp.arange(3 * LANES) % LANES
    widen = (lane_head[:, None] == jnp.arange(gw)[None, :] // P).astype(BF16)
    lane = jnp.arange(LANES)
    pick = ((lane_head[None, :, None] == jnp.arange(G)[:, None, None] * rg + lane[None, None, :])
            & (lane[None, None, :] < rg)).astype(BF16)
    w_conv = w_conv.astype(F32)
    b_conv = b_conv.reshape(1, -1).astype(F32)
    args = [xbc, z, xbc, xbc, dt, w_conv, w_conv, w_conv, b_conv, b_conv, b_conv, a_pad, d_exp, gn_w, pick, widen]
    if carried:
        in_specs += triple(L, row) + [pl.BlockSpec((nseq, 1, gw, N), lambda b, g, c: (b, g, 0, 0))]
        args += [hist, hist, hist, init_state]
    y, st = pl.pallas_call(
        functools.partial(_ssd_body, t_in, nseq, carried),
        grid=(nbb, G, nc), in_specs=in_specs,
        out_specs=[pl.BlockSpec((L, gw), lambda b, g, c: (row(b, g, c), g)),
                   pl.BlockSpec((nseq, 1, gw, N), lambda b, g, c: (b, g, 0, 0))],
        out_shape=[jax.ShapeDtypeStruct((nb * seq, inner), BF16), jax.ShapeDtypeStruct((nb, G, gw, N), F32)],
        scratch_shapes=[pltpu.VMEM((gw, nseq * N), F32), pltpu.VMEM((SUBLANES, gw + 2 * N), F32)],
        compiler_params=_params(3), name="ssd",
    )(*args)
    return y, st


def _ffn(x, norm_w, w_gate, w_up, w_down, layer):
    h = _rmsnorm_pair(x, norm_w, BF16)
    a = _mm_pair([h], [_W(w_gate, layer), _W(w_up, layer)], out_dtype=BF16, mode="swiglu", name="ffn_gate_up")
    return _mm_pair([a], [_W(w_down, layer)], res=x, name="ffn_down")


def kernel(x_prompt, x_sample, cache_k, cache_v, cache_logf, page_table, state_sconv, state_mconv, state_ssm, norm_mix, norm_ffn, norm_final, w_in_even, b_forget, w_sconv, w_out_even, w_in_ssm, w_mconv, b_mconv, dt_bias, a_log, d_skip, w_gnorm, w_out_ssm, w_gate, w_up, w_down):
    bp, seq, d = x_prompt.shape
    bs, dec_seq, _ = x_sample.shape
    mp, ms = bp * seq, bs * dec_seq
    nh, dh = ATT_HEADS, ATT_HEAD_DIM
    att = nh * dh
    sc_ch = state_sconv.shape[-1]
    x = (x_prompt.reshape(mp, d), x_sample.reshape(ms, d))

    i = 0
    h = _rmsnorm_pair(x, norm_mix[0], BF16)
    c0 = 3 * sc_ch
    w_in_t = jnp.swapaxes(w_in_even, 1, 2)
    win = lambda col0, n: _W(w_in_t, i, col0=col0, n=n, transposed=True)
    gcu = _mm_pair([h], [win(0, c0)], name="in_even_conv")
    q = _mm_pair([h], [win(c0, att)], out_dtype=BF16, scale=dh ** -0.5, name="in_even_q")
    k = _mm_pair([h], [win(c0 + att, att)], name="in_even_k")
    v = _mm_pair([h], [win(c0 + 2 * att, att)], name="in_even_v")
    lf_p, fc_p, lf_s, fc_s = _forget_pair(h, w_in_even[i, :, c0 + 3 * att:], b_forget[i], bp, seq, dec_seq)
    logf_p = lf_p[:, :nh].reshape(bp, seq, nh)
    logf_s = lf_s[:, :nh].reshape(bs, dec_seq, nh)
    y_a, sconv_p, sconv_s = _sconv_pair(gcu, state_sconv[i], w_sconv[i], bp, seq, dec_seq)
    yb_p = _fox_prompt(q[0], k[0], v[0], fc_p[:, :nh].reshape(bp, seq, nh), bp, seq)
    yb_s = _fox_sample(q[1], k[1], v[1], fc_s[:, :nh].reshape(bs, dec_seq, nh),
                       cache_k[i], cache_v[i], cache_logf[i], page_table, dec_seq)
    x = _mm_pair([y_a, (yb_p, yb_s)],
                 [_W(w_out_even, i, row0=0, k=sc_ch), _W(w_out_even, i, row0=sc_ch, k=att)], res=x, name="out_even")
    x = _ffn(x, norm_ffn[0], w_gate, w_up, w_down, 0)

    j = 0
    G = SSM_GROUPS
    n_heads = a_log.shape[1]
    rg = n_heads // G
    inner = n_heads * SSM_HEAD_DIM
    conv_dim = state_mconv.shape[-1]
    h = _rmsnorm_pair(x, norm_mix[1], BF16)
    w_ssm_t = jnp.swapaxes(w_in_ssm, 1, 2)
    z = _mm_pair([h], [_W(w_ssm_t, j, n=inner, transposed=True)], name="in_ssm_z")
    xbc = _mm_pair([h], [_W(w_ssm_t, j, col0=inner, n=conv_dim, transposed=True)], name="in_ssm_xbc")
    assert n_heads <= LANES
    w_dt = jnp.zeros((d, LANES), BF16).at[:, :n_heads].set(w_in_ssm[j, :, inner + conv_dim:].astype(BF16))
    b_dt = jnp.zeros((1, LANES), F32).at[0, :n_heads].set(dt_bias[j].astype(F32))
    dt = _mm_pair([h], [_W(w_dt)], mode="softplus", bias=b_dt, name="in_ssm_dt")
    a_pad = jnp.zeros((G, LANES), F32).at[:, :rg].set(-jnp.exp(a_log[j].astype(F32)).reshape(G, rg)).reshape(1, G * LANES)
    d_exp = jnp.repeat(d_skip[j].astype(F32), SSM_HEAD_DIM).reshape(1, inner)
    gn_w = w_gnorm[j].astype(F32).reshape(1, inner)
    hw = SSM_CONV - 1
    mconv_p = xbc[0].reshape(bp, seq, conv_dim)[:, seq - hw:, :]
    mconv_s = xbc[1].reshape(bs, dec_seq, conv_dim)[:, dec_seq - hw:, :]
    yp, ssm_p = _ssd(xbc[0], z[0], dt[0], w_mconv[j], b_mconv[j], a_pad, d_exp, gn_w, None, None, bp, seq)
    init = state_ssm[j].astype(F32).reshape(bs, G, rg * SSM_HEAD_DIM, D_STATE)
    ysm, ssm_s = _ssd(xbc[1], z[1], dt[1], w_mconv[j], b_mconv[j], a_pad, d_exp, gn_w,
                      _pad_hist(state_mconv[j], dec_seq), init, bs, dec_seq)
    x = _mm_pair([(yp, ysm)], [_W(w_out_ssm, j)], res=x, name="out_ssm")
    x = _ffn(x, norm_ffn[1], w_gate, w_up, w_down, 1)

    y = _rmsnorm_pair(x, norm_final, F32)
    st_shape = (n_heads, SSM_HEAD_DIM, D_STATE)
    return (y[0].reshape(bp, seq, d), y[1].reshape(bs, dec_seq, d),
            k[0].reshape(1, bp, seq, nh, dh), v[0].reshape(1, bp, seq, nh, dh), logf_p[None],
            sconv_p[None], mconv_p[None], ssm_p.reshape((1, bp) + st_shape),
            k[1].reshape(1, bs, dec_seq, nh, dh), v[1].reshape(1, bs, dec_seq, nh, dh), logf_s[None],
            sconv_s[None], mconv_s[None], ssm_s.reshape((1, bs) + st_shape))
```

```python
import functools

import jax
import jax.numpy as jnp
from jax import lax
from jax.experimental import pallas as pl
from jax.experimental.pallas import tpu as pltpu

F32 = jnp.float32
BF16 = jnp.bfloat16

SC_WIDTH = 3
ATT_HEADS = 8
ATT_HEAD_DIM = 128
SSM_HEAD_DIM = 64
SSM_GROUPS = 8
D_STATE = 128
SSM_CONV = 4
SSM_CHUNK = 128
EPS = 1e-6

LANES = 128
SUBLANES = 8
VMEM_LIMIT_BYTES = 56 * 1024 * 1024
NEG_BIG = -1e30

HIGHEST = lax.Precision.HIGHEST
NT_DIMS = (((1,), (1,)), ((), ()))
TN_DIMS = (((0,), (0,)), ((), ()))


def _params(n_grid_dims):
    return pltpu.CompilerParams(dimension_semantics=("arbitrary",) * n_grid_dims,
                                vmem_limit_bytes=VMEM_LIMIT_BYTES)


def _row_tile(mp, pref):
    t = pref
    while mp % t:
        t //= 2
    assert t >= SUBLANES, (mp, pref)
    return t


def _rmsnorm_body(nt, xp, xs, w, op, os):
    i = pl.program_id(0)

    def norm(x):
        y = x * lax.rsqrt(jnp.mean(x * x, axis=-1, keepdims=True) + EPS)
        return (y * w[...]).astype(op.dtype)

    @pl.when(i < nt)
    def _():
        op[...] = norm(xp[...])

    @pl.when(i == nt)
    def _():
        os[...] = norm(xs[...])


def _rmsnorm_pair(x, w, out_dtype):
    xp, xs = x
    mp, d = xp.shape
    ms = xs.shape[0]
    tm = _row_tile(mp, 512)
    nt = mp // tm
    pidx = lambda i: (jnp.minimum(i, nt - 1), 0)
    return pl.pallas_call(
        functools.partial(_rmsnorm_body, nt),
        grid=(nt + 1,),
        in_specs=[pl.BlockSpec((tm, d), pidx), pl.BlockSpec((ms, d), lambda i: (0, 0)),
                  pl.BlockSpec((1, d), lambda i: (0, 0))],
        out_specs=[pl.BlockSpec((tm, d), pidx), pl.BlockSpec((ms, d), lambda i: (0, 0))],
        out_shape=[jax.ShapeDtypeStruct((mp, d), out_dtype), jax.ShapeDtypeStruct((ms, d), out_dtype)],
        compiler_params=_params(1), name="rmsnorm",
    )(xp, xs, w.reshape(1, d).astype(F32))


def _softplus(x):
    return jnp.maximum(x, 0.0) + jnp.log1p(jnp.exp(-jnp.abs(x)))


def _silu(x):
    return x * (1.0 / (1.0 + jnp.exp(-x)))


def _split3(v):
    hi = v.astype(BF16)
    r1 = v - hi.astype(F32)
    mid = r1.astype(BF16)
    lo = (r1 - mid.astype(F32)).astype(BF16)
    return hi, mid, lo


class _W:
    def __init__(self, arr, layer=None, row0=0, k=None, col0=0, n=None, transposed=False):
        self.arr, self.layer, self.row0, self.col0, self.transposed = arr, layer, row0, col0, transposed
        kdim, ndim = (-1, -2) if transposed else (-2, -1)
        self.k = arr.shape[kdim] if k is None else k
        self.n = arr.shape[ndim] if n is None else n

    def block(self, tn):
        return (tn, self.k) if self.transposed else (self.k, tn)

    def spec(self, tn):
        assert self.row0 % self.k == 0 and self.col0 % tn == 0
        rb, cb = self.row0 // self.k, self.col0 // tn
        if self.transposed:
            idx = lambda j, i: (cb + j, rb)
        else:
            idx = lambda j, i: (rb, cb + j)
        if self.layer is None:
            return pl.BlockSpec(self.block(tn), idx)
        layer = self.layer
        return pl.BlockSpec((None,) + self.block(tn), lambda j, i: (layer,) + idx(j, i))


def _mm_body(nt, n_lhs, mode, has_res, has_bias, scale, w_is_f32, w_transposed, *refs):
    i = pl.program_id(1)

    def mm(x, idx):
        dims = NT_DIMS if w_transposed[idx] else (((1,), (0,)), ((), ()))
        return lax.dot_general(x, wrefs[idx][...], dims, preferred_element_type=F32)

    xrefs = refs[:2 * n_lhs]
    n_w = len(w_is_f32)
    wrefs = list(refs[2 * n_lhs:2 * n_lhs + n_w])
    pos = 2 * n_lhs + n_w
    rp = rs = bias = None
    if has_res:
        rp, rs = refs[pos], refs[pos + 1]
        pos += 2
    if has_bias:
        bias = refs[pos]
        pos += 1
    op, os = refs[pos], refs[pos + 1]
    scratch = list(refs[pos + 2:])

    for idx in range(n_w):
        if w_is_f32[idx]:
            wsrc, wdst = wrefs[idx], scratch.pop(0)

            @pl.when(i == 0)
            def _(wsrc=wsrc, wdst=wdst):
                wdst[...] = wsrc[...].astype(BF16)

            wrefs[idx] = wdst

    def compute(which, out, res):
        if mode == "swiglu":
            x = xrefs[which][...]
            acc = _silu(mm(x, 0)) * mm(x, 1)
        else:
            acc = mm(xrefs[which][...], 0)
            for k in range(1, n_lhs):
                acc += mm(xrefs[2 * k + which][...], k)
        if scale is not None:
            acc = acc * scale
        if has_bias:
            acc = acc + bias[...]
        if mode == "softplus":
            acc = _softplus(acc)
        if has_res:
            acc = res[...] + acc
        out[...] = acc.astype(out.dtype)

    @pl.when(i == 0)
    def _():
        compute(1, os, rs)

    @pl.when(i > 0)
    def _():
        compute(0, op, rp)


def _mm_pair(xs_list, ws_list, *, out_dtype=F32, mode="plain", res=None, bias=None, scale=None, name="matmul"):
    n_lhs = len(xs_list)
    mp = xs_list[0][0].shape[0]
    ms = xs_list[0][1].shape[0]
    n = ws_list[0].n
    kmax = max(w.k for w in ws_list)
    tm = _row_tile(mp, 1024 if kmax <= 4096 else 512)
    nt = mp // tm
    if n % 1024 == 0 and kmax <= 2048 and mode != "swiglu":
        tn = 1024
    elif n % 512 == 0:
        tn = 512
    else:
        assert n % LANES == 0 and n <= 512, n
        tn = n
    pidx_x = lambda j, i: (jnp.maximum(i - 1, 0), 0)
    pidx_o = lambda j, i: (jnp.maximum(i - 1, 0), j)
    in_specs, args = [], []
    for xp, xs in xs_list:
        kk = xp.shape[1]
        in_specs += [pl.BlockSpec((tm, kk), pidx_x), pl.BlockSpec((ms, kk), lambda j, i: (0, 0))]
        args += [xp, xs]
    w_is_f32, scratch = [], []
    for w in ws_list:
        in_specs.append(w.spec(tn))
        args.append(w.arr)
        w_is_f32.append(w.arr.dtype != BF16)
        if w_is_f32[-1]:
            scratch.append(pltpu.VMEM(w.block(tn), BF16))
    if res is not None:
        in_specs += [pl.BlockSpec((tm, tn), pidx_o), pl.BlockSpec((ms, tn), lambda j, i: (0, j))]
        args += [res[0], res[1]]
    if bias is not None:
        in_specs.append(pl.BlockSpec((1, tn), lambda j, i: (0, j)))
        args.append(bias.reshape(1, n).astype(F32))
    outs = pl.pallas_call(
        functools.partial(_mm_body, nt, n_lhs, mode, res is not None, bias is not None, scale, tuple(w_is_f32),
                          tuple(w.transposed for w in ws_list)),
        grid=(n // tn, nt + 1),
        in_specs=in_specs,
        out_specs=[pl.BlockSpec((tm, tn), pidx_o), pl.BlockSpec((ms, tn), lambda j, i: (0, j))],
        out_shape=[jax.ShapeDtypeStruct((mp, n), out_dtype), jax.ShapeDtypeStruct((ms, n), out_dtype)],
        scratch_shapes=scratch,
        compiler_params=_params(2), name=name,
    )(*args)
    return outs[0], outs[1]


def _log_sigmoid(x):
    return -_softplus(-x)


def _forget_body(nb, seq, dec_seq, ck, xp, xs, w, b, lp, fp, ls, fs):
    i = pl.program_id(0)

    @pl.when(i < nb)
    def _():
        r = lax.broadcasted_iota(jnp.int32, (ck, ck), 0)
        c = lax.broadcasted_iota(jnp.int32, (ck, ck), 1)
        tri = (r >= c).astype(F32)
        carry = jnp.zeros((1, LANES), F32)
        for t in range(seq // ck):
            f = jnp.dot(xp[t * ck:(t + 1) * ck, :], w[...], preferred_element_type=F32) + b[...]
            lf = _log_sigmoid(f)
            lp[t * ck:(t + 1) * ck, :] = lf
            cs = jnp.dot(tri, lf, preferred_element_type=F32, precision=HIGHEST) + carry
            fp[t * ck:(t + 1) * ck, :] = cs
            carry = cs[ck - 1:ck, :]

    @pl.when(i == nb)
    def _():
        ms = xs.shape[0]
        r = lax.broadcasted_iota(jnp.int32, (ms, ms), 0)
        c = lax.broadcasted_iota(jnp.int32, (ms, ms), 1)
        tri = ((r // dec_seq == c // dec_seq) & (r >= c)).astype(F32)
        f = jnp.dot(xs[...], w[...], preferred_element_type=F32) + b[...]
        lf = _log_sigmoid(f)
        ls[...] = lf
        fs[...] = jnp.dot(tri, lf, preferred_element_type=F32, precision=HIGHEST)


def _forget_pair(h, w_f, b_f, nb, seq, dec_seq):
    hp, hs = h
    mp, d = hp.shape
    ms = hs.shape[0]
    assert ms <= 512 and ms % SUBLANES == 0
    ck = _row_tile(seq, 256)
    n_h = w_f.shape[1]
    wpad = jnp.zeros((d, LANES), BF16).at[:, :n_h].set(w_f.astype(BF16))
    bpad = jnp.zeros((1, LANES), F32).at[0, :n_h].set(b_f.astype(F32))
    pidx = lambda i: (jnp.minimum(i, nb - 1), 0)
    cidx = lambda i: (0, 0)
    return pl.pallas_call(
        functools.partial(_forget_body, nb, seq, dec_seq, ck),
        grid=(nb + 1,),
        in_specs=[pl.BlockSpec((seq, d), pidx), pl.BlockSpec((ms, d), cidx),
                  pl.BlockSpec((d, LANES), cidx), pl.BlockSpec((1, LANES), cidx)],
        out_specs=[pl.BlockSpec((seq, LANES), pidx), pl.BlockSpec((seq, LANES), pidx),
                   pl.BlockSpec((ms, LANES), cidx), pl.BlockSpec((ms, LANES), cidx)],
        out_shape=[jax.ShapeDtypeStruct((mp, LANES), F32), jax.ShapeDtypeStruct((mp, LANES), F32),
                   jax.ShapeDtypeStruct((ms, LANES), F32), jax.ShapeDtypeStruct((ms, LANES), F32)],
        compiler_params=_params(1), name="forget_gates",
    )(hp, hs, wpad, bpad)


def _shifted(x, hist, k, width, t_in_seq):
    n = x.shape[0]
    xk = pltpu.roll(x, k, axis=0)
    if hist is None:
        return xk
    back = (width - 1 - k) % n
    hk = hist if back == 0 else pltpu.roll(hist, n - back, axis=0)
    return jnp.where(t_in_seq >= k, xk, hk)


def _conv_rows(x, hist, wref, width, t_in_seq):
    acc = x * wref[width - 1:width, :]
    for k in range(1, width):
        acc += _shifted(x, hist, k, width, t_in_seq) * wref[width - 1 - k:width - k, :]
    return acc


def _sconv_body(nt, tiles_per_seq, width, dec_seq, gbp, gcp, up, gbs, gcs, us, hs, w, yp, ys, zp_tail, zs, carry):
    i = pl.program_id(1)

    @pl.when(i < nt)
    def _():
        t0 = i % tiles_per_seq
        x = gcp[...] * up[...]
        ts = x.shape[0]

        @pl.when(t0 == 0)
        def _():
            carry[...] = jnp.zeros_like(carry)

        yp[...] = (gbp[...] * _conv_rows(x, None, w, width, None)).astype(yp.dtype)
        head = x[:SUBLANES, :]
        t8 = lax.broadcasted_iota(jnp.int32, head.shape, 0)
        yp[:SUBLANES, :] = (gbp[:SUBLANES, :] * _conv_rows(head, carry[...], w, width, t8)).astype(yp.dtype)
        tail = x[ts - SUBLANES:, :]
        carry[...] = pltpu.roll(tail, width - 1, axis=0)
        zp_tail[0] = tail

    @pl.when(i == nt)
    def _():
        x = gcs[...] * us[...]
        t = lax.broadcasted_iota(jnp.int32, x.shape, 0) % dec_seq
        ys[...] = (gbs[...] * _conv_rows(x, hs[...], w, width, t)).astype(ys.dtype)
        zs[...] = x


def _pad_hist(state, dec_seq):
    b, hw, c = state.shape
    assert dec_seq >= hw
    return jnp.concatenate([state.astype(F32), jnp.zeros((b, dec_seq - hw, c), F32)], axis=1).reshape(b * dec_seq, c)


def _sconv_pair(gcu, state, w, nb, seq, dec_seq):
    gp, gs = gcu
    mp = gp.shape[0]
    ms = gs.shape[0]
    ch = gp.shape[1] // 3
    tc = 512
    ncb = ch // tc
    ts = _row_tile(seq, 512)
    tps = seq // ts
    nt = mp // ts
    pr = lambda j, i: jnp.minimum(i, nt - 1)
    in_specs = []
    for part in range(3):
        in_specs.append(pl.BlockSpec((ts, tc), lambda j, i, part=part: (pr(j, i), part * ncb + j)))
    for part in range(3):
        in_specs.append(pl.BlockSpec((ms, tc), lambda j, i, part=part: (0, part * ncb + j)))
    in_specs += [pl.BlockSpec((ms, tc), lambda j, i: (0, j)), pl.BlockSpec((SC_WIDTH, tc), lambda j, i: (0, j))]
    out_specs = [pl.BlockSpec((ts, tc), lambda j, i: (pr(j, i), j)),
                 pl.BlockSpec((ms, tc), lambda j, i: (0, j)),
                 pl.BlockSpec((1, SUBLANES, tc), lambda j, i: (pr(j, i) // tps, 0, j)),
                 pl.BlockSpec((ms, tc), lambda j, i: (0, j))]
    out_shape = [jax.ShapeDtypeStruct((mp, ch), BF16), jax.ShapeDtypeStruct((ms, ch), BF16),
                 jax.ShapeDtypeStruct((nb, SUBLANES, ch), F32), jax.ShapeDtypeStruct((ms, ch), F32)]
    yp, ys, zp_tail, zs = pl.pallas_call(
        functools.partial(_sconv_body, nt, tps, SC_WIDTH, dec_seq),
        grid=(ncb, nt + 1), in_specs=in_specs, out_specs=out_specs, out_shape=out_shape,
        scratch_shapes=[pltpu.VMEM((SUBLANES, tc), F32)],
        compiler_params=_params(2), name="short_conv",
    )(gp, gp, gp, gs, gs, gs, _pad_hist(state, dec_seq), w.astype(F32))
    hw = SC_WIDTH - 1
    new_p = zp_tail[:, SUBLANES - hw:, :]
    new_s = zs.reshape(ms // dec_seq, dec_seq, ch)[:, dec_seq - hw:, :]
    return (yp, ys), new_p, new_s


def _fox_prompt_body(tq, hps, q, k, v, fr, o, m_s, l_s, acc_s):
    qi = pl.program_id(2)
    dh = ATT_HEAD_DIM
    m_s[...] = jnp.full_like(m_s, NEG_BIG)
    l_s[...] = jnp.zeros_like(l_s)
    acc_s[...] = jnp.zeros_like(acc_s)

    def step(ki, masked):
        start = pl.multiple_of(ki * tq, tq)
        for hh in range(hps):
            cols = slice(hh * dh, (hh + 1) * dh)
            kb = k[pl.ds(start, tq), cols].astype(BF16)
            vb = v[pl.ds(start, tq), cols].astype(BF16)
            s = lax.dot_general(q[:, cols], kb, NT_DIMS, preferred_element_type=F32) - fr[0, hh, ki]
            if masked:
                r = lax.broadcasted_iota(jnp.int32, s.shape, 0)
                c = lax.broadcasted_iota(jnp.int32, s.shape, 1)
                s = jnp.where(r >= c, s, NEG_BIG)
            m_prev = m_s[hh]
            m_new = jnp.maximum(m_prev, jnp.max(s, axis=-1, keepdims=True))
            p = jnp.exp(s - jnp.tile(m_new, (1, tq // dh)))
            alpha = jnp.exp(m_prev - m_new)
            l_s[hh] = alpha * l_s[hh] + jnp.sum(p, axis=-1, keepdims=True)
            acc_s[hh] = alpha * acc_s[hh] + jnp.dot(p.astype(BF16), vb, preferred_element_type=F32)
            m_s[hh] = m_new

    def loop_body(ki, carry):
        step(ki, False)
        return carry

    lax.fori_loop(0, qi, loop_body, 0)
    step(qi, True)
    for hh in range(hps):
        o[:, hh * dh:(hh + 1) * dh] = (acc_s[hh] / l_s[hh]).astype(o.dtype)


def _fox_prompt(q, k, v, fcum, nb, seq):
    mp = q.shape[0]
    dh = ATT_HEAD_DIM
    nh = q.shape[1] // dh
    tq = _row_tile(seq, 512)
    nq = seq // tq
    frow = jnp.transpose(fcum, (0, 2, 1)).reshape(nb, nh, nq, 1, tq)
    hps = 2 if nh % 2 == 0 else 1
    gw = hps * dh
    return pl.pallas_call(
        functools.partial(_fox_prompt_body, tq, hps),
        grid=(nb, nh // hps, nq),
        in_specs=[pl.BlockSpec((tq, gw), lambda b, h, i: (b * nq + i, h)),
                  pl.BlockSpec((seq, gw), lambda b, h, i: (b, h)),
                  pl.BlockSpec((seq, gw), lambda b, h, i: (b, h)),
                  pl.BlockSpec((1, hps, nq, 1, tq), lambda b, h, i: (b, h, 0, 0, 0))],
        out_specs=pl.BlockSpec((tq, gw), lambda b, h, i: (b * nq + i, h)),
        out_shape=jax.ShapeDtypeStruct((mp, nh * dh), BF16),
        scratch_shapes=[pltpu.VMEM((hps, tq, dh), F32), pltpu.VMEM((hps, tq, dh), F32),
                        pltpu.VMEM((hps, tq, dh), F32)],
        compiler_params=_params(3), name="fox_prompt",
    )(q, k, v, frow)


def _past_bias_body(n_pages, page, nh, *refs):
    lf_refs = refs[1:1 + n_pages]
    out = refs[1 + n_pages]
    rows, later_in_page, same_head = refs[2 + n_pages:]
    w = page * nh
    npp = rows.shape[0]

    @pl.when(pl.program_id(0) == 0)
    def _():
        a = lax.broadcasted_iota(jnp.int32, (w, w), 0)
        c = lax.broadcasted_iota(jnp.int32, (w, w), 1)
        same = (a % nh) == (c % nh)
        same_head[...] = same.astype(BF16)
        later_in_page[...] = (same & ((a // nh) > (c // nh))).astype(BF16)
        rows[...] = jnp.zeros_like(rows)

    for p in range(n_pages):
        rows[p:p + 1, :] = lf_refs[p][0]
    lf3 = jnp.concatenate(_split3(rows[...]), axis=0)
    within3 = jnp.dot(lf3, later_in_page[...], preferred_element_type=F32)
    totals3 = jnp.dot(lf3, same_head[...], preferred_element_type=F32)
    within = within3[:npp] + within3[npp:2 * npp] + within3[2 * npp:]
    totals = totals3[:npp] + totals3[npp:2 * npp] + totals3[2 * npp:]
    pr = lax.broadcasted_iota(jnp.int32, (npp, npp), 0)
    pc = lax.broadcasted_iota(jnp.int32, (npp, npp), 1)
    later_pages = (pc > pr).astype(BF16)
    after3 = jnp.dot(later_pages, jnp.concatenate(_split3(totals), axis=1), preferred_element_type=F32)
    after = after3[:, :w] + after3[:, w:2 * w] + after3[:, 2 * w:]
    out[0] = (within + after)[:n_pages, :]


def _past_bias(cache_lf, page_table):
    n_phys, page, nh = cache_lf.shape
    nb, n_pages = page_table.shape
    w = page * nh
    np_pad = -(-n_pages // LANES) * LANES
    lf = cache_lf.reshape(n_phys, 1, w)
    in_specs = [pl.BlockSpec((1, 1, w), lambda b, pt, p=p: (pt[b, p], 0, 0)) for p in range(n_pages)]
    grid_spec = pltpu.PrefetchScalarGridSpec(
        num_scalar_prefetch=1, grid=(nb,), in_specs=in_specs,
        out_specs=pl.BlockSpec((1, n_pages, w), lambda b, pt: (b, 0, 0)),
        scratch_shapes=[pltpu.VMEM((np_pad, w), F32), pltpu.VMEM((w, w), BF16), pltpu.VMEM((w, w), BF16)])
    return pl.pallas_call(
        functools.partial(_past_bias_body, n_pages, page, nh),
        grid_spec=grid_spec, out_shape=jax.ShapeDtypeStruct((nb, n_pages, w), F32),
        compiler_params=_params(1), name="fox_past_bias",
    )(page_table, *([lf] * n_pages))


def _fox_sample_body(pps, n_steps, nh, dec_seq, *refs):
    qall, knew, vnew, frow_new, rb = refs[1:6]
    krefs = refs[6:6 + pps]
    vrefs = refs[6 + pps:6 + 2 * pps]
    o = refs[6 + 2 * pps]
    m_s, l_s, acc_s, s_scr = refs[7 + 2 * pps:]
    step = pl.program_id(1)
    qv = qall[0]
    nrow = qv.shape[0]

    @pl.when(step == 0)
    def _():
        kn = knew[0].astype(BF16)
        g = lax.dot_general(qv, kn, NT_DIMS, preferred_element_type=F32)
        r = lax.broadcasted_iota(jnp.int32, g.shape, 0)
        c = lax.broadcasted_iota(jnp.int32, g.shape, 1)
        valid = ((c % nh) == (r // dec_seq)) & ((c // nh) <= (r % dec_seq))
        s = jnp.where(valid, g - frow_new[0], NEG_BIG)
        m = jnp.max(s, axis=-1, keepdims=True)
        p = jnp.exp(s - m)
        m_s[...] = m
        l_s[...] = jnp.sum(p, axis=-1, keepdims=True)
        acc_s[...] = jnp.dot(p.astype(BF16), vnew[0].astype(BF16), preferred_element_type=F32)

    lanes = krefs[0].shape[1]
    r = lax.broadcasted_iota(jnp.int32, (nrow, lanes), 0)
    c = lax.broadcasted_iota(jnp.int32, (nrow, lanes), 1)
    head_ok = (c % nh) == (r // dec_seq)
    m_prev = m_s[...]
    m_new = m_prev
    for i in range(pps):
        g = lax.dot_general(qv, krefs[i][0].astype(BF16), NT_DIMS, preferred_element_type=F32)
        s = jnp.where(head_ok, g + rb[0, i], NEG_BIG)
        s_scr[i] = s
        m_new = jnp.maximum(m_new, jnp.max(s, axis=-1, keepdims=True))
    alpha = jnp.exp(m_prev - m_new)
    psum = acc = None
    for i in range(pps):
        p = jnp.exp(s_scr[i] - m_new)
        pv = jnp.dot(p.astype(BF16), vrefs[i][0].astype(BF16), preferred_element_type=F32)
        ps = jnp.sum(p, axis=-1, keepdims=True)
        psum = ps if psum is None else psum + ps
        acc = pv if acc is None else acc + pv
    l_s[...] = alpha * l_s[...] + psum
    acc_s[...] = alpha * acc_s[...] + acc
    m_s[...] = m_new

    @pl.when(step == n_steps - 1)
    def _():
        o[0] = (acc_s[...] / l_s[...]).astype(o.dtype)


def _fox_sample(q, k, v, fn, cache_k, cache_v, cache_lf, page_table, dec_seq):
    ms = q.shape[0]
    nb = ms // dec_seq
    nh, dh = ATT_HEADS, ATT_HEAD_DIM
    n_phys, page = cache_k.shape[0], cache_k.shape[1]
    n_pages = page_table.shape[1]
    pps = 8
    while n_pages % pps:
        pps //= 2
    n_steps = n_pages // pps
    nrow = nh * dec_seq
    rbias = _past_bias(cache_lf, page_table).reshape(nb, n_pages, 1, page * nh)
    qall = jnp.transpose(q.reshape(nb, dec_seq, nh, dh), (0, 2, 1, 3)).reshape(nb, nrow, dh)
    knew = k.reshape(nb, dec_seq * nh, dh)
    vnew = v.reshape(nb, dec_seq * nh, dh)
    frow_new = fn.reshape(nb, 1, dec_seq * nh)
    ck = cache_k.reshape(n_phys, page * nh, dh)
    cv = cache_v.reshape(n_phys, page * nh, dh)
    per_b = lambda shape: pl.BlockSpec((1,) + shape, lambda b, s, pt: (b, 0, 0))
    in_specs = [per_b((nrow, dh)), per_b((dec_seq * nh, dh)), per_b((dec_seq * nh, dh)),
                per_b((1, dec_seq * nh)),
                pl.BlockSpec((1, pps, 1, page * nh), lambda b, s, pt: (b, s, 0, 0))]
    for _ in range(2):
        in_specs += [pl.BlockSpec((1, page * nh, dh), lambda b, s, pt, i=i: (pt[b, s * pps + i], 0, 0))
                     for i in range(pps)]
    grid_spec = pltpu.PrefetchScalarGridSpec(
        num_scalar_prefetch=1, grid=(nb, n_steps), in_specs=in_specs,
        out_specs=pl.BlockSpec((1, nrow, dh), lambda b, s, pt: (b, 0, 0)),
        scratch_shapes=[pltpu.VMEM((nrow, 1), F32), pltpu.VMEM((nrow, 1), F32), pltpu.VMEM((nrow, dh), F32),
                        pltpu.VMEM((pps, nrow, page * nh), F32)])
    o = pl.pallas_call(
        functools.partial(_fox_sample_body, pps, n_steps, nh, dec_seq),
        grid_spec=grid_spec, out_shape=jax.ShapeDtypeStruct((nb, nrow, dh), BF16),
        compiler_params=_params(2), name="fox_sample",
    )(page_table, qall, knew, vnew, frow_new, rbias, *([ck] * pps), *([cv] * pps))
    return jnp.transpose(o.reshape(nb, nh, dec_seq, dh), (0, 2, 1, 3)).reshape(ms, nh * dh)


def _ssd_body(t_in, nseq, carried, xs_r, z_r, b_r, c_r, dt_r, wx_r, wb_r, wc_r, bx_r, bb_r, bc_r, a_r, d_r, gn_r,
              sel_r, ex_r, *rest):
    if carried:
        hx_r, hb_r, hc_r, s0_r, y_o, st_o, st, carry = rest
    else:
        hx_r = hb_r = hc_r = s0_r = None
        y_o, st_o, st, carry = rest
    ci = pl.program_id(2)
    nc = pl.num_programs(2)
    L = t_in * nseq
    P, N = SSM_HEAD_DIM, D_STATE
    gw = xs_r.shape[1]
    R = gw // P

    @pl.when(ci == 0)
    def _():
        if carried:
            for s in range(nseq):
                st[:, s * N:(s + 1) * N] = s0_r[s, 0]
        else:
            st[...] = jnp.zeros_like(st)
            carry[...] = jnp.zeros_like(carry)

    def conv_act(raw_r, hist_r, w_r, bias_r, lo, hi):
        x = raw_r[...]
        if carried:
            t = lax.broadcasted_iota(jnp.int32, x.shape, 0) % t_in
            y = _conv_rows(x, hist_r[...], w_r, SSM_CONV, t)
        else:
            y = _conv_rows(x, None, w_r, SSM_CONV, None)
            head = x[:SUBLANES, :]
            t8 = lax.broadcasted_iota(jnp.int32, head.shape, 0)
            y8 = _conv_rows(head, carry[:, lo:hi], w_r, SSM_CONV, t8)
            y = jnp.concatenate([y8, y[SUBLANES:, :]], axis=0)
            carry[:, lo:hi] = pltpu.roll(x[L - SUBLANES:, :], SSM_CONV - 1, axis=0)
        return _silu(y + bias_r[...])

    xs = conv_act(xs_r, hx_r, wx_r, bx_r, 0, gw)
    bmf = conv_act(b_r, hb_r, wb_r, bb_r, gw, gw + N)
    cmf = conv_act(c_r, hc_r, wc_r, bc_r, gw + N, gw + 2 * N)
    bm, cm = bmf.astype(BF16), cmf.astype(BF16)
    dt = jnp.dot(jnp.concatenate(_split3(dt_r[...]), axis=1), sel_r[...], preferred_element_type=F32)
    r = lax.broadcasted_iota(jnp.int32, (L, L), 0)
    c = lax.broadcasted_iota(jnp.int32, (L, L), 1)
    if nseq == 1:
        tri = r >= c
        acum = jnp.dot(tri.astype(F32), dt * a_r[...], preferred_element_type=F32, precision=HIGHEST)
        a_tot = acum[L - 1:L, :]
    else:
        same = (r // t_in) == (c // t_in)
        tri = same & (r >= c)
        sums = jnp.dot(jnp.concatenate([tri.astype(F32), same.astype(F32)], axis=0), dt * a_r[...],
                       preferred_element_type=F32, precision=HIGHEST)
        acum, a_tot = sums[:L], sums[L:]
    acum_t = acum.T
    dt_t = dt.T
    cols = jnp.concatenate([dt * jnp.exp(a_tot - acum), jnp.exp(acum)], axis=0)
    wide = jnp.dot(jnp.concatenate(_split3(cols), axis=1), ex_r[...], preferred_element_type=F32)
    w_state, w_off = wide[:L], wide[L:]
    cb = lax.dot_general(cm, bm, NT_DIMS, preferred_element_type=F32)
    if nseq == 1:
        cblk, bblk = cm, bm
        decay_t = acum_t
    else:
        seq_of_row = lax.broadcasted_iota(jnp.int32, (L, N), 0) // t_in
        cblk = jnp.concatenate([jnp.where(seq_of_row == s, cmf, 0.0) for s in range(nseq)], axis=1).astype(BF16)
        bblk = jnp.concatenate([jnp.where(seq_of_row == s, bmf, 0.0) for s in range(nseq)], axis=1).astype(BF16)
        decay_t = a_tot.T
    s_prev = st[...]
    y_off = lax.dot_general(cblk, s_prev.astype(BF16), NT_DIMS, preferred_element_type=F32) * w_off
    xb = xs.astype(BF16)
    ys = []
    for h in range(R):
        decay = jnp.exp(jnp.where(tri, acum[:, h:h + 1] - acum_t[h:h + 1, :], -jnp.inf))
        m = (cb * decay * dt_t[h:h + 1, :]).astype(BF16)
        ys.append(jnp.dot(m, xb[:, h * P:(h + 1) * P], preferred_element_type=F32))
    new_states = lax.dot_general((xs * w_state).astype(BF16), bblk, TN_DIMS, preferred_element_type=F32)
    for s in range(nseq):
        last = (s + 1) * t_in - 1
        for h in range(R):
            blk = (slice(h * P, (h + 1) * P), slice(s * N, (s + 1) * N))
            st[blk] = s_prev[blk] * jnp.exp(decay_t[h:h + 1, last:last + 1]) + new_states[blk]
    y = jnp.concatenate(ys, axis=1) + y_off + d_r[...] * xs
    y = y * _silu(z_r[...])
    y = y * lax.rsqrt(jnp.mean(y * y, axis=-1, keepdims=True) + EPS)
    y_o[...] = (y * gn_r[...]).astype(y_o.dtype)

    @pl.when(ci == nc - 1)
    def _():
        for s in range(nseq):
            st_o[s, 0] = st[:, s * N:(s + 1) * N]


def _ssd(xbc, z, dt, w_conv, b_conv, a_pad, d_exp, gn_w, hist, init_state, nb, seq):
    G, N, P = SSM_GROUPS, D_STATE, SSM_HEAD_DIM
    inner = z.shape[1]
    gw = inner // G
    rg = gw // P
    L = SSM_CHUNK
    carried = init_state is not None
    if carried:
        t_in, nc = seq, 1
        assert L % seq == 0 and nb % (L // seq) == 0, (nb, seq)
    else:
        t_in, nc = L, seq // L
        assert seq % L == 0, seq
    nseq = L // t_in
    nbb = nb // nseq
    nx, nbc = inner // gw, inner // N
    row = lambda b, g, c: b * nc + c
    xcol = lambda b, g, c: g
    bcol = lambda b, g, c: nbc + g
    ccol = lambda b, g, c: nbc + G + g

    def triple(rows_, row_idx):
        return [pl.BlockSpec((rows_, gw), lambda b, g, c: (row_idx(b, g, c), xcol(b, g, c))),
                pl.BlockSpec((rows_, N), lambda b, g, c: (row_idx(b, g, c), bcol(b, g, c))),
                pl.BlockSpec((rows_, N), lambda b, g, c: (row_idx(b, g, c), ccol(b, g, c)))]

    zero_row = lambda b, g, c: 0
    x_spec, b_spec, c_spec = triple(L, row)
    in_specs = [x_spec, pl.BlockSpec((L, gw), lambda b, g, c: (row(b, g, c), g)), b_spec, c_spec,
                pl.BlockSpec((L, LANES), lambda b, g, c: (row(b, g, c), 0))]
    in_specs += triple(SSM_CONV, zero_row) + triple(1, zero_row)
    in_specs += [pl.BlockSpec((1, LANES), lambda b, g, c: (0, g)),
                 pl.BlockSpec((1, gw), lambda b, g, c: (0, g)),
                 pl.BlockSpec((1, gw), lambda b, g, c: (0, g)),
                 pl.BlockSpec((None, 3 * LANES, LANES), lambda b, g, c: (g, 0, 0)),
                 pl.BlockSpec((3 * LANES, gw), lambda b, g, c: (0, 0))]
    lane_head = jnp.arange(3 * LANES) % LANES
    widen = (lane_head[:, None] == jnp.arange(gw)[None, :] // P).astype(BF16)
    lane = jnp.arange(LANES)
    pick = ((lane_head[None, :, None] == jnp.arange(G)[:, None, None] * rg + lane[None, None, :])
            & (lane[None, None, :] < rg)).astype(BF16)
    w_conv = w_conv.astype(F32)
    b_conv = b_conv.reshape(1, -1).astype(F32)
    args = [xbc, z, xbc, xbc, dt, w_conv, w_conv, w_conv, b_conv, b_conv, b_conv, a_pad, d_exp, gn_w, pick, widen]
    if carried:
        in_specs += triple(L, row) + [pl.BlockSpec((nseq, 1, gw, N), lambda b, g, c: (b, g, 0, 0))]
        args += [hist, hist, hist, init_state]
    y, st = pl.pallas_call(
        functools.partial(_ssd_body, t_in, nseq, carried),
        grid=(nbb, G, nc), in_specs=in_specs,
        out_specs=[pl.BlockSpec((L, gw), lambda b, g, c: (row(b, g, c), g)),
                   pl.BlockSpec((nseq, 1, gw, N), lambda b, g, c: (b, g, 0, 0))],
        out_shape=[jax.ShapeDtypeStruct((nb * seq, inner), BF16), jax.ShapeDtypeStruct((nb, G, gw, N), F32)],
        scratch_shapes=[pltpu.VMEM((gw, nseq * N), F32), pltpu.VMEM((SUBLANES, gw + 2 * N), F32)],
        compiler_params=_params(3), name="ssd",
    )(*args)
    return y, st


def _ffn(x, norm_w, w_gate, w_up, w_down, layer):
    h = _rmsnorm_pair(x, norm_w, BF16)
    a = _mm_pair([h], [_W(w_gate, layer), _W(w_up, layer)], out_dtype=BF16, mode="swiglu", name="ffn_gate_up")
    return _mm_pair([a], [_W(w_down, layer)], res=x, name="ffn_down")


def kernel(x_prompt, x_sample, cache_k, cache_v, cache_logf, page_table, state_sconv, state_mconv, state_ssm, norm_mix, norm_ffn, norm_final, w_in_even, b_forget, w_sconv, w_out_even, w_in_ssm, w_mconv, b_mconv, dt_bias, a_log, d_skip, w_gnorm, w_out_ssm, w_gate, w_up, w_down):
    bp, seq, d = x_prompt.shape
    bs, dec_seq, _ = x_sample.shape
    mp, ms = bp * seq, bs * dec_seq
    nh, dh = ATT_HEADS, ATT_HEAD_DIM
    att = nh * dh
    sc_ch = state_sconv.shape[-1]
    x = (x_prompt.reshape(mp, d), x_sample.reshape(ms, d))

    i = 0
    h = _rmsnorm_pair(x, norm_mix[0], BF16)
    c0 = 3 * sc_ch
    w_in_t = jnp.swapaxes(w_in_even, 1, 2)
    win = lambda col0, n: _W(w_in_t, i, col0=col0, n=n, transposed=True)
    gcu = _mm_pair([h], [win(0, c0)], name="in_even_conv")
    q = _mm_pair([h], [win(c0, att)], out_dtype=BF16, scale=dh ** -0.5, name="in_even_q")
    k = _mm_pair([h], [win(c0 + att, att)], name="in_even_k")
    v = _mm_pair([h], [win(c0 + 2 * att, att)], name="in_even_v")
    lf_p, fc_p, lf_s, fc_s = _forget_pair(h, w_in_even[i, :, c0 + 3 * att:], b_forget[i], bp, seq, dec_seq)
    logf_p = lf_p[:, :nh].reshape(bp, seq, nh)
    logf_s = lf_s[:, :nh].reshape(bs, dec_seq, nh)
    y_a, sconv_p, sconv_s = _sconv_pair(gcu, state_sconv[i], w_sconv[i], bp, seq, dec_seq)
    yb_p = _fox_prompt(q[0], k[0], v[0], fc_p[:, :nh].reshape(bp, seq, nh), bp, seq)
    yb_s = _fox_sample(q[1], k[1], v[1], fc_s[:, :nh].reshape(bs, dec_seq, nh),
                       cache_k[i], cache_v[i], cache_logf[i], page_table, dec_seq)
    x = _mm_pair([y_a, (yb_p, yb_s)],
                 [_W(w_out_even, i, row0=0, k=sc_ch), _W(w_out_even, i, row0=sc_ch, k=att)], res=x, name="out_even")
    x = _ffn(x, norm_ffn[0], w_gate, w_up, w_down, 0)

    j = 0
    G = SSM_GROUPS
    n_heads = a_log.shape[1]
    rg = n_heads // G
    inner = n_heads * SSM_HEAD_DIM
    conv_dim = state_mconv.shape[-1]
    h = _rmsnorm_pair(x, norm_mix[1], BF16)
    w_ssm_t = jnp.swapaxes(w_in_ssm, 1, 2)
    z = _mm_pair([h], [_W(w_ssm_t, j, n=inner, transposed=True)], name="in_ssm_z")
    xbc = _mm_pair([h], [_W(w_ssm_t, j, col0=inner, n=conv_dim, transposed=True)], name="in_ssm_xbc")
    assert n_heads <= LANES
    w_dt = jnp.zeros((d, LANES), BF16).at[:, :n_heads].set(w_in_ssm[j, :, inner + conv_dim:].astype(BF16))
    b_dt = jnp.zeros((1, LANES), F32).at[0, :n_heads].set(dt_bias[j].astype(F32))
    dt = _mm_pair([h], [_W(w_dt)], mode="softplus", bias=b_dt, name="in_ssm_dt")
    a_pad = jnp.zeros((G, LANES), F32).at[:, :rg].set(-jnp.exp(a_log[j].astype(F32)).reshape(G, rg)).reshape(1, G * LANES)
    d_exp = jnp.repeat(d_skip[j].astype(F32), SSM_HEAD_DIM).reshape(1, inner)
    gn_w = w_gnorm[j].astype(F32).reshape(1, inner)
    hw = SSM_CONV - 1
    mconv_p = xbc[0].reshape(bp, seq, conv_dim)[:, seq - hw:, :]
    mconv_s = xbc[1].reshape(bs, dec_seq, conv_dim)[:, dec_seq - hw:, :]
    yp, ssm_p = _ssd(xbc[0], z[0], dt[0], w_mconv[j], b_mconv[j], a_pad, d_exp, gn_w, None, None, bp, seq)
    init = state_ssm[j].astype(F32).reshape(bs, G, rg * SSM_HEAD_DIM, D_STATE)
    ysm, ssm_s = _ssd(xbc[1], z[1], dt[1], w_mconv[j], b_mconv[j], a_pad, d_exp, gn_w,
                      _pad_hist(state_mconv[j], dec_seq), init, bs, dec_seq)
    x = _mm_pair([(yp, ysm)], [_W(w_out_ssm, j)], res=x, name="out_ssm")
    x = _ffn(x, norm_ffn[1], w_gate, w_up, w_down, 1)

    y = _rmsnorm_pair(x, norm_final, F32)
    st_shape = (n_heads, SSM_HEAD_DIM, D_STATE)
    return (y[0].reshape(bp, seq, d), y[1].reshape(bs, dec_seq, d),
            k[0].reshape(1, bp, seq, nh, dh), v[0].reshape(1, bp, seq, nh, dh), logf_p[None],
            sconv_p[None], mconv_p[None], ssm_p.reshape((1, bp) + st_shape),
            k[1].reshape(1, bs, dec_seq, nh, dh), v[1].reshape(1, bs, dec_seq, nh, dh), logf_s[None],
            sconv_s[None], mconv_s[None], ssm_s.reshape((1, bs) + st_shape))
```

```python
import functools

import jax
import jax.numpy as jnp
from jax import lax
from jax.experimental import pallas as pl
from jax.experimental.pallas import tpu as pltpu

F32 = jnp.float32
BF16 = jnp.bfloat16

SC_WIDTH = 3
ATT_HEADS = 8
ATT_HEAD_DIM = 128
SSM_HEAD_DIM = 64
SSM_GROUPS = 8
D_STATE = 128
SSM_CONV = 4
SSM_CHUNK = 128
EPS = 1e-6

LANES = 128
SUBLANES = 8
VMEM_LIMIT_BYTES = 56 * 1024 * 1024
NEG_BIG = -1e30
LOG2E = 1.4426950408889634

HIGHEST = lax.Precision.HIGHEST
NT_DIMS = (((1,), (1,)), ((), ()))
TN_DIMS = (((0,), (0,)), ((), ()))


def _params(n_grid_dims):
    return pltpu.CompilerParams(dimension_semantics=("arbitrary",) * n_grid_dims,
                                vmem_limit_bytes=VMEM_LIMIT_BYTES)


def _row_tile(mp, pref):
    t = pref
    while mp % t:
        t //= 2
    assert t >= SUBLANES, (mp, pref)
    return t


def _rmsnorm_body(nt, xp, xs, w, op, os):
    i = pl.program_id(0)

    def norm(x):
        y = x * lax.rsqrt(jnp.mean(x * x, axis=-1, keepdims=True) + EPS)
        return (y * w[...]).astype(op.dtype)

    @pl.when(i < nt)
    def _():
        op[...] = norm(xp[...])

    @pl.when(i == nt)
    def _():
        os[...] = norm(xs[...])


def _rmsnorm_pair(x, w, out_dtype):
    xp, xs = x
    mp, d = xp.shape
    ms = xs.shape[0]
    tm = _row_tile(mp, 512)
    nt = mp // tm
    pidx = lambda i: (jnp.minimum(i, nt - 1), 0)
    return pl.pallas_call(
        functools.partial(_rmsnorm_body, nt),
        grid=(nt + 1,),
        in_specs=[pl.BlockSpec((tm, d), pidx), pl.BlockSpec((ms, d), lambda i: (0, 0)),
                  pl.BlockSpec((1, d), lambda i: (0, 0))],
        out_specs=[pl.BlockSpec((tm, d), pidx), pl.BlockSpec((ms, d), lambda i: (0, 0))],
        out_shape=[jax.ShapeDtypeStruct((mp, d), out_dtype), jax.ShapeDtypeStruct((ms, d), out_dtype)],
        compiler_params=_params(1), name="rmsnorm",
    )(xp, xs, w.reshape(1, d).astype(F32))


def _softplus(x):
    return jnp.maximum(x, 0.0) + jnp.log1p(jnp.exp(-jnp.abs(x)))


def _silu(x):
    return x * (1.0 / (1.0 + jnp.exp(-x)))


def _split3(v):
    hi = v.astype(BF16)
    r1 = v - hi.astype(F32)
    mid = r1.astype(BF16)
    lo = (r1 - mid.astype(F32)).astype(BF16)
    return hi, mid, lo


class _W:
    def __init__(self, arr, layer=None, row0=0, k=None, col0=0, n=None, transposed=False):
        self.arr, self.layer, self.row0, self.col0, self.transposed = arr, layer, row0, col0, transposed
        kdim, ndim = (-1, -2) if transposed else (-2, -1)
        self.k = arr.shape[kdim] if k is None else k
        self.n = arr.shape[ndim] if n is None else n

    def block(self, tn):
        return (tn, self.k) if self.transposed else (self.k, tn)

    def spec(self, tn):
        assert self.row0 % self.k == 0 and self.col0 % tn == 0
        rb, cb = self.row0 // self.k, self.col0 // tn
        if self.transposed:
            idx = lambda j, i: (cb + j, rb)
        else:
            idx = lambda j, i: (rb, cb + j)
        if self.layer is None:
            return pl.BlockSpec(self.block(tn), idx)
        layer = self.layer
        return pl.BlockSpec((None,) + self.block(tn), lambda j, i: (layer,) + idx(j, i))


def _mm_body(nt, n_lhs, mode, has_res, has_bias, scale, w_is_f32, w_transposed, *refs):
    i = pl.program_id(1)

    def mm(x, idx):
        dims = NT_DIMS if w_transposed[idx] else (((1,), (0,)), ((), ()))
        return lax.dot_general(x, wrefs[idx][...], dims, preferred_element_type=F32)

    xrefs = refs[:2 * n_lhs]
    n_w = len(w_is_f32)
    wrefs = list(refs[2 * n_lhs:2 * n_lhs + n_w])
    pos = 2 * n_lhs + n_w
    rp = rs = bias = None
    if has_res:
        rp, rs = refs[pos], refs[pos + 1]
        pos += 2
    if has_bias:
        bias = refs[pos]
        pos += 1
    op, os = refs[pos], refs[pos + 1]
    scratch = list(refs[pos + 2:])

    for idx in range(n_w):
        if w_is_f32[idx]:
            wsrc, wdst = wrefs[idx], scratch.pop(0)

            @pl.when(i == 0)
            def _(wsrc=wsrc, wdst=wdst):
                wdst[...] = wsrc[...].astype(BF16)

            wrefs[idx] = wdst

    def compute(which, out, res):
        if mode == "swiglu":
            x = xrefs[which][...]
            acc = _silu(mm(x, 0)) * mm(x, 1)
        else:
            acc = mm(xrefs[which][...], 0)
            for k in range(1, n_lhs):
                acc += mm(xrefs[2 * k + which][...], k)
        if scale is not None:
            acc = acc * scale
        if has_bias:
            acc = acc + bias[...]
        if mode == "softplus":
            acc = _softplus(acc)
        if has_res:
            acc = res[...] + acc
        out[...] = acc.astype(out.dtype)

    @pl.when(i == 0)
    def _():
        compute(1, os, rs)

    @pl.when(i > 0)
    def _():
        compute(0, op, rp)


def _mm_pair(xs_list, ws_list, *, out_dtype=F32, mode="plain", res=None, bias=None, scale=None, name="matmul"):
    n_lhs = len(xs_list)
    mp = xs_list[0][0].shape[0]
    ms = xs_list[0][1].shape[0]
    n = ws_list[0].n
    kmax = max(w.k for w in ws_list)
    tm = _row_tile(mp, 1024 if kmax <= 4096 else 512)
    nt = mp // tm
    if n % 1024 == 0 and kmax <= 2048 and mode != "swiglu":
        tn = 1024
    elif n % 512 == 0:
        tn = 512
    else:
        assert n % LANES == 0 and n <= 512, n
        tn = n
    pidx_x = lambda j, i: (jnp.maximum(i - 1, 0), 0)
    pidx_o = lambda j, i: (jnp.maximum(i - 1, 0), j)
    in_specs, args = [], []
    for xp, xs in xs_list:
        kk = xp.shape[1]
        in_specs += [pl.BlockSpec((tm, kk), pidx_x), pl.BlockSpec((ms, kk), lambda j, i: (0, 0))]
        args += [xp, xs]
    w_is_f32, scratch = [], []
    for w in ws_list:
        in_specs.append(w.spec(tn))
        args.append(w.arr)
        w_is_f32.append(w.arr.dtype != BF16)
        if w_is_f32[-1]:
            scratch.append(pltpu.VMEM(w.block(tn), BF16))
    if res is not None:
        in_specs += [pl.BlockSpec((tm, tn), pidx_o), pl.BlockSpec((ms, tn), lambda j, i: (0, j))]
        args += [res[0], res[1]]
    if bias is not None:
        in_specs.append(pl.BlockSpec((1, tn), lambda j, i: (0, j)))
        args.append(bias.reshape(1, n).astype(F32))
    outs = pl.pallas_call(
        functools.partial(_mm_body, nt, n_lhs, mode, res is not None, bias is not None, scale, tuple(w_is_f32),
                          tuple(w.transposed for w in ws_list)),
        grid=(n // tn, nt + 1),
        in_specs=in_specs,
        out_specs=[pl.BlockSpec((tm, tn), pidx_o), pl.BlockSpec((ms, tn), lambda j, i: (0, j))],
        out_shape=[jax.ShapeDtypeStruct((mp, n), out_dtype), jax.ShapeDtypeStruct((ms, n), out_dtype)],
        scratch_shapes=scratch,
        compiler_params=_params(2), name=name,
    )(*args)
    return outs[0], outs[1]


def _log_sigmoid(x):
    return -_softplus(-x)


def _forget_body(nb, seq, dec_seq, ck, xp, xs, w, b, lp, fp, ls, fs):
    i = pl.program_id(0)

    @pl.when(i < nb)
    def _():
        r = lax.broadcasted_iota(jnp.int32, (ck, ck), 0)
        c = lax.broadcasted_iota(jnp.int32, (ck, ck), 1)
        tri = (r >= c).astype(F32)
        carry = jnp.zeros((1, LANES), F32)
        for t in range(seq // ck):
            f = jnp.dot(xp[t * ck:(t + 1) * ck, :], w[...], preferred_element_type=F32) + b[...]
            lf = _log_sigmoid(f)
            lp[t * ck:(t + 1) * ck, :] = lf
            cs = jnp.dot(tri, lf, preferred_element_type=F32, precision=HIGHEST) + carry
            fp[t * ck:(t + 1) * ck, :] = cs
            carry = cs[ck - 1:ck, :]

    @pl.when(i == nb)
    def _():
        ms = xs.shape[0]
        r = lax.broadcasted_iota(jnp.int32, (ms, ms), 0)
        c = lax.broadcasted_iota(jnp.int32, (ms, ms), 1)
        tri = ((r // dec_seq == c // dec_seq) & (r >= c)).astype(F32)
        f = jnp.dot(xs[...], w[...], preferred_element_type=F32) + b[...]
        lf = _log_sigmoid(f)
        ls[...] = lf
        fs[...] = jnp.dot(tri, lf, preferred_element_type=F32, precision=HIGHEST)


def _forget_pair(h, w_f, b_f, nb, seq, dec_seq):
    hp, hs = h
    mp, d = hp.shape
    ms = hs.shape[0]
    assert ms <= 512 and ms % SUBLANES == 0
    ck = _row_tile(seq, 256)
    n_h = w_f.shape[1]
    wpad = jnp.zeros((d, LANES), BF16).at[:, :n_h].set(w_f.astype(BF16))
    bpad = jnp.zeros((1, LANES), F32).at[0, :n_h].set(b_f.astype(F32))
    pidx = lambda i: (jnp.minimum(i, nb - 1), 0)
    cidx = lambda i: (0, 0)
    return pl.pallas_call(
        functools.partial(_forget_body, nb, seq, dec_seq, ck),
        grid=(nb + 1,),
        in_specs=[pl.BlockSpec((seq, d), pidx), pl.BlockSpec((ms, d), cidx),
                  pl.BlockSpec((d, LANES), cidx), pl.BlockSpec((1, LANES), cidx)],
        out_specs=[pl.BlockSpec((seq, LANES), pidx), pl.BlockSpec((seq, LANES), pidx),
                   pl.BlockSpec((ms, LANES), cidx), pl.BlockSpec((ms, LANES), cidx)],
        out_shape=[jax.ShapeDtypeStruct((mp, LANES), F32), jax.ShapeDtypeStruct((mp, LANES), F32),
                   jax.ShapeDtypeStruct((ms, LANES), F32), jax.ShapeDtypeStruct((ms, LANES), F32)],
        compiler_params=_params(1), name="forget_gates",
    )(hp, hs, wpad, bpad)


def _shifted(x, hist, k, width, t_in_seq):
    n = x.shape[0]
    xk = pltpu.roll(x, k, axis=0)
    if hist is None:
        return xk
    back = (width - 1 - k) % n
    hk = hist if back == 0 else pltpu.roll(hist, n - back, axis=0)
    return jnp.where(t_in_seq >= k, xk, hk)


def _conv_rows(x, hist, wref, width, t_in_seq):
    acc = x * wref[width - 1:width, :]
    for k in range(1, width):
        acc += _shifted(x, hist, k, width, t_in_seq) * wref[width - 1 - k:width - k, :]
    return acc


def _sconv_body(nt, tiles_per_seq, width, dec_seq, gbp, gcp, up, gbs, gcs, us, hs, w, yp, ys, zp_tail, zs, carry):
    i = pl.program_id(1)

    @pl.when(i < nt)
    def _():
        t0 = i % tiles_per_seq
        x = gcp[...] * up[...]
        ts = x.shape[0]

        @pl.when(t0 == 0)
        def _():
            carry[...] = jnp.zeros_like(carry)

        yp[...] = (gbp[...] * _conv_rows(x, None, w, width, None)).astype(yp.dtype)
        head = x[:SUBLANES, :]
        t8 = lax.broadcasted_iota(jnp.int32, head.shape, 0)
        yp[:SUBLANES, :] = (gbp[:SUBLANES, :] * _conv_rows(head, carry[...], w, width, t8)).astype(yp.dtype)
        tail = x[ts - SUBLANES:, :]
        carry[...] = pltpu.roll(tail, width - 1, axis=0)
        zp_tail[0] = tail

    @pl.when(i == nt)
    def _():
        x = gcs[...] * us[...]
        t = lax.broadcasted_iota(jnp.int32, x.shape, 0) % dec_seq
        ys[...] = (gbs[...] * _conv_rows(x, hs[...], w, width, t)).astype(ys.dtype)
        zs[...] = x


def _pad_hist(state, dec_seq):
    b, hw, c = state.shape
    assert dec_seq >= hw
    return jnp.concatenate([state.astype(F32), jnp.zeros((b, dec_seq - hw, c), F32)], axis=1).reshape(b * dec_seq, c)


def _sconv_pair(gcu, state, w, nb, seq, dec_seq):
    gp, gs = gcu
    mp = gp.shape[0]
    ms = gs.shape[0]
    ch = gp.shape[1] // 3
    tc = 512
    ncb = ch // tc
    ts = _row_tile(seq, 512)
    tps = seq // ts
    nt = mp // ts
    pr = lambda j, i: jnp.minimum(i, nt - 1)
    in_specs = []
    for part in range(3):
        in_specs.append(pl.BlockSpec((ts, tc), lambda j, i, part=part: (pr(j, i), part * ncb + j)))
    for part in range(3):
        in_specs.append(pl.BlockSpec((ms, tc), lambda j, i, part=part: (0, part * ncb + j)))
    in_specs += [pl.BlockSpec((ms, tc), lambda j, i: (0, j)), pl.BlockSpec((SC_WIDTH, tc), lambda j, i: (0, j))]
    out_specs = [pl.BlockSpec((ts, tc), lambda j, i: (pr(j, i), j)),
                 pl.BlockSpec((ms, tc), lambda j, i: (0, j)),
                 pl.BlockSpec((1, SUBLANES, tc), lambda j, i: (pr(j, i) // tps, 0, j)),
                 pl.BlockSpec((ms, tc), lambda j, i: (0, j))]
    out_shape = [jax.ShapeDtypeStruct((mp, ch), BF16), jax.ShapeDtypeStruct((ms, ch), BF16),
                 jax.ShapeDtypeStruct((nb, SUBLANES, ch), F32), jax.ShapeDtypeStruct((ms, ch), F32)]
    yp, ys, zp_tail, zs = pl.pallas_call(
        functools.partial(_sconv_body, nt, tps, SC_WIDTH, dec_seq),
        grid=(ncb, nt + 1), in_specs=in_specs, out_specs=out_specs, out_shape=out_shape,
        scratch_shapes=[pltpu.VMEM((SUBLANES, tc), F32)],
        compiler_params=_params(2), name="short_conv",
    )(gp, gp, gp, gs, gs, gs, _pad_hist(state, dec_seq), w.astype(F32))
    hw = SC_WIDTH - 1
    new_p = zp_tail[:, SUBLANES - hw:, :]
    new_s = zs.reshape(ms // dec_seq, dec_seq, ch)[:, dec_seq - hw:, :]
    return (yp, ys), new_p, new_s


def _fox_prompt_body(tq, hps, q, k, v, fr, o, m_s, l_s, acc_s):
    qi = pl.program_id(2)
    dh = ATT_HEAD_DIM
    m_s[...] = jnp.full_like(m_s, NEG_BIG)
    l_s[...] = jnp.zeros_like(l_s)
    acc_s[...] = jnp.zeros_like(acc_s)

    def step(ki, masked):
        start = pl.multiple_of(ki * tq, tq)
        for hh in range(hps):
            cols = slice(hh * dh, (hh + 1) * dh)
            kb = k[pl.ds(start, tq), cols].astype(BF16)
            vb = v[pl.ds(start, tq), cols].astype(BF16)
            s = lax.dot_general(q[:, cols], kb, NT_DIMS, preferred_element_type=F32) - fr[0, hh, ki]
            if masked:
                r = lax.broadcasted_iota(jnp.int32, s.shape, 0)
                c = lax.broadcasted_iota(jnp.int32, s.shape, 1)
                s = jnp.where(r >= c, s, NEG_BIG)
            m_prev = m_s[hh]
            m_new = jnp.maximum(m_prev, jnp.max(s, axis=-1, keepdims=True))
            p = jnp.exp(s - jnp.tile(m_new, (1, tq // dh)))
            alpha = jnp.exp(m_prev - m_new)
            l_s[hh] = alpha * l_s[hh] + jnp.sum(p, axis=-1, keepdims=True)
            acc_s[hh] = alpha * acc_s[hh] + jnp.dot(p.astype(BF16), vb, preferred_element_type=F32)
            m_s[hh] = m_new

    def loop_body(ki, carry):
        step(ki, False)
        return carry

    lax.fori_loop(0, qi, loop_body, 0)
    step(qi, True)
    for hh in range(hps):
        o[:, hh * dh:(hh + 1) * dh] = (acc_s[hh] / l_s[hh]).astype(o.dtype)


def _fox_prompt(q, k, v, fcum, nb, seq):
    mp = q.shape[0]
    dh = ATT_HEAD_DIM
    nh = q.shape[1] // dh
    tq = _row_tile(seq, 512)
    nq = seq // tq
    frow = jnp.transpose(fcum, (0, 2, 1)).reshape(nb, nh, nq, 1, tq)
    hps = 2 if nh % 2 == 0 else 1
    gw = hps * dh
    return pl.pallas_call(
        functools.partial(_fox_prompt_body, tq, hps),
        grid=(nb, nh // hps, nq),
        in_specs=[pl.BlockSpec((tq, gw), lambda b, h, i: (b * nq + i, h)),
                  pl.BlockSpec((seq, gw), lambda b, h, i: (b, h)),
                  pl.BlockSpec((seq, gw), lambda b, h, i: (b, h)),
                  pl.BlockSpec((1, hps, nq, 1, tq), lambda b, h, i: (b, h, 0, 0, 0))],
        out_specs=pl.BlockSpec((tq, gw), lambda b, h, i: (b * nq + i, h)),
        out_shape=jax.ShapeDtypeStruct((mp, nh * dh), BF16),
        scratch_shapes=[pltpu.VMEM((hps, tq, dh), F32), pltpu.VMEM((hps, tq, dh), F32),
                        pltpu.VMEM((hps, tq, dh), F32)],
        compiler_params=_params(3), name="fox_prompt",
    )(q, k, v, frow)


def _past_bias_body(n_pages, page, nh, *refs):
    lf_refs = refs[1:1 + n_pages]
    out = refs[1 + n_pages]
    rows, later_in_page, spread_heads = refs[2 + n_pages:]
    w = page * nh
    npp = rows.shape[0]

    @pl.when(pl.program_id(0) == 0)
    def _():
        a = lax.broadcasted_iota(jnp.int32, (w, w), 0)
        c = lax.broadcasted_iota(jnp.int32, (w, w), 1)
        later_in_page[...] = (((a % nh) == (c % nh)) & ((a // nh) > (c // nh))).astype(BF16)
        k = lax.broadcasted_iota(jnp.int32, spread_heads.shape, 0)
        c2 = lax.broadcasted_iota(jnp.int32, spread_heads.shape, 1)
        spread_heads[...] = ((k % LANES) == (c2 % nh)).astype(BF16)
        rows[...] = jnp.zeros_like(rows)

    for p in range(n_pages):
        rows[p:p + 1, :] = lf_refs[p][0]
    lf = rows[...]
    within3 = jnp.dot(jnp.concatenate(_split3(lf), axis=0), later_in_page[...], preferred_element_type=F32)
    within = within3[:npp] + within3[npp:2 * npp] + within3[2 * npp:]
    first = (within + lf)[:, :LANES]
    totals = jnp.dot(jnp.concatenate(_split3(first), axis=1), spread_heads[...], preferred_element_type=F32)
    pr = lax.broadcasted_iota(jnp.int32, (npp, npp), 0)
    pc = lax.broadcasted_iota(jnp.int32, (npp, npp), 1)
    later_pages = (pc > pr).astype(BF16)
    after3 = jnp.dot(later_pages, jnp.concatenate(_split3(totals), axis=1), preferred_element_type=F32)
    after = after3[:, :w] + after3[:, w:2 * w] + after3[:, 2 * w:]
    out[0] = (within + after)[:n_pages, :]


def _past_bias(cache_lf, page_table):
    n_phys, page, nh = cache_lf.shape
    nb, n_pages = page_table.shape
    w = page * nh
    np_pad = -(-n_pages // LANES) * LANES
    lf = cache_lf.reshape(n_phys, 1, w)
    in_specs = [pl.BlockSpec((1, 1, w), lambda b, pt, p=p: (pt[b, p], 0, 0)) for p in range(n_pages)]
    grid_spec = pltpu.PrefetchScalarGridSpec(
        num_scalar_prefetch=1, grid=(nb,), in_specs=in_specs,
        out_specs=pl.BlockSpec((1, n_pages, w), lambda b, pt: (b, 0, 0)),
        scratch_shapes=[pltpu.VMEM((np_pad, w), F32), pltpu.VMEM((w, w), BF16), pltpu.VMEM((3 * LANES, w), BF16)])
    return pl.pallas_call(
        functools.partial(_past_bias_body, n_pages, page, nh),
        grid_spec=grid_spec, out_shape=jax.ShapeDtypeStruct((nb, n_pages, w), F32),
        compiler_params=_params(1), name="fox_past_bias",
    )(page_table, *([lf] * n_pages))


def _fox_sample_body(pps, n_steps, nh, dec_seq, *refs):
    qall, knew, vnew, frow_new, rb = refs[1:6]
    krefs = refs[6:6 + pps]
    vrefs = refs[6 + pps:6 + 2 * pps]
    o = refs[6 + 2 * pps]
    m_s, l_s, acc_s, s_scr, mloc_s = refs[7 + 2 * pps:]
    step = pl.program_id(1)
    qv = qall[0]
    nrow = qv.shape[0]
    cur = step % 2
    prev = 1 - cur

    @pl.when(step == 0)
    def _():
        kn = knew[0].astype(BF16)
        g = lax.dot_general(qv, kn, NT_DIMS, preferred_element_type=F32)
        r = lax.broadcasted_iota(jnp.int32, g.shape, 0)
        c = lax.broadcasted_iota(jnp.int32, g.shape, 1)
        valid = ((c % nh) == (r // dec_seq)) & ((c // nh) <= (r % dec_seq))
        s = jnp.where(valid, g - frow_new[0], NEG_BIG)
        m = jnp.max(s, axis=-1, keepdims=True)
        p = jnp.exp(s - m)
        m_s[...] = m
        l_s[...] = jnp.sum(p, axis=-1, keepdims=True)
        acc_s[...] = jnp.dot(p.astype(BF16), vnew[0].astype(BF16), preferred_element_type=F32)
        s_scr[1] = jnp.full(s_scr.shape[1:], NEG_BIG, F32)
        mloc_s[1] = jnp.full(mloc_s.shape[1:], NEG_BIG, F32)

    m_prev = m_s[...]
    m_new = jnp.maximum(m_prev, mloc_s[prev])
    alpha = jnp.exp(m_prev - m_new)
    psum = acc = None
    for i in range(pps):
        p = jnp.exp(s_scr[prev, i] - m_new)
        pv = jnp.dot(p.astype(BF16), vrefs[i][0].astype(BF16), preferred_element_type=F32)
        ps = jnp.sum(p, axis=-1, keepdims=True)
        psum = ps if psum is None else psum + ps
        acc = pv if acc is None else acc + pv
    l_new = alpha * l_s[...] + psum
    acc_new = alpha * acc_s[...] + acc
    l_s[...] = l_new
    acc_s[...] = acc_new
    m_s[...] = m_new

    lanes = krefs[0].shape[1]
    r = lax.broadcasted_iota(jnp.int32, (nrow, lanes), 0)
    c = lax.broadcasted_iota(jnp.int32, (nrow, lanes), 1)
    head_ok = (c % nh) == (r // dec_seq)
    mloc = None
    for i in range(pps):
        g = lax.dot_general(qv, krefs[i][0].astype(BF16), NT_DIMS, preferred_element_type=F32)
        s = jnp.where(head_ok, g + rb[0, i], NEG_BIG)
        s_scr[cur, i] = s
        mi = jnp.max(s, axis=-1, keepdims=True)
        mloc = mi if mloc is None else jnp.maximum(mloc, mi)
    mloc_s[cur] = mloc

    @pl.when(step == n_steps)
    def _():
        o[0] = (acc_new / l_new).astype(o.dtype)


def _fox_sample(q, k, v, fn, cache_k, cache_v, cache_lf, page_table, dec_seq):
    ms = q.shape[0]
    nb = ms // dec_seq
    nh, dh = ATT_HEADS, ATT_HEAD_DIM
    n_phys, page = cache_k.shape[0], cache_k.shape[1]
    n_pages = page_table.shape[1]
    pps = 8
    while n_pages % pps:
        pps //= 2
    n_steps = n_pages // pps
    nrow = nh * dec_seq
    rbias = _past_bias(cache_lf, page_table).reshape(nb, n_pages, 1, page * nh)
    qall = jnp.transpose(q.reshape(nb, dec_seq, nh, dh), (0, 2, 1, 3)).reshape(nb, nrow, dh)
    knew = k.reshape(nb, dec_seq * nh, dh)
    vnew = v.reshape(nb, dec_seq * nh, dh)
    frow_new = fn.reshape(nb, 1, dec_seq * nh)
    ck = cache_k.reshape(n_phys, page * nh, dh)
    cv = cache_v.reshape(n_phys, page * nh, dh)
    per_b = lambda shape: pl.BlockSpec((1,) + shape, lambda b, s, pt: (b, 0, 0))
    in_specs = [per_b((nrow, dh)), per_b((dec_seq * nh, dh)), per_b((dec_seq * nh, dh)),
                per_b((1, dec_seq * nh)),
                pl.BlockSpec((1, pps, 1, page * nh), lambda b, s, pt: (b, jnp.minimum(s, n_steps - 1), 0, 0))]
    k_group = lambda s: jnp.minimum(s, n_steps - 1)
    v_group = lambda s: jnp.maximum(s - 1, 0)
    for group in (k_group, v_group):
        in_specs += [pl.BlockSpec((1, page * nh, dh),
                                  lambda b, s, pt, i=i, group=group: (pt[b, group(s) * pps + i], 0, 0))
                     for i in range(pps)]
    grid_spec = pltpu.PrefetchScalarGridSpec(
        num_scalar_prefetch=1, grid=(nb, n_steps + 1), in_specs=in_specs,
        out_specs=pl.BlockSpec((1, nrow, dh), lambda b, s, pt: (b, 0, 0)),
        scratch_shapes=[pltpu.VMEM((nrow, 1), F32), pltpu.VMEM((nrow, 1), F32), pltpu.VMEM((nrow, dh), F32),
                        pltpu.VMEM((2, pps, nrow, page * nh), F32), pltpu.VMEM((2, nrow, 1), F32)])
    o = pl.pallas_call(
        functools.partial(_fox_sample_body, pps, n_steps, nh, dec_seq),
        grid_spec=grid_spec, out_shape=jax.ShapeDtypeStruct((nb, nrow, dh), BF16),
        compiler_params=_params(2), name="fox_sample",
    )(page_table, qall, knew, vnew, frow_new, rbias, *([ck] * pps), *([cv] * pps))
    return jnp.transpose(o.reshape(nb, nh, dec_seq, dh), (0, 2, 1, 3)).reshape(ms, nh * dh)


def _ssd_body(t_in, nseq, carried, xs_r, z_r, b_r, c_r, dt_r, gc_r, sel_r, ex_r, *rest):
    if carried:
        hx_r, hb_r, hc_r, s0_r, y_o, st_o, st, carry = rest
    else:
        hx_r = hb_r = hc_r = s0_r = None
        y_o, st_o, st, carry = rest
    ci = pl.program_id(2)
    nc = pl.num_programs(2)
    L = t_in * nseq
    P, N = SSM_HEAD_DIM, D_STATE
    gw = xs_r.shape[1]
    R = gw // P

    @pl.when(ci == 0)
    def _():
        if carried:
            for s in range(nseq):
                st[:, s * N:(s + 1) * N] = s0_r[s, 0]
        else:
            st[...] = jnp.zeros_like(st)
            carry[...] = jnp.zeros_like(carry)

    d_skip = gc_r[SSM_CONV + 1:SSM_CONV + 2, 0:gw]
    a_rate = gc_r[SSM_CONV + 1:SSM_CONV + 2, gw:gw + LANES]
    gain = gc_r[SSM_CONV + 2:SSM_CONV + 3, 0:gw]

    def conv_act(raw_r, hist_r, lo, hi):
        x = raw_r[...]
        w_r = gc_r[0:SSM_CONV, lo:hi]
        bias = gc_r[SSM_CONV:SSM_CONV + 1, lo:hi]
        if carried:
            t = lax.broadcasted_iota(jnp.int32, x.shape, 0) % t_in
            y = _conv_rows(x, hist_r[...], w_r, SSM_CONV, t)
        else:
            y = _conv_rows(x, None, w_r, SSM_CONV, None)
            head = x[:SUBLANES, :]
            t8 = lax.broadcasted_iota(jnp.int32, head.shape, 0)
            y8 = _conv_rows(head, carry[:, lo:hi], w_r, SSM_CONV, t8)
            y = jnp.concatenate([y8, y[SUBLANES:, :]], axis=0)
            carry[:, lo:hi] = pltpu.roll(x[L - SUBLANES:, :], SSM_CONV - 1, axis=0)
        return _silu(y + bias)

    xs = conv_act(xs_r, hx_r, 0, gw)
    bmf = conv_act(b_r, hb_r, gw, gw + N)
    cmf = conv_act(c_r, hc_r, gw + N, gw + 2 * N)
    bm, cm = bmf.astype(BF16), cmf.astype(BF16)
    dt = jnp.dot(jnp.concatenate(_split3(dt_r[...]), axis=1), sel_r[...], preferred_element_type=F32)
    r = lax.broadcasted_iota(jnp.int32, (L, L), 0)
    c = lax.broadcasted_iota(jnp.int32, (L, L), 1)
    da = dt * (a_rate * LOG2E)
    if nseq == 1:
        tri = r >= c
        acum = jnp.dot(tri.astype(F32), da, preferred_element_type=F32, precision=HIGHEST)
        a_tot = acum[L - 1:L, :]
    else:
        same = (r // t_in) == (c // t_in)
        tri = same & (r >= c)
        sums = jnp.dot(jnp.concatenate([tri.astype(F32), same.astype(F32)], axis=0), da,
                       preferred_element_type=F32, precision=HIGHEST)
        acum, a_tot = sums[:L], sums[L:]
    acum_t = acum.T
    dt_t = dt.T.astype(BF16)
    cols = jnp.concatenate([dt * jnp.exp2(a_tot - acum), jnp.exp2(acum)], axis=0)
    wide = jnp.dot(jnp.concatenate(_split3(cols), axis=1), ex_r[...], preferred_element_type=F32)
    w_state, w_off = wide[:L], wide[L:]
    cb = lax.dot_general(cm, bm, NT_DIMS, preferred_element_type=F32).astype(BF16)
    if nseq == 1:
        cblk, bblk = cm, bm
        decay_t = acum_t
    else:
        seq_of_row = lax.broadcasted_iota(jnp.int32, (L, N), 0) // t_in
        cblk = jnp.concatenate([jnp.where(seq_of_row == s, cmf, 0.0) for s in range(nseq)], axis=1).astype(BF16)
        bblk = jnp.concatenate([jnp.where(seq_of_row == s, bmf, 0.0) for s in range(nseq)], axis=1).astype(BF16)
        decay_t = a_tot.T
    s_prev = st[...]
    y_off = lax.dot_general(cblk, s_prev.astype(BF16), NT_DIMS, preferred_element_type=F32) * w_off
    xb = xs.astype(BF16)
    ys = []
    for h in range(R):
        decay = jnp.exp2(jnp.where(tri, acum[:, h:h + 1] - acum_t[h:h + 1, :], -jnp.inf))
        m = cb * decay.astype(BF16) * dt_t[h:h + 1, :]
        ys.append(jnp.dot(m, xb[:, h * P:(h + 1) * P], preferred_element_type=F32))
    new_states = lax.dot_general((xs * w_state).astype(BF16), bblk, TN_DIMS, preferred_element_type=F32)
    for s in range(nseq):
        last = (s + 1) * t_in - 1
        for h in range(R):
            blk = (slice(h * P, (h + 1) * P), slice(s * N, (s + 1) * N))
            st[blk] = s_prev[blk] * jnp.exp2(decay_t[h:h + 1, last:last + 1]) + new_states[blk]
    y = jnp.concatenate(ys, axis=1) + y_off + d_skip * xs
    y = y * _silu(z_r[...])
    y = y * lax.rsqrt(jnp.mean(y * y, axis=-1, keepdims=True) + EPS)
    y_o[...] = (y * gain).astype(y_o.dtype)

    @pl.when(ci == nc - 1)
    def _():
        for s in range(nseq):
            st_o[s, 0] = st[:, s * N:(s + 1) * N]


def _ssd(xbc, z, dt, w_conv, b_conv, a_pad, d_exp, gn_w, hist, init_state, nb, seq):
    G, N, P = SSM_GROUPS, D_STATE, SSM_HEAD_DIM
    inner = z.shape[1]
    gw = inner // G
    rg = gw // P
    L = SSM_CHUNK
    carried = init_state is not None
    if carried:
        t_in, nc = seq, 1
        assert L % seq == 0 and nb % (L // seq) == 0, (nb, seq)
    else:
        t_in, nc = L, seq // L
        assert seq % L == 0, seq
    nseq = L // t_in
    nbb = nb // nseq
    nbc = inner // N
    row = lambda b, g, c: b * nc + c

    def triple():
        return [pl.BlockSpec((L, gw), lambda b, g, c: (row(b, g, c), g)),
                pl.BlockSpec((L, N), lambda b, g, c: (row(b, g, c), nbc + g)),
                pl.BlockSpec((L, N), lambda b, g, c: (row(b, g, c), nbc + G + g))]

    cw = gw + 2 * N

    def by_group(v):
        parts = [v[:, :inner].reshape(-1, G, gw), v[:, inner:inner + G * N].reshape(-1, G, N),
                 v[:, inner + G * N:].reshape(-1, G, N)]
        return jnp.transpose(jnp.concatenate(parts, axis=2), (1, 0, 2))

    zpad = lambda n: jnp.zeros((G, 1, n), F32)
    gconst = jnp.concatenate([
        by_group(w_conv.astype(F32)), by_group(b_conv.reshape(1, -1).astype(F32)),
        jnp.concatenate([d_exp.reshape(G, 1, gw), a_pad.reshape(G, 1, LANES), zpad(cw - gw - LANES)], axis=2),
        jnp.concatenate([gn_w.reshape(G, 1, gw), zpad(cw - gw)], axis=2),
        jnp.zeros((G, SUBLANES - SSM_CONV - 3, cw), F32)], axis=1)
    x_spec, b_spec, c_spec = triple()
    in_specs = [x_spec, pl.BlockSpec((L, gw), lambda b, g, c: (row(b, g, c), g)), b_spec, c_spec,
                pl.BlockSpec((L, LANES), lambda b, g, c: (row(b, g, c), 0)),
                pl.BlockSpec((None, SUBLANES, cw), lambda b, g, c: (g, 0, 0)),
                pl.BlockSpec((None, 3 * LANES, LANES), lambda b, g, c: (g, 0, 0)),
                pl.BlockSpec((3 * LANES, gw), lambda b, g, c: (0, 0))]
    lane_head = jnp.arange(3 * LANES) % LANES
    widen = (lane_head[:, None] == jnp.arange(gw)[None, :] // P).astype(BF16)
    lane = jnp.arange(LANES)
    pick = ((lane_head[None, :, None] == jnp.arange(G)[:, None, None] * rg + lane[None, None, :])
            & (lane[None, None, :] < rg)).astype(BF16)
    args = [xbc, z, xbc, xbc, dt, gconst, pick, widen]
    if carried:
        in_specs += triple() + [pl.BlockSpec((nseq, 1, gw, N), lambda b, g, c: (b, g, 0, 0))]
        args += [hist, hist, hist, init_state]
    y, st = pl.pallas_call(
        functools.partial(_ssd_body, t_in, nseq, carried),
        grid=(nbb, G, nc), in_specs=in_specs,
        out_specs=[pl.BlockSpec((L, gw), lambda b, g, c: (row(b, g, c), g)),
                   pl.BlockSpec((nseq, 1, gw, N), lambda b, g, c: (b, g, 0, 0))],
        out_shape=[jax.ShapeDtypeStruct((nb * seq, inner), BF16), jax.ShapeDtypeStruct((nb, G, gw, N), F32)],
        scratch_shapes=[pltpu.VMEM((gw, nseq * N), F32), pltpu.VMEM((SUBLANES, gw + 2 * N), F32)],
        compiler_params=_params(3), name="ssd",
    )(*args)
    return y, st


def _ffn(x, norm_w, w_gate, w_up, w_down, layer):
    h = _rmsnorm_pair(x, norm_w, BF16)
    a = _mm_pair([h], [_W(w_gate, layer), _W(w_up, layer)], out_dtype=BF16, mode="swiglu", name="ffn_gate_up")
    return _mm_pair([a], [_W(w_down, layer)], res=x, name="ffn_down")


def kernel(x_prompt, x_sample, cache_k, cache_v, cache_logf, page_table, state_sconv, state_mconv, state_ssm, norm_mix, norm_ffn, norm_final, w_in_even, b_forget, w_sconv, w_out_even, w_in_ssm, w_mconv, b_mconv, dt_bias, a_log, d_skip, w_gnorm, w_out_ssm, w_gate, w_up, w_down):
    bp, seq, d = x_prompt.shape
    bs, dec_seq, _ = x_sample.shape
    mp, ms = bp * seq, bs * dec_seq
    nh, dh = ATT_HEADS, ATT_HEAD_DIM
    att = nh * dh
    sc_ch = state_sconv.shape[-1]
    x = (x_prompt.reshape(mp, d), x_sample.reshape(ms, d))

    i = 0
    h = _rmsnorm_pair(x, norm_mix[0], BF16)
    c0 = 3 * sc_ch
    w_in_t = jnp.swapaxes(w_in_even, 1, 2)
    win = lambda col0, n: _W(w_in_t, i, col0=col0, n=n, transposed=True)
    gcu = _mm_pair([h], [win(0, c0)], name="in_even_conv")
    q = _mm_pair([h], [win(c0, att)], out_dtype=BF16, scale=dh ** -0.5, name="in_even_q")
    k = _mm_pair([h], [win(c0 + att, att)], name="in_even_k")
    v = _mm_pair([h], [win(c0 + 2 * att, att)], name="in_even_v")
    lf_p, fc_p, lf_s, fc_s = _forget_pair(h, w_in_even[i, :, c0 + 3 * att:], b_forget[i], bp, seq, dec_seq)
    logf_p = lf_p[:, :nh].reshape(bp, seq, nh)
    logf_s = lf_s[:, :nh].reshape(bs, dec_seq, nh)
    y_a, sconv_p, sconv_s = _sconv_pair(gcu, state_sconv[i], w_sconv[i], bp, seq, dec_seq)
    yb_p = _fox_prompt(q[0], k[0], v[0], fc_p[:, :nh].reshape(bp, seq, nh), bp, seq)
    yb_s = _fox_sample(q[1], k[1], v[1], fc_s[:, :nh].reshape(bs, dec_seq, nh),
                       cache_k[i], cache_v[i], cache_logf[i], page_table, dec_seq)
    x = _mm_pair([y_a, (yb_p, yb_s)],
                 [_W(w_out_even, i, row0=0, k=sc_ch), _W(w_out_even, i, row0=sc_ch, k=att)], res=x, name="out_even")
    x = _ffn(x, norm_ffn[0], w_gate, w_up, w_down, 0)

    j = 0
    G = SSM_GROUPS
    n_heads = a_log.shape[1]
    rg = n_heads // G
    inner = n_heads * SSM_HEAD_DIM
    conv_dim = state_mconv.shape[-1]
    h = _rmsnorm_pair(x, norm_mix[1], BF16)
    w_ssm_t = jnp.swapaxes(w_in_ssm, 1, 2)
    z = _mm_pair([h], [_W(w_ssm_t, j, n=inner, transposed=True)], name="in_ssm_z")
    xbc = _mm_pair([h], [_W(w_ssm_t, j, col0=inner, n=conv_dim, transposed=True)], name="in_ssm_xbc")
    assert n_heads <= LANES
    w_dt = jnp.zeros((d, LANES), BF16).at[:, :n_heads].set(w_in_ssm[j, :, inner + conv_dim:].astype(BF16))
    b_dt = jnp.zeros((1, LANES), F32).at[0, :n_heads].set(dt_bias[j].astype(F32))
    dt = _mm_pair([h], [_W(w_dt)], mode="softplus", bias=b_dt, name="in_ssm_dt")
    a_pad = jnp.zeros((G, LANES), F32).at[:, :rg].set(-jnp.exp(a_log[j].astype(F32)).reshape(G, rg)).reshape(1, G * LANES)
    d_exp = jnp.repeat(d_skip[j].astype(F32), SSM_HEAD_DIM).reshape(1, inner)
    gn_w = w_gnorm[j].astype(F32).reshape(1, inner)
    hw = SSM_CONV - 1
    mconv_p = xbc[0].reshape(bp, seq, conv_dim)[:, seq - hw:, :]
    mconv_s = xbc[1].reshape(bs, dec_seq, conv_dim)[:, dec_seq - hw:, :]
    yp, ssm_p = _ssd(xbc[0], z[0], dt[0], w_mconv[j], b_mconv[j], a_pad, d_exp, gn_w, None, None, bp, seq)
    init = state_ssm[j].astype(F32).reshape(bs, G, rg * SSM_HEAD_DIM, D_STATE)
    ysm, ssm_s = _ssd(xbc[1], z[1], dt[1], w_mconv[j], b_mconv[j], a_pad, d_exp, gn_w,
                      _pad_hist(state_mconv[j], dec_seq), init, bs, dec_seq)
    x = _mm_pair([(yp, ysm)], [_W(w_out_ssm, j)], res=x, name="out_ssm")
    x = _ffn(x, norm_ffn[1], w_gate, w_up, w_down, 1)

    y = _rmsnorm_pair(x, norm_final, F32)
    st_shape = (n_heads, SSM_HEAD_DIM, D_STATE)
    return (y[0].reshape(bp, seq, d), y[1].reshape(bs, dec_seq, d),
            k[0].reshape(1, bp, seq, nh, dh), v[0].reshape(1, bp, seq, nh, dh), logf_p[None],
            sconv_p[None], mconv_p[None], ssm_p.reshape((1, bp) + st_shape),
            k[1].reshape(1, bs, dec_seq, nh, dh), v[1].reshape(1, bs, dec_seq, nh, dh), logf_s[None],
            sconv_s[None], mconv_s[None], ssm_s.reshape((1, bs) + st_shape))
```

```python
import functools

import jax
import jax.numpy as jnp
from jax import lax
from jax.experimental import pallas as pl
from jax.experimental.pallas import tpu as pltpu

F32 = jnp.float32
BF16 = jnp.bfloat16

SC_WIDTH = 3
ATT_HEADS = 8
ATT_HEAD_DIM = 128
SSM_HEAD_DIM = 64
SSM_GROUPS = 8
D_STATE = 128
SSM_CONV = 4
SSM_CHUNK = 128
EPS = 1e-6

LANES = 128
SUBLANES = 8
VMEM_LIMIT_BYTES = 56 * 1024 * 1024
NEG_BIG = -1e30
LOG2E = 1.4426950408889634

HIGHEST = lax.Precision.HIGHEST
NT_DIMS = (((1,), (1,)), ((), ()))
TN_DIMS = (((0,), (0,)), ((), ()))


def _params(n_grid_dims):
    return pltpu.CompilerParams(dimension_semantics=("arbitrary",) * n_grid_dims,
                                vmem_limit_bytes=VMEM_LIMIT_BYTES)


def _row_tile(mp, pref):
    t = pref
    while mp % t:
        t //= 2
    assert t >= SUBLANES, (mp, pref)
    return t


def _rmsnorm_body(nt, xp, xs, w, op, os):
    i = pl.program_id(0)

    def norm(x):
        y = x * lax.rsqrt(jnp.mean(x * x, axis=-1, keepdims=True) + EPS)
        return (y * w[...]).astype(op.dtype)

    @pl.when(i < nt)
    def _():
        op[...] = norm(xp[...])

    @pl.when(i == nt)
    def _():
        os[...] = norm(xs[...])


def _rmsnorm_pair(x, w, out_dtype):
    xp, xs = x
    mp, d = xp.shape
    ms = xs.shape[0]
    tm = _row_tile(mp, 512)
    nt = mp // tm
    pidx = lambda i: (jnp.minimum(i, nt - 1), 0)
    return pl.pallas_call(
        functools.partial(_rmsnorm_body, nt),
        grid=(nt + 1,),
        in_specs=[pl.BlockSpec((tm, d), pidx), pl.BlockSpec((ms, d), lambda i: (0, 0)),
                  pl.BlockSpec((1, d), lambda i: (0, 0))],
        out_specs=[pl.BlockSpec((tm, d), pidx), pl.BlockSpec((ms, d), lambda i: (0, 0))],
        out_shape=[jax.ShapeDtypeStruct((mp, d), out_dtype), jax.ShapeDtypeStruct((ms, d), out_dtype)],
        compiler_params=_params(1), name="rmsnorm",
    )(xp, xs, w.reshape(1, d).astype(F32))


def _softplus(x):
    return jnp.maximum(x, 0.0) + jnp.log1p(jnp.exp(-jnp.abs(x)))


def _silu(x):
    return x * (1.0 / (1.0 + jnp.exp(-x)))


def _split3(v):
    hi = v.astype(BF16)
    r1 = v - hi.astype(F32)
    mid = r1.astype(BF16)
    lo = (r1 - mid.astype(F32)).astype(BF16)
    return hi, mid, lo


class _W:
    def __init__(self, arr, layer=None, row0=0, k=None, col0=0, n=None, transposed=False):
        self.arr, self.layer, self.row0, self.col0, self.transposed = arr, layer, row0, col0, transposed
        kdim, ndim = (-1, -2) if transposed else (-2, -1)
        self.k = arr.shape[kdim] if k is None else k
        self.n = arr.shape[ndim] if n is None else n

    def block(self, tn):
        return (tn, self.k) if self.transposed else (self.k, tn)

    def spec(self, tn):
        assert self.row0 % self.k == 0 and self.col0 % tn == 0
        rb, cb = self.row0 // self.k, self.col0 // tn
        if self.transposed:
            idx = lambda j, i: (cb + j, rb)
        else:
            idx = lambda j, i: (rb, cb + j)
        if self.layer is None:
            return pl.BlockSpec(self.block(tn), idx)
        layer = self.layer
        return pl.BlockSpec((None,) + self.block(tn), lambda j, i: (layer,) + idx(j, i))


def _mm_body(nt, n_lhs, mode, has_res, has_bias, scale, w_is_f32, w_transposed, *refs):
    i = pl.program_id(1)

    def mm(x, idx):
        dims = NT_DIMS if w_transposed[idx] else (((1,), (0,)), ((), ()))
        return lax.dot_general(x, wrefs[idx][...], dims, preferred_element_type=F32)

    xrefs = refs[:2 * n_lhs]
    n_w = len(w_is_f32)
    wrefs = list(refs[2 * n_lhs:2 * n_lhs + n_w])
    pos = 2 * n_lhs + n_w
    rp = rs = bias = None
    if has_res:
        rp, rs = refs[pos], refs[pos + 1]
        pos += 2
    if has_bias:
        bias = refs[pos]
        pos += 1
    op, os = refs[pos], refs[pos + 1]
    scratch = list(refs[pos + 2:])

    for idx in range(n_w):
        if w_is_f32[idx]:
            wsrc, wdst = wrefs[idx], scratch.pop(0)

            @pl.when(i == 0)
            def _(wsrc=wsrc, wdst=wdst):
                wdst[...] = wsrc[...].astype(BF16)

            wrefs[idx] = wdst

    def compute(which, out, res):
        if mode == "swiglu":
            x = xrefs[which][...]
            acc = _silu(mm(x, 0)) * mm(x, 1)
        else:
            acc = mm(xrefs[which][...], 0)
            for k in range(1, n_lhs):
                acc += mm(xrefs[2 * k + which][...], k)
        if scale is not None:
            acc = acc * scale
        if has_bias:
            acc = acc + bias[...]
        if mode == "softplus":
            acc = _softplus(acc)
        if has_res:
            acc = res[...] + acc
        out[...] = acc.astype(out.dtype)

    @pl.when(i == 0)
    def _():
        compute(1, os, rs)

    @pl.when(i > 0)
    def _():
        compute(0, op, rp)


def _mm_pair(xs_list, ws_list, *, out_dtype=F32, mode="plain", res=None, bias=None, scale=None, name="matmul"):
    n_lhs = len(xs_list)
    mp = xs_list[0][0].shape[0]
    ms = xs_list[0][1].shape[0]
    n = ws_list[0].n
    kmax = max(w.k for w in ws_list)
    tm = _row_tile(mp, 2048 if mode == "swiglu" else 1024 if kmax <= 4096 else 512)
    nt = mp // tm
    if n % 1024 == 0 and kmax <= 2048 and mode != "swiglu":
        tn = 1024
    elif n % 512 == 0:
        tn = 512
    else:
        assert n % LANES == 0 and n <= 512, n
        tn = n
    pidx_x = lambda j, i: (jnp.maximum(i - 1, 0), 0)
    pidx_o = lambda j, i: (jnp.maximum(i - 1, 0), j)
    in_specs, args = [], []
    for xp, xs in xs_list:
        kk = xp.shape[1]
        in_specs += [pl.BlockSpec((tm, kk), pidx_x), pl.BlockSpec((ms, kk), lambda j, i: (0, 0))]
        args += [xp, xs]
    w_is_f32, scratch = [], []
    for w in ws_list:
        in_specs.append(w.spec(tn))
        args.append(w.arr)
        w_is_f32.append(w.arr.dtype != BF16)
        if w_is_f32[-1]:
            scratch.append(pltpu.VMEM(w.block(tn), BF16))
    if res is not None:
        in_specs += [pl.BlockSpec((tm, tn), pidx_o), pl.BlockSpec((ms, tn), lambda j, i: (0, j))]
        args += [res[0], res[1]]
    if bias is not None:
        in_specs.append(pl.BlockSpec((1, tn), lambda j, i: (0, j)))
        args.append(bias.reshape(1, n).astype(F32))
    outs = pl.pallas_call(
        functools.partial(_mm_body, nt, n_lhs, mode, res is not None, bias is not None, scale, tuple(w_is_f32),
                          tuple(w.transposed for w in ws_list)),
        grid=(n // tn, nt + 1),
        in_specs=in_specs,
        out_specs=[pl.BlockSpec((tm, tn), pidx_o), pl.BlockSpec((ms, tn), lambda j, i: (0, j))],
        out_shape=[jax.ShapeDtypeStruct((mp, n), out_dtype), jax.ShapeDtypeStruct((ms, n), out_dtype)],
        scratch_shapes=scratch,
        compiler_params=_params(2), name=name,
    )(*args)
    return outs[0], outs[1]


def _log_sigmoid(x):
    return -_softplus(-x)


def _forget_body(nb, seq, dec_seq, ck, xp, xs, w, b, lp, fp, ls, fs):
    i = pl.program_id(0)

    @pl.when(i < nb)
    def _():
        r = lax.broadcasted_iota(jnp.int32, (ck, ck), 0)
        c = lax.broadcasted_iota(jnp.int32, (ck, ck), 1)
        tri = (r >= c).astype(F32)
        carry = jnp.zeros((1, LANES), F32)
        for t in range(seq // ck):
            f = jnp.dot(xp[t * ck:(t + 1) * ck, :], w[...], preferred_element_type=F32) + b[...]
            lf = _log_sigmoid(f)
            lp[t * ck:(t + 1) * ck, :] = lf
            cs = jnp.dot(tri, lf, preferred_element_type=F32, precision=HIGHEST) + carry
            fp[t * ck:(t + 1) * ck, :] = cs
            carry = cs[ck - 1:ck, :]

    @pl.when(i == nb)
    def _():
        ms = xs.shape[0]
        r = lax.broadcasted_iota(jnp.int32, (ms, ms), 0)
        c = lax.broadcasted_iota(jnp.int32, (ms, ms), 1)
        tri = ((r // dec_seq == c // dec_seq) & (r >= c)).astype(F32)
        f = jnp.dot(xs[...], w[...], preferred_element_type=F32) + b[...]
        lf = _log_sigmoid(f)
        ls[...] = lf
        fs[...] = jnp.dot(tri, lf, preferred_element_type=F32, precision=HIGHEST)


def _forget_pair(h, w_f, b_f, nb, seq, dec_seq):
    hp, hs = h
    mp, d = hp.shape
    ms = hs.shape[0]
    assert ms <= 512 and ms % SUBLANES == 0
    ck = _row_tile(seq, 256)
    n_h = w_f.shape[1]
    wpad = jnp.zeros((d, LANES), BF16).at[:, :n_h].set(w_f.astype(BF16))
    bpad = jnp.zeros((1, LANES), F32).at[0, :n_h].set(b_f.astype(F32))
    pidx = lambda i: (jnp.minimum(i, nb - 1), 0)
    cidx = lambda i: (0, 0)
    return pl.pallas_call(
        functools.partial(_forget_body, nb, seq, dec_seq, ck),
        grid=(nb + 1,),
        in_specs=[pl.BlockSpec((seq, d), pidx), pl.BlockSpec((ms, d), cidx),
                  pl.BlockSpec((d, LANES), cidx), pl.BlockSpec((1, LANES), cidx)],
        out_specs=[pl.BlockSpec((seq, LANES), pidx), pl.BlockSpec((seq, LANES), pidx),
                   pl.BlockSpec((ms, LANES), cidx), pl.BlockSpec((ms, LANES), cidx)],
        out_shape=[jax.ShapeDtypeStruct((mp, LANES), F32), jax.ShapeDtypeStruct((mp, LANES), F32),
                   jax.ShapeDtypeStruct((ms, LANES), F32), jax.ShapeDtypeStruct((ms, LANES), F32)],
        compiler_params=_params(1), name="forget_gates",
    )(hp, hs, wpad, bpad)


def _shifted(x, hist, k, width, t_in_seq):
    n = x.shape[0]
    xk = pltpu.roll(x, k, axis=0)
    if hist is None:
        return xk
    back = (width - 1 - k) % n
    hk = hist if back == 0 else pltpu.roll(hist, n - back, axis=0)
    return jnp.where(t_in_seq >= k, xk, hk)


def _conv_rows(x, hist, wref, width, t_in_seq):
    acc = x * wref[width - 1:width, :]
    for k in range(1, width):
        acc += _shifted(x, hist, k, width, t_in_seq) * wref[width - 1 - k:width - k, :]
    return acc


def _sconv_body(nt, tiles_per_seq, width, dec_seq, gbp, gcp, up, gbs, gcs, us, hs, w, yp, ys, zp_tail, zs, carry):
    i = pl.program_id(1)

    @pl.when(i < nt)
    def _():
        t0 = i % tiles_per_seq
        x = gcp[...] * up[...]
        ts = x.shape[0]

        @pl.when(t0 == 0)
        def _():
            carry[...] = jnp.zeros_like(carry)

        yp[...] = (gbp[...] * _conv_rows(x, None, w, width, None)).astype(yp.dtype)
        head = x[:SUBLANES, :]
        t8 = lax.broadcasted_iota(jnp.int32, head.shape, 0)
        yp[:SUBLANES, :] = (gbp[:SUBLANES, :] * _conv_rows(head, carry[...], w, width, t8)).astype(yp.dtype)
        tail = x[ts - SUBLANES:, :]
        carry[...] = pltpu.roll(tail, width - 1, axis=0)
        zp_tail[0] = tail

    @pl.when(i == nt)
    def _():
        x = gcs[...] * us[...]
        t = lax.broadcasted_iota(jnp.int32, x.shape, 0) % dec_seq
        ys[...] = (gbs[...] * _conv_rows(x, hs[...], w, width, t)).astype(ys.dtype)
        zs[...] = x


def _pad_hist(state, dec_seq):
    b, hw, c = state.shape
    assert dec_seq >= hw
    return jnp.concatenate([state.astype(F32), jnp.zeros((b, dec_seq - hw, c), F32)], axis=1).reshape(b * dec_seq, c)


def _sconv_pair(gcu, state, w, nb, seq, dec_seq):
    gp, gs = gcu
    mp = gp.shape[0]
    ms = gs.shape[0]
    ch = gp.shape[1] // 3
    tc = 512
    ncb = ch // tc
    ts = _row_tile(seq, 512)
    tps = seq // ts
    nt = mp // ts
    pr = lambda j, i: jnp.minimum(i, nt - 1)
    in_specs = []
    for part in range(3):
        in_specs.append(pl.BlockSpec((ts, tc), lambda j, i, part=part: (pr(j, i), part * ncb + j)))
    for part in range(3):
        in_specs.append(pl.BlockSpec((ms, tc), lambda j, i, part=part: (0, part * ncb + j)))
    in_specs += [pl.BlockSpec((ms, tc), lambda j, i: (0, j)), pl.BlockSpec((SC_WIDTH, tc), lambda j, i: (0, j))]
    out_specs = [pl.BlockSpec((ts, tc), lambda j, i: (pr(j, i), j)),
                 pl.BlockSpec((ms, tc), lambda j, i: (0, j)),
                 pl.BlockSpec((1, SUBLANES, tc), lambda j, i: (pr(j, i) // tps, 0, j)),
                 pl.BlockSpec((ms, tc), lambda j, i: (0, j))]
    out_shape = [jax.ShapeDtypeStruct((mp, ch), BF16), jax.ShapeDtypeStruct((ms, ch), BF16),
                 jax.ShapeDtypeStruct((nb, SUBLANES, ch), F32), jax.ShapeDtypeStruct((ms, ch), F32)]
    yp, ys, zp_tail, zs = pl.pallas_call(
        functools.partial(_sconv_body, nt, tps, SC_WIDTH, dec_seq),
        grid=(ncb, nt + 1), in_specs=in_specs, out_specs=out_specs, out_shape=out_shape,
        scratch_shapes=[pltpu.VMEM((SUBLANES, tc), F32)],
        compiler_params=_params(2), name="short_conv",
    )(gp, gp, gp, gs, gs, gs, _pad_hist(state, dec_seq), w.astype(F32))
    hw = SC_WIDTH - 1
    new_p = zp_tail[:, SUBLANES - hw:, :]
    new_s = zs.reshape(ms // dec_seq, dec_seq, ch)[:, dec_seq - hw:, :]
    return (yp, ys), new_p, new_s


def _fox_prompt_body(tq, hps, q, k, v, fr, o, m_s, l_s, acc_s):
    qi = pl.program_id(2)
    dh = ATT_HEAD_DIM
    m_s[...] = jnp.full_like(m_s, NEG_BIG)
    l_s[...] = jnp.zeros_like(l_s)
    acc_s[...] = jnp.zeros_like(acc_s)

    def step(ki, masked):
        start = pl.multiple_of(ki * tq, tq)
        for hh in range(hps):
            cols = slice(hh * dh, (hh + 1) * dh)
            kb = k[pl.ds(start, tq), cols].astype(BF16)
            vb = v[pl.ds(start, tq), cols].astype(BF16)
            s = lax.dot_general(q[:, cols], kb, NT_DIMS, preferred_element_type=F32) - fr[0, hh, ki]
            if masked:
                r = lax.broadcasted_iota(jnp.int32, s.shape, 0)
                c = lax.broadcasted_iota(jnp.int32, s.shape, 1)
                s = jnp.where(r >= c, s, NEG_BIG)
            m_prev = m_s[hh]
            m_new = jnp.maximum(m_prev, jnp.max(s, axis=-1, keepdims=True))
            p = jnp.exp(s - jnp.tile(m_new, (1, tq // dh)))
            alpha = jnp.exp(m_prev - m_new)
            l_s[hh] = alpha * l_s[hh] + jnp.sum(p, axis=-1, keepdims=True)
            acc_s[hh] = alpha * acc_s[hh] + jnp.dot(p.astype(BF16), vb, preferred_element_type=F32)
            m_s[hh] = m_new

    def loop_body(ki, carry):
        step(ki, False)
        return carry

    lax.fori_loop(0, qi, loop_body, 0)
    step(qi, True)
    for hh in range(hps):
        o[:, hh * dh:(hh + 1) * dh] = (acc_s[hh] / l_s[hh]).astype(o.dtype)


def _fox_prompt(q, k, v, fcum, nb, seq):
    mp = q.shape[0]
    dh = ATT_HEAD_DIM
    nh = q.shape[1] // dh
    tq = _row_tile(seq, 512)
    nq = seq // tq
    frow = jnp.transpose(fcum, (0, 2, 1)).reshape(nb, nh, nq, 1, tq)
    hps = 2 if nh % 2 == 0 else 1
    gw = hps * dh
    return pl.pallas_call(
        functools.partial(_fox_prompt_body, tq, hps),
        grid=(nb, nh // hps, nq),
        in_specs=[pl.BlockSpec((tq, gw), lambda b, h, i: (b * nq + i, h)),
                  pl.BlockSpec((seq, gw), lambda b, h, i: (b, h)),
                  pl.BlockSpec((seq, gw), lambda b, h, i: (b, h)),
                  pl.BlockSpec((1, hps, nq, 1, tq), lambda b, h, i: (b, h, 0, 0, 0))],
        out_specs=pl.BlockSpec((tq, gw), lambda b, h, i: (b * nq + i, h)),
        out_shape=jax.ShapeDtypeStruct((mp, nh * dh), BF16),
        scratch_shapes=[pltpu.VMEM((hps, tq, dh), F32), pltpu.VMEM((hps, tq, dh), F32),
                        pltpu.VMEM((hps, tq, dh), F32)],
        compiler_params=_params(3), name="fox_prompt",
    )(q, k, v, frow)


def _past_bias_body(n_pages, page, nh, *refs):
    lf_refs = refs[1:1 + n_pages]
    out = refs[1 + n_pages]
    rows, later_in_page, spread_heads = refs[2 + n_pages:]
    w = page * nh
    npp = rows.shape[0]

    @pl.when(pl.program_id(0) == 0)
    def _():
        a = lax.broadcasted_iota(jnp.int32, (w, w), 0)
        c = lax.broadcasted_iota(jnp.int32, (w, w), 1)
        later_in_page[...] = (((a % nh) == (c % nh)) & ((a // nh) > (c // nh))).astype(BF16)
        k = lax.broadcasted_iota(jnp.int32, spread_heads.shape, 0)
        c2 = lax.broadcasted_iota(jnp.int32, spread_heads.shape, 1)
        spread_heads[...] = ((k % LANES) == (c2 % nh)).astype(BF16)
        rows[...] = jnp.zeros_like(rows)

    for p in range(n_pages):
        rows[p:p + 1, :] = lf_refs[p][0]
    lf = rows[...]
    within3 = jnp.dot(jnp.concatenate(_split3(lf), axis=0), later_in_page[...], preferred_element_type=F32)
    within = within3[:npp] + within3[npp:2 * npp] + within3[2 * npp:]
    first = (within + lf)[:, :LANES]
    totals = jnp.dot(jnp.concatenate(_split3(first), axis=1), spread_heads[...], preferred_element_type=F32)
    pr = lax.broadcasted_iota(jnp.int32, (npp, npp), 0)
    pc = lax.broadcasted_iota(jnp.int32, (npp, npp), 1)
    later_pages = (pc > pr).astype(BF16)
    after3 = jnp.dot(later_pages, jnp.concatenate(_split3(totals), axis=1), preferred_element_type=F32)
    after = after3[:, :w] + after3[:, w:2 * w] + after3[:, 2 * w:]
    out[0] = (within + after)[:n_pages, :]


def _past_bias(cache_lf, page_table):
    n_phys, page, nh = cache_lf.shape
    nb, n_pages = page_table.shape
    w = page * nh
    np_pad = -(-n_pages // LANES) * LANES
    lf = cache_lf.reshape(n_phys, 1, w)
    in_specs = [pl.BlockSpec((1, 1, w), lambda b, pt, p=p: (pt[b, p], 0, 0)) for p in range(n_pages)]
    grid_spec = pltpu.PrefetchScalarGridSpec(
        num_scalar_prefetch=1, grid=(nb,), in_specs=in_specs,
        out_specs=pl.BlockSpec((1, n_pages, w), lambda b, pt: (b, 0, 0)),
        scratch_shapes=[pltpu.VMEM((np_pad, w), F32), pltpu.VMEM((w, w), BF16), pltpu.VMEM((3 * LANES, w), BF16)])
    return pl.pallas_call(
        functools.partial(_past_bias_body, n_pages, page, nh),
        grid_spec=grid_spec, out_shape=jax.ShapeDtypeStruct((nb, n_pages, w), F32),
        compiler_params=_params(1), name="fox_past_bias",
    )(page_table, *([lf] * n_pages))


def _fox_sample_body(pps, n_steps, nh, dec_seq, *refs):
    qall, knew, vnew, frow_new, rb = refs[1:6]
    krefs = refs[6:6 + pps]
    vrefs = refs[6 + pps:6 + 2 * pps]
    o = refs[6 + 2 * pps]
    m_s, l_s, acc_s, s_scr = refs[7 + 2 * pps:]
    step = pl.program_id(1)
    qv = qall[0]
    nrow = qv.shape[0]

    @pl.when(step == 0)
    def _():
        kn = knew[0].astype(BF16)
        g = lax.dot_general(qv, kn, NT_DIMS, preferred_element_type=F32)
        r = lax.broadcasted_iota(jnp.int32, g.shape, 0)
        c = lax.broadcasted_iota(jnp.int32, g.shape, 1)
        valid = ((c % nh) == (r // dec_seq)) & ((c // nh) <= (r % dec_seq))
        s = jnp.where(valid, g - frow_new[0], NEG_BIG)
        m = jnp.max(s, axis=-1, keepdims=True)
        p = jnp.exp(s - m)
        m_s[...] = m
        l_s[...] = jnp.sum(p, axis=-1, keepdims=True)
        acc_s[...] = jnp.dot(p.astype(BF16), vnew[0].astype(BF16), preferred_element_type=F32)

    lanes = krefs[0].shape[1]
    r = lax.broadcasted_iota(jnp.int32, (nrow, lanes), 0)
    c = lax.broadcasted_iota(jnp.int32, (nrow, lanes), 1)
    head_ok = (c % nh) == (r // dec_seq)
    m_prev = m_s[...]
    m_new = m_prev
    for i in range(pps):
        g = lax.dot_general(qv, krefs[i][0].astype(BF16), NT_DIMS, preferred_element_type=F32)
        s = jnp.where(head_ok, g + rb[0, i], NEG_BIG)
        s_scr[i] = s
        m_new = jnp.maximum(m_new, jnp.max(s, axis=-1, keepdims=True))
    alpha = jnp.exp(m_prev - m_new)
    psum = acc = None
    for i in range(pps):
        p = jnp.exp(s_scr[i] - m_new)
        pv = jnp.dot(p.astype(BF16), vrefs[i][0].astype(BF16), preferred_element_type=F32)
        ps = jnp.sum(p, axis=-1, keepdims=True)
        psum = ps if psum is None else psum + ps
        acc = pv if acc is None else acc + pv
    l_s[...] = alpha * l_s[...] + psum
    acc_s[...] = alpha * acc_s[...] + acc
    m_s[...] = m_new

    @pl.when(step == n_steps - 1)
    def _():
        o[0] = (acc_s[...] / l_s[...]).astype(o.dtype)


def _fox_sample(q, k, v, fn, cache_k, cache_v, cache_lf, page_table, dec_seq):
    ms = q.shape[0]
    nb = ms // dec_seq
    nh, dh = ATT_HEADS, ATT_HEAD_DIM
    n_phys, page = cache_k.shape[0], cache_k.shape[1]
    n_pages = page_table.shape[1]
    pps = 16
    while n_pages % pps:
        pps //= 2
    n_steps = n_pages // pps
    nrow = nh * dec_seq
    rbias = _past_bias(cache_lf, page_table).reshape(nb, n_pages, 1, page * nh)
    qall = jnp.transpose(q.reshape(nb, dec_seq, nh, dh), (0, 2, 1, 3)).reshape(nb, nrow, dh)
    knew = k.reshape(nb, dec_seq * nh, dh)
    vnew = v.reshape(nb, dec_seq * nh, dh)
    frow_new = fn.reshape(nb, 1, dec_seq * nh)
    ck = cache_k.reshape(n_phys, page * nh, dh)
    cv = cache_v.reshape(n_phys, page * nh, dh)
    per_b = lambda shape: pl.BlockSpec((1,) + shape, lambda b, s, pt: (b, 0, 0))
    in_specs = [per_b((nrow, dh)), per_b((dec_seq * nh, dh)), per_b((dec_seq * nh, dh)),
                per_b((1, dec_seq * nh)),
                pl.BlockSpec((1, pps, 1, page * nh), lambda b, s, pt: (b, s, 0, 0))]
    for _ in range(2):
        in_specs += [pl.BlockSpec((1, page * nh, dh), lambda b, s, pt, i=i: (pt[b, s * pps + i], 0, 0))
                     for i in range(pps)]
    grid_spec = pltpu.PrefetchScalarGridSpec(
        num_scalar_prefetch=1, grid=(nb, n_steps), in_specs=in_specs,
        out_specs=pl.BlockSpec((1, nrow, dh), lambda b, s, pt: (b, 0, 0)),
        scratch_shapes=[pltpu.VMEM((nrow, 1), F32), pltpu.VMEM((nrow, 1), F32), pltpu.VMEM((nrow, dh), F32),
                        pltpu.VMEM((pps, nrow, page * nh), F32)])
    o = pl.pallas_call(
        functools.partial(_fox_sample_body, pps, n_steps, nh, dec_seq),
        grid_spec=grid_spec, out_shape=jax.ShapeDtypeStruct((nb, nrow, dh), BF16),
        compiler_params=_params(2), name="fox_sample",
    )(page_table, qall, knew, vnew, frow_new, rbias, *([ck] * pps), *([cv] * pps))
    return jnp.transpose(o.reshape(nb, nh, dec_seq, dh), (0, 2, 1, 3)).reshape(ms, nh * dh)


def _ssd_body(t_in, nseq, carried, xs_r, z_r, b_r, c_r, dt_r, gc_r, sel_r, ex_r, *rest):
    if carried:
        hx_r, hb_r, hc_r, s0_r, y_o, st_o, st, carry = rest
    else:
        hx_r = hb_r = hc_r = s0_r = None
        y_o, st_o, st, carry = rest
    ci = pl.program_id(2)
    nc = pl.num_programs(2)
    L = t_in * nseq
    P, N = SSM_HEAD_DIM, D_STATE
    gw = xs_r.shape[1]
    R = gw // P

    @pl.when(ci == 0)
    def _():
        if carried:
            for s in range(nseq):
                st[:, s * N:(s + 1) * N] = s0_r[s, 0]
        else:
            st[...] = jnp.zeros_like(st)
            carry[...] = jnp.zeros_like(carry)

    d_skip = gc_r[SSM_CONV + 1:SSM_CONV + 2, 0:gw]
    a_rate = gc_r[SSM_CONV + 1:SSM_CONV + 2, gw:gw + LANES]
    gain = gc_r[SSM_CONV + 2:SSM_CONV + 3, 0:gw]

    def conv_act(raw_r, hist_r, lo, hi):
        x = raw_r[...]
        w_r = gc_r[0:SSM_CONV, lo:hi]
        bias = gc_r[SSM_CONV:SSM_CONV + 1, lo:hi]
        if carried:
            t = lax.broadcasted_iota(jnp.int32, x.shape, 0) % t_in
            y = _conv_rows(x, hist_r[...], w_r, SSM_CONV, t)
        else:
            y = _conv_rows(x, None, w_r, SSM_CONV, None)
            head = x[:SUBLANES, :]
            t8 = lax.broadcasted_iota(jnp.int32, head.shape, 0)
            y8 = _conv_rows(head, carry[:, lo:hi], w_r, SSM_CONV, t8)
            y = jnp.concatenate([y8, y[SUBLANES:, :]], axis=0)
            carry[:, lo:hi] = pltpu.roll(x[L - SUBLANES:, :], SSM_CONV - 1, axis=0)
        return _silu(y + bias)

    xs = conv_act(xs_r, hx_r, 0, gw)
    bmf = conv_act(b_r, hb_r, gw, gw + N)
    cmf = conv_act(c_r, hc_r, gw + N, gw + 2 * N)
    bm, cm = bmf.astype(BF16), cmf.astype(BF16)
    dt = jnp.dot(jnp.concatenate(_split3(dt_r[...]), axis=1), sel_r[...], preferred_element_type=F32)
    r = lax.broadcasted_iota(jnp.int32, (L, L), 0)
    c = lax.broadcasted_iota(jnp.int32, (L, L), 1)
    da = dt * (a_rate * LOG2E)
    if nseq == 1:
        tri = r >= c
        acum = jnp.dot(tri.astype(F32), da, preferred_element_type=F32, precision=HIGHEST)
        a_tot = acum[L - 1:L, :]
    else:
        same = (r // t_in) == (c // t_in)
        tri = same & (r >= c)
        sums = jnp.dot(jnp.concatenate([tri.astype(F32), same.astype(F32)], axis=0), da,
                       preferred_element_type=F32, precision=HIGHEST)
        acum, a_tot = sums[:L], sums[L:]
    acum_t = acum.T
    dt_t = dt.T.astype(BF16)
    cols = jnp.concatenate([dt * jnp.exp2(a_tot - acum), jnp.exp2(acum)], axis=0)
    wide = jnp.dot(jnp.concatenate(_split3(cols), axis=1), ex_r[...], preferred_element_type=F32)
    w_state, w_off = wide[:L], wide[L:]
    cb = lax.dot_general(cm, bm, NT_DIMS, preferred_element_type=F32).astype(BF16)
    if nseq == 1:
        cblk, bblk = cm, bm
        decay_t = acum_t
    else:
        seq_of_row = lax.broadcasted_iota(jnp.int32, (L, N), 0) // t_in
        cblk = jnp.concatenate([jnp.where(seq_of_row == s, cmf, 0.0) for s in range(nseq)], axis=1).astype(BF16)
        bblk = jnp.concatenate([jnp.where(seq_of_row == s, bmf, 0.0) for s in range(nseq)], axis=1).astype(BF16)
        decay_t = a_tot.T
    s_prev = st[...]
    y_off = lax.dot_general(cblk, s_prev.astype(BF16), NT_DIMS, preferred_element_type=F32) * w_off
    xb = xs.astype(BF16)
    ys = []
    for h in range(R):
        decay = jnp.exp2(jnp.where(tri, acum[:, h:h + 1] - acum_t[h:h + 1, :], -jnp.inf))
        m = cb * decay.astype(BF16) * dt_t[h:h + 1, :]
        ys.append(jnp.dot(m, xb[:, h * P:(h + 1) * P], preferred_element_type=F32))
    new_states = lax.dot_general((xs * w_state).astype(BF16), bblk, TN_DIMS, preferred_element_type=F32)
    for s in range(nseq):
        last = (s + 1) * t_in - 1
        for h in range(R):
            blk = (slice(h * P, (h + 1) * P), slice(s * N, (s + 1) * N))
            st[blk] = s_prev[blk] * jnp.exp2(decay_t[h:h + 1, last:last + 1]) + new_states[blk]
    y = jnp.concatenate(ys, axis=1) + y_off + d_skip * xs
    y = y * _silu(z_r[...])
    y = y * lax.rsqrt(jnp.mean(y * y, axis=-1, keepdims=True) + EPS)
    y_o[...] = (y * gain).astype(y_o.dtype)

    @pl.when(ci == nc - 1)
    def _():
        for s in range(nseq):
            st_o[s, 0] = st[:, s * N:(s + 1) * N]


def _ssd(xbc, z, dt, w_conv, b_conv, a_pad, d_exp, gn_w, hist, init_state, nb, seq):
    G, N, P = SSM_GROUPS, D_STATE, SSM_HEAD_DIM
    inner = z.shape[1]
    gw = inner // G
    rg = gw // P
    L = SSM_CHUNK
    carried = init_state is not None
    if carried:
        t_in, nc = seq, 1
        assert L % seq == 0 and nb % (L // seq) == 0, (nb, seq)
    else:
        t_in, nc = L, seq // L
        assert seq % L == 0, seq
    nseq = L // t_in
    nbb = nb // nseq
    nbc = inner // N
    row = lambda b, g, c: b * nc + c

    def triple():
        return [pl.BlockSpec((L, gw), lambda b, g, c: (row(b, g, c), g)),
                pl.BlockSpec((L, N), lambda b, g, c: (row(b, g, c), nbc + g)),
                pl.BlockSpec((L, N), lambda b, g, c: (row(b, g, c), nbc + G + g))]

    cw = gw + 2 * N

    def by_group(v):
        parts = [v[:, :inner].reshape(-1, G, gw), v[:, inner:inner + G * N].reshape(-1, G, N),
                 v[:, inner + G * N:].reshape(-1, G, N)]
        return jnp.transpose(jnp.concatenate(parts, axis=2), (1, 0, 2))

    zpad = lambda n: jnp.zeros((G, 1, n), F32)
    gconst = jnp.concatenate([
        by_group(w_conv.astype(F32)), by_group(b_conv.reshape(1, -1).astype(F32)),
        jnp.concatenate([d_exp.reshape(G, 1, gw), a_pad.reshape(G, 1, LANES), zpad(cw - gw - LANES)], axis=2),
        jnp.concatenate([gn_w.reshape(G, 1, gw), zpad(cw - gw)], axis=2),
        jnp.zeros((G, SUBLANES - SSM_CONV - 3, cw), F32)], axis=1)
    x_spec, b_spec, c_spec = triple()
    in_specs = [x_spec, pl.BlockSpec((L, gw), lambda b, g, c: (row(b, g, c), g)), b_spec, c_spec,
                pl.BlockSpec((L, LANES), lambda b, g, c: (row(b, g, c), 0)),
                pl.BlockSpec((None, SUBLANES, cw), lambda b, g, c: (g, 0, 0)),
                pl.BlockSpec((None, 3 * LANES, LANES), lambda b, g, c: (g, 0, 0)),
                pl.BlockSpec((3 * LANES, gw), lambda b, g, c: (0, 0))]
    lane_head = jnp.arange(3 * LANES) % LANES
    widen = (lane_head[:, None] == jnp.arange(gw)[None, :] // P).astype(BF16)
    lane = jnp.arange(LANES)
    pick = ((lane_head[None, :, None] == jnp.arange(G)[:, None, None] * rg + lane[None, None, :])
            & (lane[None, None, :] < rg)).astype(BF16)
    args = [xbc, z, xbc, xbc, dt, gconst, pick, widen]
    if carried:
        in_specs += triple() + [pl.BlockSpec((nseq, 1, gw, N), lambda b, g, c: (b, g, 0, 0))]
        args += [hist, hist, hist, init_state]
    y, st = pl.pallas_call(
        functools.partial(_ssd_body, t_in, nseq, carried),
        grid=(nbb, G, nc), in_specs=in_specs,
        out_specs=[pl.BlockSpec((L, gw), lambda b, g, c: (row(b, g, c), g)),
                   pl.BlockSpec((nseq, 1, gw, N), lambda b, g, c: (b, g, 0, 0))],
        out_shape=[jax.ShapeDtypeStruct((nb * seq, inner), BF16), jax.ShapeDtypeStruct((nb, G, gw, N), F32)],
        scratch_shapes=[pltpu.VMEM((gw, nseq * N), F32), pltpu.VMEM((SUBLANES, gw + 2 * N), F32)],
        compiler_params=_params(3), name="ssd",
    )(*args)
    return y, st


def _ffn(x, norm_w, w_gate, w_up, w_down, layer):
    h = _rmsnorm_pair(x, norm_w, BF16)
    a = _mm_pair([h], [_W(w_gate, layer), _W(w_up, layer)], out_dtype=BF16, mode="swiglu", name="ffn_gate_up")
    return _mm_pair([a], [_W(w_down, layer)], res=x, name="ffn_down")


def kernel(x_prompt, x_sample, cache_k, cache_v, cache_logf, page_table, state_sconv, state_mconv, state_ssm, norm_mix, norm_ffn, norm_final, w_in_even, b_forget, w_sconv, w_out_even, w_in_ssm, w_mconv, b_mconv, dt_bias, a_log, d_skip, w_gnorm, w_out_ssm, w_gate, w_up, w_down):
    bp, seq, d = x_prompt.shape
    bs, dec_seq, _ = x_sample.shape
    mp, ms = bp * seq, bs * dec_seq
    nh, dh = ATT_HEADS, ATT_HEAD_DIM
    att = nh * dh
    sc_ch = state_sconv.shape[-1]
    x = (x_prompt.reshape(mp, d), x_sample.reshape(ms, d))

    i = 0
    h = _rmsnorm_pair(x, norm_mix[0], BF16)
    c0 = 3 * sc_ch
    w_in_t = jnp.swapaxes(w_in_even, 1, 2)
    win = lambda col0, n: _W(w_in_t, i, col0=col0, n=n, transposed=True)
    gcu = _mm_pair([h], [win(0, c0)], name="in_even_conv")
    q = _mm_pair([h], [win(c0, att)], out_dtype=BF16, scale=dh ** -0.5, name="in_even_q")
    k = _mm_pair([h], [win(c0 + att, att)], name="in_even_k")
    v = _mm_pair([h], [win(c0 + 2 * att, att)], name="in_even_v")
    lf_p, fc_p, lf_s, fc_s = _forget_pair(h, w_in_even[i, :, c0 + 3 * att:], b_forget[i], bp, seq, dec_seq)
    logf_p = lf_p[:, :nh].reshape(bp, seq, nh)
    logf_s = lf_s[:, :nh].reshape(bs, dec_seq, nh)
    y_a, sconv_p, sconv_s = _sconv_pair(gcu, state_sconv[i], w_sconv[i], bp, seq, dec_seq)
    yb_p = _fox_prompt(q[0], k[0], v[0], fc_p[:, :nh].reshape(bp, seq, nh), bp, seq)
    yb_s = _fox_sample(q[1], k[1], v[1], fc_s[:, :nh].reshape(bs, dec_seq, nh),
                       cache_k[i], cache_v[i], cache_logf[i], page_table, dec_seq)
    x = _mm_pair([y_a, (yb_p, yb_s)],
                 [_W(w_out_even, i, row0=0, k=sc_ch), _W(w_out_even, i, row0=sc_ch, k=att)], res=x, name="out_even")
    x = _ffn(x, norm_ffn[0], w_gate, w_up, w_down, 0)

    j = 0
    G = SSM_GROUPS
    n_heads = a_log.shape[1]
    rg = n_heads // G
    inner = n_heads * SSM_HEAD_DIM
    conv_dim = state_mconv.shape[-1]
    h = _rmsnorm_pair(x, norm_mix[1], BF16)
    w_ssm_t = jnp.swapaxes(w_in_ssm, 1, 2)
    z = _mm_pair([h], [_W(w_ssm_t, j, n=inner, transposed=True)], name="in_ssm_z")
    xbc = _mm_pair([h], [_W(w_ssm_t, j, col0=inner, n=conv_dim, transposed=True)], name="in_ssm_xbc")
    assert n_heads <= LANES
    w_dt = jnp.zeros((d, LANES), BF16).at[:, :n_heads].set(w_in_ssm[j, :, inner + conv_dim:].astype(BF16))
    b_dt = jnp.zeros((1, LANES), F32).at[0, :n_heads].set(dt_bias[j].astype(F32))
    dt = _mm_pair([h], [_W(w_dt)], mode="softplus", bias=b_dt, name="in_ssm_dt")
    a_pad = jnp.zeros((G, LANES), F32).at[:, :rg].set(-jnp.exp(a_log[j].astype(F32)).reshape(G, rg)).reshape(1, G * LANES)
    d_exp = jnp.repeat(d_skip[j].astype(F32), SSM_HEAD_DIM).reshape(1, inner)
    gn_w = w_gnorm[j].astype(F32).reshape(1, inner)
    hw = SSM_CONV - 1
    mconv_p = xbc[0].reshape(bp, seq, conv_dim)[:, seq - hw:, :]
    mconv_s = xbc[1].reshape(bs, dec_seq, conv_dim)[:, dec_seq - hw:, :]
    yp, ssm_p = _ssd(xbc[0], z[0], dt[0], w_mconv[j], b_mconv[j], a_pad, d_exp, gn_w, None, None, bp, seq)
    init = state_ssm[j].astype(F32).reshape(bs, G, rg * SSM_HEAD_DIM, D_STATE)
    ysm, ssm_s = _ssd(xbc[1], z[1], dt[1], w_mconv[j], b_mconv[j], a_pad, d_exp, gn_w,
                      _pad_hist(state_mconv[j], dec_seq), init, bs, dec_seq)
    x = _mm_pair([(yp, ysm)], [_W(w_out_ssm, j)], res=x, name="out_ssm")
    x = _ffn(x, norm_ffn[1], w_gate, w_up, w_down, 1)

    y = _rmsnorm_pair(x, norm_final, F32)
    st_shape = (n_heads, SSM_HEAD_DIM, D_STATE)
    return (y[0].reshape(bp, seq, d), y[1].reshape(bs, dec_seq, d),
            k[0].reshape(1, bp, seq, nh, dh), v[0].reshape(1, bp, seq, nh, dh), logf_p[None],
            sconv_p[None], mconv_p[None], ssm_p.reshape((1, bp) + st_shape),
            k[1].reshape(1, bs, dec_seq, nh, dh), v[1].reshape(1, bs, dec_seq, nh, dh), logf_s[None],
            sconv_s[None], mconv_s[None], ssm_s.reshape((1, bs) + st_shape))
```

```python
import functools

import jax
import jax.numpy as jnp
from jax import lax
from jax.experimental import pallas as pl
from jax.experimental.pallas import tpu as pltpu

F32 = jnp.float32
BF16 = jnp.bfloat16

SC_WIDTH = 3
ATT_HEADS = 8
ATT_HEAD_DIM = 128
SSM_HEAD_DIM = 64
SSM_GROUPS = 8
D_STATE = 128
SSM_CONV = 4
SSM_CHUNK = 128
EPS = 1e-6

LANES = 128
SUBLANES = 8
VMEM_LIMIT_BYTES = 56 * 1024 * 1024
NEG_BIG = -1e30
LOG2E = 1.4426950408889634

HIGHEST = lax.Precision.HIGHEST
NT_DIMS = (((1,), (1,)), ((), ()))
TN_DIMS = (((0,), (0,)), ((), ()))


def _params(n_grid_dims):
    return pltpu.CompilerParams(dimension_semantics=("arbitrary",) * n_grid_dims,
                                vmem_limit_bytes=VMEM_LIMIT_BYTES)


def _row_tile(mp, pref):
    t = pref
    while mp % t:
        t //= 2
    assert t >= SUBLANES, (mp, pref)
    return t


def _rmsnorm_body(nt, xp, xs, w, op, os):
    i = pl.program_id(0)

    def norm(x):
        y = x * lax.rsqrt(jnp.mean(x * x, axis=-1, keepdims=True) + EPS)
        return (y * w[...]).astype(op.dtype)

    @pl.when(i < nt)
    def _():
        op[...] = norm(xp[...])

    @pl.when(i == nt)
    def _():
        os[...] = norm(xs[...])


def _rmsnorm_pair(x, w, out_dtype):
    xp, xs = x
    mp, d = xp.shape
    ms = xs.shape[0]
    tm = _row_tile(mp, 512)
    nt = mp // tm
    pidx = lambda i: (jnp.minimum(i, nt - 1), 0)
    return pl.pallas_call(
        functools.partial(_rmsnorm_body, nt),
        grid=(nt + 1,),
        in_specs=[pl.BlockSpec((tm, d), pidx), pl.BlockSpec((ms, d), lambda i: (0, 0)),
                  pl.BlockSpec((1, d), lambda i: (0, 0))],
        out_specs=[pl.BlockSpec((tm, d), pidx), pl.BlockSpec((ms, d), lambda i: (0, 0))],
        out_shape=[jax.ShapeDtypeStruct((mp, d), out_dtype), jax.ShapeDtypeStruct((ms, d), out_dtype)],
        compiler_params=_params(1), name="rmsnorm",
    )(xp, xs, w.reshape(1, d).astype(F32))


def _softplus(x):
    return jnp.maximum(x, 0.0) + jnp.log1p(jnp.exp(-jnp.abs(x)))


def _silu(x):
    return x * (1.0 / (1.0 + jnp.exp(-x)))


def _split3(v):
    hi = v.astype(BF16)
    r1 = v - hi.astype(F32)
    mid = r1.astype(BF16)
    lo = (r1 - mid.astype(F32)).astype(BF16)
    return hi, mid, lo


class _W:
    def __init__(self, arr, layer=None, row0=0, k=None, col0=0, n=None, transposed=False):
        self.arr, self.layer, self.row0, self.col0, self.transposed = arr, layer, row0, col0, transposed
        kdim, ndim = (-1, -2) if transposed else (-2, -1)
        self.k = arr.shape[kdim] if k is None else k
        self.n = arr.shape[ndim] if n is None else n

    def block(self, tn):
        return (tn, self.k) if self.transposed else (self.k, tn)

    def spec(self, tn):
        assert self.row0 % self.k == 0 and self.col0 % tn == 0
        rb, cb = self.row0 // self.k, self.col0 // tn
        if self.transposed:
            idx = lambda j, i: (cb + j, rb)
        else:
            idx = lambda j, i: (rb, cb + j)
        if self.layer is None:
            return pl.BlockSpec(self.block(tn), idx)
        layer = self.layer
        return pl.BlockSpec((None,) + self.block(tn), lambda j, i: (layer,) + idx(j, i))


def _mm_body(nt, n_lhs, mode, has_res, has_bias, scale, w_is_f32, w_transposed, *refs):
    i = pl.program_id(1)

    def mm(x, idx):
        dims = NT_DIMS if w_transposed[idx] else (((1,), (0,)), ((), ()))
        return lax.dot_general(x, wrefs[idx][...], dims, preferred_element_type=F32)

    xrefs = refs[:2 * n_lhs]
    n_w = len(w_is_f32)
    wrefs = list(refs[2 * n_lhs:2 * n_lhs + n_w])
    pos = 2 * n_lhs + n_w
    rp = rs = bias = None
    if has_res:
        rp, rs = refs[pos], refs[pos + 1]
        pos += 2
    if has_bias:
        bias = refs[pos]
        pos += 1
    op, os = refs[pos], refs[pos + 1]
    scratch = list(refs[pos + 2:])

    for idx in range(n_w):
        if w_is_f32[idx]:
            wsrc, wdst = wrefs[idx], scratch.pop(0)

            @pl.when(i == 0)
            def _(wsrc=wsrc, wdst=wdst):
                wdst[...] = wsrc[...].astype(BF16)

            wrefs[idx] = wdst

    def compute(which, out, res):
        if mode == "swiglu":
            x = xrefs[which][...]
            acc = _silu(mm(x, 0)) * mm(x, 1)
        else:
            acc = mm(xrefs[which][...], 0)
            for k in range(1, n_lhs):
                acc += mm(xrefs[2 * k + which][...], k)
        if scale is not None:
            acc = acc * scale
        if has_bias:
            acc = acc + bias[...]
        if mode == "softplus":
            acc = _softplus(acc)
        if has_res:
            acc = res[...] + acc
        out[...] = acc.astype(out.dtype)

    @pl.when(i == 0)
    def _():
        compute(1, os, rs)

    @pl.when(i > 0)
    def _():
        compute(0, op, rp)


def _mm_pair(xs_list, ws_list, *, out_dtype=F32, mode="plain", res=None, bias=None, scale=None, name="matmul"):
    n_lhs = len(xs_list)
    mp = xs_list[0][0].shape[0]
    ms = xs_list[0][1].shape[0]
    n = ws_list[0].n
    kmax = max(w.k for w in ws_list)
    tm = _row_tile(mp, 2048 if mode == "swiglu" else 1024 if kmax <= 4096 else 512)
    nt = mp // tm
    if n % 1024 == 0 and kmax <= 2048 and mode != "swiglu":
        tn = 1024
    elif n % 512 == 0:
        tn = 512
    else:
        assert n % LANES == 0 and n <= 512, n
        tn = n
    pidx_x = lambda j, i: (jnp.maximum(i - 1, 0), 0)
    pidx_o = lambda j, i: (jnp.maximum(i - 1, 0), j)
    in_specs, args = [], []
    for xp, xs in xs_list:
        kk = xp.shape[1]
        in_specs += [pl.BlockSpec((tm, kk), pidx_x), pl.BlockSpec((ms, kk), lambda j, i: (0, 0))]
        args += [xp, xs]
    w_is_f32, scratch = [], []
    for w in ws_list:
        in_specs.append(w.spec(tn))
        args.append(w.arr)
        w_is_f32.append(w.arr.dtype != BF16)
        if w_is_f32[-1]:
            scratch.append(pltpu.VMEM(w.block(tn), BF16))
    if res is not None:
        in_specs += [pl.BlockSpec((tm, tn), pidx_o), pl.BlockSpec((ms, tn), lambda j, i: (0, j))]
        args += [res[0], res[1]]
    if bias is not None:
        in_specs.append(pl.BlockSpec((1, tn), lambda j, i: (0, j)))
        args.append(bias.reshape(1, n).astype(F32))
    outs = pl.pallas_call(
        functools.partial(_mm_body, nt, n_lhs, mode, res is not None, bias is not None, scale, tuple(w_is_f32),
                          tuple(w.transposed for w in ws_list)),
        grid=(n // tn, nt + 1),
        in_specs=in_specs,
        out_specs=[pl.BlockSpec((tm, tn), pidx_o), pl.BlockSpec((ms, tn), lambda j, i: (0, j))],
        out_shape=[jax.ShapeDtypeStruct((mp, n), out_dtype), jax.ShapeDtypeStruct((ms, n), out_dtype)],
        scratch_shapes=scratch,
        compiler_params=_params(2), name=name,
    )(*args)
    return outs[0], outs[1]


def _log_sigmoid(x):
    return -_softplus(-x)


def _forget_body(nb, seq, dec_seq, ck, xp, xs, w, b, lp, fp, ls, fs):
    i = pl.program_id(0)

    @pl.when(i < nb)
    def _():
        r = lax.broadcasted_iota(jnp.int32, (ck, ck), 0)
        c = lax.broadcasted_iota(jnp.int32, (ck, ck), 1)
        tri = (r >= c).astype(F32)
        carry = jnp.zeros((1, LANES), F32)
        for t in range(seq // ck):
            f = jnp.dot(xp[t * ck:(t + 1) * ck, :], w[...], preferred_element_type=F32) + b[...]
            lf = _log_sigmoid(f)
            lp[t * ck:(t + 1) * ck, :] = lf
            cs = jnp.dot(tri, lf, preferred_element_type=F32, precision=HIGHEST) + carry
            fp[t * ck:(t + 1) * ck, :] = cs
            carry = cs[ck - 1:ck, :]

    @pl.when(i == nb)
    def _():
        ms = xs.shape[0]
        r = lax.broadcasted_iota(jnp.int32, (ms, ms), 0)
        c = lax.broadcasted_iota(jnp.int32, (ms, ms), 1)
        tri = ((r // dec_seq == c // dec_seq) & (r >= c)).astype(F32)
        f = jnp.dot(xs[...], w[...], preferred_element_type=F32) + b[...]
        lf = _log_sigmoid(f)
        ls[...] = lf
        fs[...] = jnp.dot(tri, lf, preferred_element_type=F32, precision=HIGHEST)


def _forget_pair(h, w_f, b_f, nb, seq, dec_seq):
    hp, hs = h
    mp, d = hp.shape
    ms = hs.shape[0]
    assert ms <= 512 and ms % SUBLANES == 0
    ck = _row_tile(seq, 256)
    n_h = w_f.shape[1]
    wpad = jnp.zeros((d, LANES), BF16).at[:, :n_h].set(w_f.astype(BF16))
    bpad = jnp.zeros((1, LANES), F32).at[0, :n_h].set(b_f.astype(F32))
    pidx = lambda i: (jnp.minimum(i, nb - 1), 0)
    cidx = lambda i: (0, 0)
    return pl.pallas_call(
        functools.partial(_forget_body, nb, seq, dec_seq, ck),
        grid=(nb + 1,),
        in_specs=[pl.BlockSpec((seq, d), pidx), pl.BlockSpec((ms, d), cidx),
                  pl.BlockSpec((d, LANES), cidx), pl.BlockSpec((1, LANES), cidx)],
        out_specs=[pl.BlockSpec((seq, LANES), pidx), pl.BlockSpec((seq, LANES), pidx),
                   pl.BlockSpec((ms, LANES), cidx), pl.BlockSpec((ms, LANES), cidx)],
        out_shape=[jax.ShapeDtypeStruct((mp, LANES), F32), jax.ShapeDtypeStruct((mp, LANES), F32),
                   jax.ShapeDtypeStruct((ms, LANES), F32), jax.ShapeDtypeStruct((ms, LANES), F32)],
        compiler_params=_params(1), name="forget_gates",
    )(hp, hs, wpad, bpad)


def _shifted(x, hist, k, width, t_in_seq):
    n = x.shape[0]
    xk = pltpu.roll(x, k, axis=0)
    if hist is None:
        return xk
    back = (width - 1 - k) % n
    hk = hist if back == 0 else pltpu.roll(hist, n - back, axis=0)
    return jnp.where(t_in_seq >= k, xk, hk)


def _conv_rows(x, hist, wref, width, t_in_seq):
    acc = x * wref[width - 1:width, :]
    for k in range(1, width):
        acc += _shifted(x, hist, k, width, t_in_seq) * wref[width - 1 - k:width - k, :]
    return acc


def _sconv_body(nt, tiles_per_seq, width, dec_seq, gbp, gcp, up, gbs, gcs, us, hs, w, yp, ys, zp_tail, zs, carry):
    i = pl.program_id(1)

    @pl.when(i < nt)
    def _():
        t0 = i % tiles_per_seq
        x = gcp[...] * up[...]
        ts = x.shape[0]

        @pl.when(t0 == 0)
        def _():
            carry[...] = jnp.zeros_like(carry)

        yp[...] = (gbp[...] * _conv_rows(x, None, w, width, None)).astype(yp.dtype)
        head = x[:SUBLANES, :]
        t8 = lax.broadcasted_iota(jnp.int32, head.shape, 0)
        yp[:SUBLANES, :] = (gbp[:SUBLANES, :] * _conv_rows(head, carry[...], w, width, t8)).astype(yp.dtype)
        tail = x[ts - SUBLANES:, :]
        carry[...] = pltpu.roll(tail, width - 1, axis=0)
        zp_tail[0] = tail

    @pl.when(i == nt)
    def _():
        x = gcs[...] * us[...]
        t = lax.broadcasted_iota(jnp.int32, x.shape, 0) % dec_seq
        ys[...] = (gbs[...] * _conv_rows(x, hs[...], w, width, t)).astype(ys.dtype)
        zs[...] = x


def _pad_hist(state, dec_seq):
    b, hw, c = state.shape
    assert dec_seq >= hw
    return jnp.concatenate([state.astype(F32), jnp.zeros((b, dec_seq - hw, c), F32)], axis=1).reshape(b * dec_seq, c)


def _sconv_pair(gcu, state, w, nb, seq, dec_seq):
    gp, gs = gcu
    mp = gp.shape[0]
    ms = gs.shape[0]
    ch = gp.shape[1] // 3
    tc = 512
    ncb = ch // tc
    ts = _row_tile(seq, 512)
    tps = seq // ts
    nt = mp // ts
    pr = lambda j, i: jnp.minimum(i, nt - 1)
    in_specs = []
    for part in range(3):
        in_specs.append(pl.BlockSpec((ts, tc), lambda j, i, part=part: (pr(j, i), part * ncb + j)))
    for part in range(3):
        in_specs.append(pl.BlockSpec((ms, tc), lambda j, i, part=part: (0, part * ncb + j)))
    in_specs += [pl.BlockSpec((ms, tc), lambda j, i: (0, j)), pl.BlockSpec((SC_WIDTH, tc), lambda j, i: (0, j))]
    out_specs = [pl.BlockSpec((ts, tc), lambda j, i: (pr(j, i), j)),
                 pl.BlockSpec((ms, tc), lambda j, i: (0, j)),
                 pl.BlockSpec((1, SUBLANES, tc), lambda j, i: (pr(j, i) // tps, 0, j)),
                 pl.BlockSpec((ms, tc), lambda j, i: (0, j))]
    out_shape = [jax.ShapeDtypeStruct((mp, ch), BF16), jax.ShapeDtypeStruct((ms, ch), BF16),
                 jax.ShapeDtypeStruct((nb, SUBLANES, ch), F32), jax.ShapeDtypeStruct((ms, ch), F32)]
    yp, ys, zp_tail, zs = pl.pallas_call(
        functools.partial(_sconv_body, nt, tps, SC_WIDTH, dec_seq),
        grid=(ncb, nt + 1), in_specs=in_specs, out_specs=out_specs, out_shape=out_shape,
        scratch_shapes=[pltpu.VMEM((SUBLANES, tc), F32)],
        compiler_params=_params(2), name="short_conv",
    )(gp, gp, gp, gs, gs, gs, _pad_hist(state, dec_seq), w.astype(F32))
    hw = SC_WIDTH - 1
    new_p = zp_tail[:, SUBLANES - hw:, :]
    new_s = zs.reshape(ms // dec_seq, dec_seq, ch)[:, dec_seq - hw:, :]
    return (yp, ys), new_p, new_s


def _fox_prompt_body(tq, hps, q, k, v, fr, o, m_s, l_s, acc_s):
    qi = pl.program_id(2)
    dh = ATT_HEAD_DIM
    m_s[...] = jnp.full_like(m_s, NEG_BIG)
    l_s[...] = jnp.zeros_like(l_s)
    acc_s[...] = jnp.zeros_like(acc_s)

    def step(ki, masked):
        start = pl.multiple_of(ki * tq, tq)
        for hh in range(hps):
            cols = slice(hh * dh, (hh + 1) * dh)
            kb = k[pl.ds(start, tq), cols].astype(BF16)
            vb = v[pl.ds(start, tq), cols].astype(BF16)
            s = lax.dot_general(q[:, cols], kb, NT_DIMS, preferred_element_type=F32) - fr[0, hh, ki]
            if masked:
                r = lax.broadcasted_iota(jnp.int32, s.shape, 0)
                c = lax.broadcasted_iota(jnp.int32, s.shape, 1)
                s = jnp.where(r >= c, s, NEG_BIG)
            m_prev = m_s[hh]
            m_new = jnp.maximum(m_prev, jnp.max(s, axis=-1, keepdims=True))
            p = jnp.exp(s - jnp.tile(m_new, (1, tq // dh)))
            alpha = jnp.exp(m_prev - m_new)
            l_s[hh] = alpha * l_s[hh] + jnp.sum(p, axis=-1, keepdims=True)
            acc_s[hh] = alpha * acc_s[hh] + jnp.dot(p.astype(BF16), vb, preferred_element_type=F32)
            m_s[hh] = m_new

    def loop_body(ki, carry):
        step(ki, False)
        return carry

    lax.fori_loop(0, qi, loop_body, 0)
    step(qi, True)
    for hh in range(hps):
        o[:, hh * dh:(hh + 1) * dh] = (acc_s[hh] / l_s[hh]).astype(o.dtype)


def _fox_prompt(q, k, v, fcum, nb, seq):
    mp = q.shape[0]
    dh = ATT_HEAD_DIM
    nh = q.shape[1] // dh
    tq = _row_tile(seq, 512)
    nq = seq // tq
    frow = jnp.transpose(fcum, (0, 2, 1)).reshape(nb, nh, nq, 1, tq)
    hps = 2 if nh % 2 == 0 else 1
    gw = hps * dh
    return pl.pallas_call(
        functools.partial(_fox_prompt_body, tq, hps),
        grid=(nb, nh // hps, nq),
        in_specs=[pl.BlockSpec((tq, gw), lambda b, h, i: (b * nq + i, h)),
                  pl.BlockSpec((seq, gw), lambda b, h, i: (b, h)),
                  pl.BlockSpec((seq, gw), lambda b, h, i: (b, h)),
                  pl.BlockSpec((1, hps, nq, 1, tq), lambda b, h, i: (b, h, 0, 0, 0))],
        out_specs=pl.BlockSpec((tq, gw), lambda b, h, i: (b * nq + i, h)),
        out_shape=jax.ShapeDtypeStruct((mp, nh * dh), BF16),
        scratch_shapes=[pltpu.VMEM((hps, tq, dh), F32), pltpu.VMEM((hps, tq, dh), F32),
                        pltpu.VMEM((hps, tq, dh), F32)],
        compiler_params=_params(3), name="fox_prompt",
    )(q, k, v, frow)


def _past_bias_body(n_pages, page, nh, *refs):
    lf_refs = refs[1:1 + n_pages]
    out = refs[1 + n_pages]
    rows, later_in_page, spread_heads = refs[2 + n_pages:]
    w = page * nh
    npp = rows.shape[0]

    @pl.when(pl.program_id(0) == 0)
    def _():
        a = lax.broadcasted_iota(jnp.int32, (w, w), 0)
        c = lax.broadcasted_iota(jnp.int32, (w, w), 1)
        later_in_page[...] = (((a % nh) == (c % nh)) & ((a // nh) > (c // nh))).astype(BF16)
        k = lax.broadcasted_iota(jnp.int32, spread_heads.shape, 0)
        c2 = lax.broadcasted_iota(jnp.int32, spread_heads.shape, 1)
        spread_heads[...] = ((k % LANES) == (c2 % nh)).astype(BF16)
        rows[...] = jnp.zeros_like(rows)

    for p in range(n_pages):
        rows[p:p + 1, :] = lf_refs[p][0]
    lf = rows[...]
    within3 = jnp.dot(jnp.concatenate(_split3(lf), axis=0), later_in_page[...], preferred_element_type=F32)
    within = within3[:npp] + within3[npp:2 * npp] + within3[2 * npp:]
    first = (within + lf)[:, :LANES]
    totals = jnp.dot(jnp.concatenate(_split3(first), axis=1), spread_heads[...], preferred_element_type=F32)
    pr = lax.broadcasted_iota(jnp.int32, (npp, npp), 0)
    pc = lax.broadcasted_iota(jnp.int32, (npp, npp), 1)
    later_pages = (pc > pr).astype(BF16)
    after3 = jnp.dot(later_pages, jnp.concatenate(_split3(totals), axis=1), preferred_element_type=F32)
    after = after3[:, :w] + after3[:, w:2 * w] + after3[:, 2 * w:]
    out[0] = (within + after)[:n_pages, :]


def _past_bias(cache_lf, page_table):
    n_phys, page, nh = cache_lf.shape
    nb, n_pages = page_table.shape
    w = page * nh
    np_pad = -(-n_pages // LANES) * LANES
    lf = cache_lf.reshape(n_phys, 1, w)
    in_specs = [pl.BlockSpec((1, 1, w), lambda b, pt, p=p: (pt[b, p], 0, 0)) for p in range(n_pages)]
    grid_spec = pltpu.PrefetchScalarGridSpec(
        num_scalar_prefetch=1, grid=(nb,), in_specs=in_specs,
        out_specs=pl.BlockSpec((1, n_pages, w), lambda b, pt: (b, 0, 0)),
        scratch_shapes=[pltpu.VMEM((np_pad, w), F32), pltpu.VMEM((w, w), BF16), pltpu.VMEM((3 * LANES, w), BF16)])
    return pl.pallas_call(
        functools.partial(_past_bias_body, n_pages, page, nh),
        grid_spec=grid_spec, out_shape=jax.ShapeDtypeStruct((nb, n_pages, w), F32),
        compiler_params=_params(1), name="fox_past_bias",
    )(page_table, *([lf] * n_pages))


def _fox_sample_body(pps, n_steps, nh, dec_seq, *refs):
    qall, knew, vnew, frow_new, rb = refs[1:6]
    krefs = refs[6:6 + pps]
    vrefs = refs[6 + pps:6 + 2 * pps]
    o = refs[6 + 2 * pps]
    m_s, l_s, acc_s, s_scr = refs[7 + 2 * pps:]
    step = pl.program_id(1)
    qv = qall[0]
    nrow = qv.shape[0]

    @pl.when(step == 0)
    def _():
        kn = knew[0].astype(BF16)
        g = lax.dot_general(qv, kn, NT_DIMS, preferred_element_type=F32)
        r = lax.broadcasted_iota(jnp.int32, g.shape, 0)
        c = lax.broadcasted_iota(jnp.int32, g.shape, 1)
        valid = ((c % nh) == (r // dec_seq)) & ((c // nh) <= (r % dec_seq))
        s = jnp.where(valid, g - frow_new[0], NEG_BIG)
        m = jnp.max(s, axis=-1, keepdims=True)
        p = jnp.exp(s - m)
        m_s[...] = m
        l_s[...] = jnp.sum(p, axis=-1, keepdims=True)
        acc_s[...] = jnp.dot(p.astype(BF16), vnew[0].astype(BF16), preferred_element_type=F32)

    lanes = krefs[0].shape[1]
    r = lax.broadcasted_iota(jnp.int32, (nrow, lanes), 0)
    c = lax.broadcasted_iota(jnp.int32, (nrow, lanes), 1)
    head_ok = (c % nh) == (r // dec_seq)
    m_prev = m_s[...]
    m_new = m_prev
    for i in range(pps):
        g = lax.dot_general(qv, krefs[i][0].astype(BF16), NT_DIMS, preferred_element_type=F32)
        s = jnp.where(head_ok, g + rb[0, i:i + 1, :], NEG_BIG)
        s_scr[i] = s
        m_new = jnp.maximum(m_new, jnp.max(s, axis=-1, keepdims=True))
    alpha = jnp.exp(m_prev - m_new)
    psum = acc = None
    for i in range(pps):
        p = jnp.exp(s_scr[i] - m_new)
        pv = jnp.dot(p.astype(BF16), vrefs[i][0].astype(BF16), preferred_element_type=F32)
        ps = jnp.sum(p, axis=-1, keepdims=True)
        psum = ps if psum is None else psum + ps
        acc = pv if acc is None else acc + pv
    l_s[...] = alpha * l_s[...] + psum
    acc_s[...] = alpha * acc_s[...] + acc
    m_s[...] = m_new

    @pl.when(step == n_steps - 1)
    def _():
        o[0] = (acc_s[...] / l_s[...]).astype(o.dtype)


def _fox_sample(q, k, v, fn, cache_k, cache_v, cache_lf, page_table, dec_seq):
    ms = q.shape[0]
    nb = ms // dec_seq
    nh, dh = ATT_HEADS, ATT_HEAD_DIM
    n_phys, page = cache_k.shape[0], cache_k.shape[1]
    n_pages = page_table.shape[1]
    pps = 16
    while n_pages % pps:
        pps //= 2
    n_steps = n_pages // pps
    nrow = nh * dec_seq
    rbias = _past_bias(cache_lf, page_table)
    qall = jnp.transpose(q.reshape(nb, dec_seq, nh, dh), (0, 2, 1, 3)).reshape(nb, nrow, dh)
    knew = k.reshape(nb, dec_seq * nh, dh)
    vnew = v.reshape(nb, dec_seq * nh, dh)
    frow_new = fn.reshape(nb, 1, dec_seq * nh)
    ck = cache_k.reshape(n_phys, page * nh, dh)
    cv = cache_v.reshape(n_phys, page * nh, dh)
    per_b = lambda shape: pl.BlockSpec((1,) + shape, lambda b, s, pt: (b, 0, 0))
    in_specs = [per_b((nrow, dh)), per_b((dec_seq * nh, dh)), per_b((dec_seq * nh, dh)),
                per_b((1, dec_seq * nh)),
                pl.BlockSpec((1, pps, page * nh), lambda b, s, pt: (b, s, 0))]
    for _ in range(2):
        in_specs += [pl.BlockSpec((1, page * nh, dh), lambda b, s, pt, i=i: (pt[b, s * pps + i], 0, 0))
                     for i in range(pps)]
    grid_spec = pltpu.PrefetchScalarGridSpec(
        num_scalar_prefetch=1, grid=(nb, n_steps), in_specs=in_specs,
        out_specs=pl.BlockSpec((1, nrow, dh), lambda b, s, pt: (b, 0, 0)),
        scratch_shapes=[pltpu.VMEM((nrow, 1), F32), pltpu.VMEM((nrow, 1), F32), pltpu.VMEM((nrow, dh), F32),
                        pltpu.VMEM((pps, nrow, page * nh), F32)])
    o = pl.pallas_call(
        functools.partial(_fox_sample_body, pps, n_steps, nh, dec_seq),
        grid_spec=grid_spec, out_shape=jax.ShapeDtypeStruct((nb, nrow, dh), BF16),
        compiler_params=_params(2), name="fox_sample",
    )(page_table, qall, knew, vnew, frow_new, rbias, *([ck] * pps), *([cv] * pps))
    return jnp.transpose(o.reshape(nb, nh, dec_seq, dh), (0, 2, 1, 3)).reshape(ms, nh * dh)


def _ssd_body(t_in, nseq, nsub, carried, xs_r, z_r, b_r, c_r, dt_r, gc_r, sel_r, ex_r, *rest):
    if carried:
        hx_r, hb_r, hc_r, s0_r, y_o, st_o, st, carry = rest
    else:
        hx_r = hb_r = hc_r = s0_r = None
        y_o, st_o, st, carry = rest
    ci = pl.program_id(2)
    nc = pl.num_programs(2)
    L = t_in * nseq
    P, N = SSM_HEAD_DIM, D_STATE
    gw = xs_r.shape[1]
    R = gw // P

    @pl.when(ci == 0)
    def _():
        if carried:
            for s in range(nseq):
                st[:, s * N:(s + 1) * N] = s0_r[s, 0]
        else:
            st[...] = jnp.zeros_like(st)
            carry[...] = jnp.zeros_like(carry)

    d_skip = gc_r[SSM_CONV + 1:SSM_CONV + 2, 0:gw]
    a_rate = gc_r[SSM_CONV + 1:SSM_CONV + 2, gw:gw + LANES]
    gain = gc_r[SSM_CONV + 2:SSM_CONV + 3, 0:gw]

    def conv_act(raw_r, hist_r, lo, hi):
        x = raw_r[...]
        w_r = gc_r[0:SSM_CONV, lo:hi]
        bias = gc_r[SSM_CONV:SSM_CONV + 1, lo:hi]
        if carried:
            t = lax.broadcasted_iota(jnp.int32, x.shape, 0) % t_in
            y = _conv_rows(x, hist_r[...], w_r, SSM_CONV, t)
        else:
            y = _conv_rows(x, None, w_r, SSM_CONV, None)
            head = x[:SUBLANES, :]
            t8 = lax.broadcasted_iota(jnp.int32, head.shape, 0)
            y8 = _conv_rows(head, carry[:, lo:hi], w_r, SSM_CONV, t8)
            y = jnp.concatenate([y8, y[SUBLANES:, :]], axis=0)
            carry[:, lo:hi] = pltpu.roll(x[x.shape[0] - SUBLANES:, :], SSM_CONV - 1, axis=0)
        return _silu(y + bias)

    xs_blk = conv_act(xs_r, hx_r, 0, gw)
    bmf_blk = conv_act(b_r, hb_r, gw, gw + N)
    cmf_blk = conv_act(c_r, hc_r, gw + N, gw + 2 * N)
    dt_blk = jnp.dot(jnp.concatenate(_split3(dt_r[...]), axis=1), sel_r[...], preferred_element_type=F32)
    r = lax.broadcasted_iota(jnp.int32, (L, L), 0)
    c = lax.broadcasted_iota(jnp.int32, (L, L), 1)
    for u in range(nsub):
        rows = slice(u * L, (u + 1) * L)
        _ssd_chunk(t_in, nseq, r, c, xs_blk[rows], bmf_blk[rows], cmf_blk[rows], dt_blk[rows], z_r[rows, :],
                   a_rate, d_skip, gain, ex_r, st, y_o.at[rows, :])

    @pl.when(ci == nc - 1)
    def _():
        for s in range(nseq):
            st_o[s, 0] = st[:, s * N:(s + 1) * N]


def _ssd_chunk(t_in, nseq, r, c, xs, bmf, cmf, dt, z, a_rate, d_skip, gain, ex_r, st, y_o):
    L = t_in * nseq
    P, N = SSM_HEAD_DIM, D_STATE
    R = xs.shape[1] // P
    bm, cm = bmf.astype(BF16), cmf.astype(BF16)
    da = dt * (a_rate * LOG2E)
    if nseq == 1:
        tri = r >= c
        acum = jnp.dot(tri.astype(F32), da, preferred_element_type=F32, precision=HIGHEST)
        a_tot = acum[L - 1:L, :]
    else:
        same = (r // t_in) == (c // t_in)
        tri = same & (r >= c)
        sums = jnp.dot(jnp.concatenate([tri.astype(F32), same.astype(F32)], axis=0), da,
                       preferred_element_type=F32, precision=HIGHEST)
        acum, a_tot = sums[:L], sums[L:]
    acum_t = acum.T
    dt_t = dt.T.astype(BF16)
    cols = jnp.concatenate([dt * jnp.exp2(a_tot - acum), jnp.exp2(acum)], axis=0)
    wide = jnp.dot(jnp.concatenate(_split3(cols), axis=1), ex_r[...], preferred_element_type=F32)
    w_state, w_off = wide[:L], wide[L:]
    cb = lax.dot_general(cm, bm, NT_DIMS, preferred_element_type=F32).astype(BF16)
    if nseq == 1:
        cblk, bblk = cm, bm
        decay_t = acum_t
    else:
        seq_of_row = lax.broadcasted_iota(jnp.int32, (L, N), 0) // t_in
        cblk = jnp.concatenate([jnp.where(seq_of_row == s, cmf, 0.0) for s in range(nseq)], axis=1).astype(BF16)
        bblk = jnp.concatenate([jnp.where(seq_of_row == s, bmf, 0.0) for s in range(nseq)], axis=1).astype(BF16)
        decay_t = a_tot.T
    s_prev = st[...]
    y_off = lax.dot_general(cblk, s_prev.astype(BF16), NT_DIMS, preferred_element_type=F32) * w_off
    xb = xs.astype(BF16)
    ys = []
    for h in range(R):
        decay = jnp.exp2(jnp.where(tri, acum[:, h:h + 1] - acum_t[h:h + 1, :], -jnp.inf))
        m = cb * decay.astype(BF16) * dt_t[h:h + 1, :]
        ys.append(jnp.dot(m, xb[:, h * P:(h + 1) * P], preferred_element_type=F32))
    new_states = lax.dot_general((xs * w_state).astype(BF16), bblk, TN_DIMS, preferred_element_type=F32)
    for s in range(nseq):
        last = (s + 1) * t_in - 1
        for h in range(R):
            blk = (slice(h * P, (h + 1) * P), slice(s * N, (s + 1) * N))
            st[blk] = s_prev[blk] * jnp.exp2(decay_t[h:h + 1, last:last + 1]) + new_states[blk]
    y = jnp.concatenate(ys, axis=1) + y_off + d_skip * xs
    y = y * _silu(z)
    y = y * lax.rsqrt(jnp.mean(y * y, axis=-1, keepdims=True) + EPS)
    y_o[...] = (y * gain).astype(y_o.dtype)


def _ssd(xbc, z, dt, w_conv, b_conv, a_pad, d_exp, gn_w, hist, init_state, nb, seq):
    G, N, P = SSM_GROUPS, D_STATE, SSM_HEAD_DIM
    inner = z.shape[1]
    gw = inner // G
    rg = gw // P
    L = SSM_CHUNK
    carried = init_state is not None
    if carried:
        t_in, nsub = seq, 1
        assert L % seq == 0 and nb % (L // seq) == 0, (nb, seq)
    else:
        t_in, nsub = L, 8 if seq % (8 * L) == 0 else 1
        assert seq % L == 0, seq
    nseq = L // t_in
    nbb = nb // nseq
    nbc = inner // N
    lb = L * nsub
    nc = max(seq // lb, 1)
    row = lambda b, g, c: b * nc + c

    def triple():
        return [pl.BlockSpec((lb, gw), lambda b, g, c: (row(b, g, c), g)),
                pl.BlockSpec((lb, N), lambda b, g, c: (row(b, g, c), nbc + g)),
                pl.BlockSpec((lb, N), lambda b, g, c: (row(b, g, c), nbc + G + g))]

    cw = gw + 2 * N

    def by_group(v):
        parts = [v[:, :inner].reshape(-1, G, gw), v[:, inner:inner + G * N].reshape(-1, G, N),
                 v[:, inner + G * N:].reshape(-1, G, N)]
        return jnp.transpose(jnp.concatenate(parts, axis=2), (1, 0, 2))

    zpad = lambda n: jnp.zeros((G, 1, n), F32)
    gconst = jnp.concatenate([
        by_group(w_conv.astype(F32)), by_group(b_conv.reshape(1, -1).astype(F32)),
        jnp.concatenate([d_exp.reshape(G, 1, gw), a_pad.reshape(G, 1, LANES), zpad(cw - gw - LANES)], axis=2),
        jnp.concatenate([gn_w.reshape(G, 1, gw), zpad(cw - gw)], axis=2),
        jnp.zeros((G, SUBLANES - SSM_CONV - 3, cw), F32)], axis=1)
    x_spec, b_spec, c_spec = triple()
    in_specs = [x_spec, pl.BlockSpec((lb, gw), lambda b, g, c: (row(b, g, c), g)), b_spec, c_spec,
                pl.BlockSpec((lb, LANES), lambda b, g, c: (row(b, g, c), 0)),
                pl.BlockSpec((None, SUBLANES, cw), lambda b, g, c: (g, 0, 0)),
                pl.BlockSpec((None, 3 * LANES, LANES), lambda b, g, c: (g, 0, 0)),
                pl.BlockSpec((3 * LANES, gw), lambda b, g, c: (0, 0))]
    lane_head = jnp.arange(3 * LANES) % LANES
    widen = (lane_head[:, None] == jnp.arange(gw)[None, :] // P).astype(BF16)
    lane = jnp.arange(LANES)
    pick = ((lane_head[None, :, None] == jnp.arange(G)[:, None, None] * rg + lane[None, None, :])
            & (lane[None, None, :] < rg)).astype(BF16)
    args = [xbc, z, xbc, xbc, dt, gconst, pick, widen]
    if carried:
        in_specs += triple() + [pl.BlockSpec((nseq, 1, gw, N), lambda b, g, c: (b, g, 0, 0))]
        args += [hist, hist, hist, init_state]
    y, st = pl.pallas_call(
        functools.partial(_ssd_body, t_in, nseq, nsub, carried),
        grid=(nbb, G, nc), in_specs=in_specs,
        out_specs=[pl.BlockSpec((lb, gw), lambda b, g, c: (row(b, g, c), g)),
                   pl.BlockSpec((nseq, 1, gw, N), lambda b, g, c: (b, g, 0, 0))],
        out_shape=[jax.ShapeDtypeStruct((nb * seq, inner), BF16), jax.ShapeDtypeStruct((nb, G, gw, N), F32)],
        scratch_shapes=[pltpu.VMEM((gw, nseq * N), F32), pltpu.VMEM((SUBLANES, gw + 2 * N), F32)],
        compiler_params=_params(3), name="ssd",
    )(*args)
    return y, st


def _ffn(x, norm_w, w_gate, w_up, w_down, layer):
    h = _rmsnorm_pair(x, norm_w, BF16)
    a = _mm_pair([h], [_W(w_gate, layer), _W(w_up, layer)], out_dtype=BF16, mode="swiglu", name="ffn_gate_up")
    return _mm_pair([a], [_W(w_down, layer)], res=x, name="ffn_down")


def kernel(x_prompt, x_sample, cache_k, cache_v, cache_logf, page_table, state_sconv, state_mconv, state_ssm, norm_mix, norm_ffn, norm_final, w_in_even, b_forget, w_sconv, w_out_even, w_in_ssm, w_mconv, b_mconv, dt_bias, a_log, d_skip, w_gnorm, w_out_ssm, w_gate, w_up, w_down):
    bp, seq, d = x_prompt.shape
    bs, dec_seq, _ = x_sample.shape
    mp, ms = bp * seq, bs * dec_seq
    nh, dh = ATT_HEADS, ATT_HEAD_DIM
    att = nh * dh
    sc_ch = state_sconv.shape[-1]
    x = (x_prompt.reshape(mp, d), x_sample.reshape(ms, d))

    i = 0
    h = _rmsnorm_pair(x, norm_mix[0], BF16)
    c0 = 3 * sc_ch
    w_in_t = jnp.swapaxes(w_in_even, 1, 2)
    win = lambda col0, n: _W(w_in_t, i, col0=col0, n=n, transposed=True)
    gcu = _mm_pair([h], [win(0, c0)], name="in_even_conv")
    q = _mm_pair([h], [win(c0, att)], out_dtype=BF16, scale=dh ** -0.5, name="in_even_q")
    k = _mm_pair([h], [win(c0 + att, att)], name="in_even_k")
    v = _mm_pair([h], [win(c0 + 2 * att, att)], name="in_even_v")
    lf_p, fc_p, lf_s, fc_s = _forget_pair(h, w_in_even[i, :, c0 + 3 * att:], b_forget[i], bp, seq, dec_seq)
    logf_p = lf_p[:, :nh].reshape(bp, seq, nh)
    logf_s = lf_s[:, :nh].reshape(bs, dec_seq, nh)
    y_a, sconv_p, sconv_s = _sconv_pair(gcu, state_sconv[i], w_sconv[i], bp, seq, dec_seq)
    yb_p = _fox_prompt(q[0], k[0], v[0], fc_p[:, :nh].reshape(bp, seq, nh), bp, seq)
    yb_s = _fox_sample(q[1], k[1], v[1], fc_s[:, :nh].reshape(bs, dec_seq, nh),
                       cache_k[i], cache_v[i], cache_logf[i], page_table, dec_seq)
    x = _mm_pair([y_a, (yb_p, yb_s)],
                 [_W(w_out_even, i, row0=0, k=sc_ch), _W(w_out_even, i, row0=sc_ch, k=att)], res=x, name="out_even")
    x = _ffn(x, norm_ffn[0], w_gate, w_up, w_down, 0)

    j = 0
    G = SSM_GROUPS
    n_heads = a_log.shape[1]
    rg = n_heads // G
    inner = n_heads * SSM_HEAD_DIM
    conv_dim = state_mconv.shape[-1]
    h = _rmsnorm_pair(x, norm_mix[1], BF16)
    w_ssm_t = jnp.swapaxes(w_in_ssm, 1, 2)
    z = _mm_pair([h], [_W(w_ssm_t, j, n=inner, transposed=True)], name="in_ssm_z")
    xbc = _mm_pair([h], [_W(w_ssm_t, j, col0=inner, n=conv_dim, transposed=True)], name="in_ssm_xbc")
    assert n_heads <= LANES
    w_dt = jnp.zeros((d, LANES), BF16).at[:, :n_heads].set(w_in_ssm[j, :, inner + conv_dim:].astype(BF16))
    b_dt = jnp.zeros((1, LANES), F32).at[0, :n_heads].set(dt_bias[j].astype(F32))
    dt = _mm_pair([h], [_W(w_dt)], mode="softplus", bias=b_dt, name="in_ssm_dt")
    a_pad = jnp.zeros((G, LANES), F32).at[:, :rg].set(-jnp.exp(a_log[j].astype(F32)).reshape(G, rg)).reshape(1, G * LANES)
    d_exp = jnp.repeat(d_skip[j].astype(F32), SSM_HEAD_DIM).reshape(1, inner)
    gn_w = w_gnorm[j].astype(F32).reshape(1, inner)
    hw = SSM_CONV - 1
    mconv_p = xbc[0].reshape(bp, seq, conv_dim)[:, seq - hw:, :]
    mconv_s = xbc[1].reshape(bs, dec_seq, conv_dim)[:, dec_seq - hw:, :]
    yp, ssm_p = _ssd(xbc[0], z[0], dt[0], w_mconv[j], b_mconv[j], a_pad, d_exp, gn_w, None, None, bp, seq)
    init = state_ssm[j].astype(F32).reshape(bs, G, rg * SSM_HEAD_DIM, D_STATE)
    ysm, ssm_s = _ssd(xbc[1], z[1], dt[1], w_mconv[j], b_mconv[j], a_pad, d_exp, gn_w,
                      _pad_hist(state_mconv[j], dec_seq), init, bs, dec_seq)
    x = _mm_pair([(yp, ysm)], [_W(w_out_ssm, j)], res=x, name="out_ssm")
    x = _ffn(x, norm_ffn[1], w_gate, w_up, w_down, 1)

    y = _rmsnorm_pair(x, norm_final, F32)
    st_shape = (n_heads, SSM_HEAD_DIM, D_STATE)
    return (y[0].reshape(bp, seq, d), y[1].reshape(bs, dec_seq, d),
            k[0].reshape(1, bp, seq, nh, dh), v[0].reshape(1, bp, seq, nh, dh), logf_p[None],
            sconv_p[None], mconv_p[None], ssm_p.reshape((1, bp) + st_shape),
            k[1].reshape(1, bs, dec_seq, nh, dh), v[1].reshape(1, bs, dec_seq, nh, dh), logf_s[None],
            sconv_s[None], mconv_s[None], ssm_s.reshape((1, bs) + st_shape))
```

```python
import functools

import jax
import jax.numpy as jnp
from jax import lax
from jax.experimental import pallas as pl
from jax.experimental.pallas import tpu as pltpu

F32 = jnp.float32
BF16 = jnp.bfloat16

SC_WIDTH = 3
ATT_HEADS = 8
ATT_HEAD_DIM = 128
SSM_HEAD_DIM = 64
SSM_GROUPS = 8
D_STATE = 128
SSM_CONV = 4
SSM_CHUNK = 128
EPS = 1e-6

LANES = 128
SUBLANES = 8
VMEM_LIMIT_BYTES = 56 * 1024 * 1024
NEG_BIG = -1e30
LOG2E = 1.4426950408889634

HIGHEST = lax.Precision.HIGHEST
NT_DIMS = (((1,), (1,)), ((), ()))
TN_DIMS = (((0,), (0,)), ((), ()))


def _params(n_grid_dims):
    return pltpu.CompilerParams(dimension_semantics=("arbitrary",) * n_grid_dims,
                                vmem_limit_bytes=VMEM_LIMIT_BYTES)


def _row_tile(mp, pref):
    t = pref
    while mp % t:
        t //= 2
    assert t >= SUBLANES, (mp, pref)
    return t


def _rmsnorm_body(nt, xp, xs, w, op, os):
    i = pl.program_id(0)

    def norm(x):
        y = x * lax.rsqrt(jnp.mean(x * x, axis=-1, keepdims=True) + EPS)
        return (y * w[...]).astype(op.dtype)

    @pl.when(i < nt)
    def _():
        op[...] = norm(xp[...])

    @pl.when(i == nt)
    def _():
        os[...] = norm(xs[...])


def _rmsnorm_pair(x, w, out_dtype):
    xp, xs = x
    mp, d = xp.shape
    ms = xs.shape[0]
    tm = _row_tile(mp, 1024)
    nt = mp // tm
    pidx = lambda i: (jnp.minimum(i, nt - 1), 0)
    return pl.pallas_call(
        functools.partial(_rmsnorm_body, nt),
        grid=(nt + 1,),
        in_specs=[pl.BlockSpec((tm, d), pidx), pl.BlockSpec((ms, d), lambda i: (0, 0)),
                  pl.BlockSpec((1, d), lambda i: (0, 0))],
        out_specs=[pl.BlockSpec((tm, d), pidx), pl.BlockSpec((ms, d), lambda i: (0, 0))],
        out_shape=[jax.ShapeDtypeStruct((mp, d), out_dtype), jax.ShapeDtypeStruct((ms, d), out_dtype)],
        compiler_params=_params(1), name="rmsnorm",
    )(xp, xs, w.reshape(1, d).astype(F32))


def _softplus(x):
    return jnp.maximum(x, 0.0) + jnp.log1p(jnp.exp(-jnp.abs(x)))


def _silu(x):
    return x * (1.0 / (1.0 + jnp.exp(-x)))


def _split3(v):
    hi = v.astype(BF16)
    r1 = v - hi.astype(F32)
    mid = r1.astype(BF16)
    lo = (r1 - mid.astype(F32)).astype(BF16)
    return hi, mid, lo


class _W:
    def __init__(self, arr, layer=None, row0=0, k=None, col0=0, n=None, transposed=False):
        self.arr, self.layer, self.row0, self.col0, self.transposed = arr, layer, row0, col0, transposed
        kdim, ndim = (-1, -2) if transposed else (-2, -1)
        self.k = arr.shape[kdim] if k is None else k
        self.n = arr.shape[ndim] if n is None else n

    def block(self, tn):
        return (tn, self.k) if self.transposed else (self.k, tn)

    def spec(self, tn):
        assert self.row0 % self.k == 0 and self.col0 % tn == 0
        rb, cb = self.row0 // self.k, self.col0 // tn
        if self.transposed:
            idx = lambda j, i: (cb + j, rb)
        else:
            idx = lambda j, i: (rb, cb + j)
        if self.layer is None:
            return pl.BlockSpec(self.block(tn), idx)
        layer = self.layer
        return pl.BlockSpec((None,) + self.block(tn), lambda j, i: (layer,) + idx(j, i))


def _mm_body(n_lhs, mode, has_res, has_bias, scale, w_is_f32, w_transposed, *refs):
    i = pl.program_id(1)

    def mm(x, idx):
        dims = NT_DIMS if w_transposed[idx] else (((1,), (0,)), ((), ()))
        return lax.dot_general(x, wrefs[idx][...], dims, preferred_element_type=F32)

    xrefs = refs[:2 * n_lhs]
    n_w = len(w_is_f32)
    wrefs = list(refs[2 * n_lhs:2 * n_lhs + n_w])
    pos = 2 * n_lhs + n_w
    rp = rs = bias = None
    if has_res:
        rp, rs = refs[pos], refs[pos + 1]
        pos += 2
    if has_bias:
        bias = refs[pos]
        pos += 1
    op, os = refs[pos], refs[pos + 1]
    scratch = list(refs[pos + 2:])

    for idx in range(n_w):
        if w_is_f32[idx]:
            wsrc, wdst = wrefs[idx], scratch.pop(0)

            @pl.when(i == 0)
            def _(wsrc=wsrc, wdst=wdst):
                wdst[...] = wsrc[...].astype(BF16)

            wrefs[idx] = wdst

    def compute(which, out, res):
        if mode == "swiglu":
            x = xrefs[which][...]
            acc = _silu(mm(x, 0)) * mm(x, 1)
        else:
            acc = mm(xrefs[which][...], 0)
            for k in range(1, n_lhs):
                acc += mm(xrefs[2 * k + which][...], k)
        if scale is not None:
            acc = acc * scale
        if has_bias:
            acc = acc + bias[...]
        if mode == "softplus":
            acc = _softplus(acc)
        if has_res:
            acc = res[...] + acc
        out[...] = acc.astype(out.dtype)

    @pl.when(i == 0)
    def _():
        compute(1, os, rs)

    @pl.when(i > 0)
    def _():
        compute(0, op, rp)


def _mm_pair(xs_list, ws_list, *, out_dtype=F32, mode="plain", res=None, bias=None, scale=None, name="matmul"):
    n_lhs = len(xs_list)
    mp = xs_list[0][0].shape[0]
    ms = xs_list[0][1].shape[0]
    n = ws_list[0].n
    kmax = max(w.k for w in ws_list)
    tm = _row_tile(mp, 2048 if mode == "swiglu" else 1024 if kmax <= 4096 else 512)
    nt = mp // tm
    if n % 1024 == 0 and kmax <= 2048 and mode != "swiglu":
        tn = 1024
    elif n % 512 == 0:
        tn = 512
    else:
        assert n % LANES == 0 and n <= 512, n
        tn = n
    pidx_x = lambda j, i: (jnp.maximum(i - 1, 0), 0)
    pidx_o = lambda j, i: (jnp.maximum(i - 1, 0), j)
    in_specs, args = [], []
    for xp, xs in xs_list:
        kk = xp.shape[1]
        in_specs += [pl.BlockSpec((tm, kk), pidx_x), pl.BlockSpec((ms, kk), lambda j, i: (0, 0))]
        args += [xp, xs]
    w_is_f32, scratch = [], []
    for w in ws_list:
        in_specs.append(w.spec(tn))
        args.append(w.arr)
        w_is_f32.append(w.arr.dtype != BF16)
        if w_is_f32[-1]:
            scratch.append(pltpu.VMEM(w.block(tn), BF16))
    if res is not None:
        in_specs += [pl.BlockSpec((tm, tn), pidx_o), pl.BlockSpec((ms, tn), lambda j, i: (0, j))]
        args += [res[0], res[1]]
    if bias is not None:
        in_specs.append(pl.BlockSpec((1, tn), lambda j, i: (0, j)))
        args.append(bias.reshape(1, n).astype(F32))
    outs = pl.pallas_call(
        functools.partial(_mm_body, n_lhs, mode, res is not None, bias is not None, scale, tuple(w_is_f32),
                          tuple(w.transposed for w in ws_list)),
        grid=(n // tn, nt + 1),
        in_specs=in_specs,
        out_specs=[pl.BlockSpec((tm, tn), pidx_o), pl.BlockSpec((ms, tn), lambda j, i: (0, j))],
        out_shape=[jax.ShapeDtypeStruct((mp, n), out_dtype), jax.ShapeDtypeStruct((ms, n), out_dtype)],
        scratch_shapes=scratch,
        compiler_params=_params(2), name=name,
    )(*args)
    return outs[0], outs[1]


def _log_sigmoid(x):
    return -_softplus(-x)


def _forget_body(nb, seq, dec_seq, ck, xp, xs, w, b, lp, fp, ls, fs):
    i = pl.program_id(0)

    @pl.when(i < nb)
    def _():
        r = lax.broadcasted_iota(jnp.int32, (ck, ck), 0)
        c = lax.broadcasted_iota(jnp.int32, (ck, ck), 1)
        tri = (r >= c).astype(F32)
        carry = jnp.zeros((1, LANES), F32)
        for t in range(seq // ck):
            f = jnp.dot(xp[t * ck:(t + 1) * ck, :], w[...], preferred_element_type=F32) + b[...]
            lf = _log_sigmoid(f)
            lp[t * ck:(t + 1) * ck, :] = lf
            cs = jnp.dot(tri, lf, preferred_element_type=F32, precision=HIGHEST) + carry
            fp[t * ck:(t + 1) * ck, :] = cs * LOG2E
            carry = cs[ck - 1:ck, :]

    @pl.when(i == nb)
    def _():
        ms = xs.shape[0]
        r = lax.broadcasted_iota(jnp.int32, (ms, ms), 0)
        c = lax.broadcasted_iota(jnp.int32, (ms, ms), 1)
        tri = ((r // dec_seq == c // dec_seq) & (r >= c)).astype(F32)
        f = jnp.dot(xs[...], w[...], preferred_element_type=F32) + b[...]
        lf = _log_sigmoid(f)
        ls[...] = lf
        fs[...] = jnp.dot(tri, lf, preferred_element_type=F32, precision=HIGHEST) * LOG2E


def _forget_pair(h, w_f, b_f, nb, seq, dec_seq):
    hp, hs = h
    mp, d = hp.shape
    ms = hs.shape[0]
    assert ms <= 512 and ms % SUBLANES == 0
    ck = _row_tile(seq, 256)
    n_h = w_f.shape[1]
    wpad = jnp.zeros((d, LANES), BF16).at[:, :n_h].set(w_f.astype(BF16))
    bpad = jnp.zeros((1, LANES), F32).at[0, :n_h].set(b_f.astype(F32))
    pidx = lambda i: (jnp.minimum(i, nb - 1), 0)
    cidx = lambda i: (0, 0)
    return pl.pallas_call(
        functools.partial(_forget_body, nb, seq, dec_seq, ck),
        grid=(nb + 1,),
        in_specs=[pl.BlockSpec((seq, d), pidx), pl.BlockSpec((ms, d), cidx),
                  pl.BlockSpec((d, LANES), cidx), pl.BlockSpec((1, LANES), cidx)],
        out_specs=[pl.BlockSpec((seq, LANES), pidx), pl.BlockSpec((seq, LANES), pidx),
                   pl.BlockSpec((ms, LANES), cidx), pl.BlockSpec((ms, LANES), cidx)],
        out_shape=[jax.ShapeDtypeStruct((mp, LANES), F32), jax.ShapeDtypeStruct((mp, LANES), F32),
                   jax.ShapeDtypeStruct((ms, LANES), F32), jax.ShapeDtypeStruct((ms, LANES), F32)],
        compiler_params=_params(1), name="forget_gates",
    )(hp, hs, wpad, bpad)


def _shifted(x, hist, k, width, t_in_seq):
    n = x.shape[0]
    xk = pltpu.roll(x, k, axis=0)
    if hist is None:
        return xk
    back = (width - 1 - k) % n
    hk = hist if back == 0 else pltpu.roll(hist, n - back, axis=0)
    return jnp.where(t_in_seq >= k, xk, hk)


def _conv_rows(x, hist, wref, width, t_in_seq):
    acc = x * wref[width - 1:width, :]
    for k in range(1, width):
        acc += _shifted(x, hist, k, width, t_in_seq) * wref[width - 1 - k:width - k, :]
    return acc


def _sconv_body(nt, tiles_per_seq, width, dec_seq, gbp, gcp, up, gbs, gcs, us, hs, w, yp, ys, zp_tail, zs, carry):
    i = pl.program_id(1)

    @pl.when(i < nt)
    def _():
        t0 = i % tiles_per_seq
        x = gcp[...] * up[...]
        ts = x.shape[0]

        @pl.when(t0 == 0)
        def _():
            carry[...] = jnp.zeros_like(carry)

        yp[...] = (gbp[...] * _conv_rows(x, None, w, width, None)).astype(yp.dtype)
        head = x[:SUBLANES, :]
        t8 = lax.broadcasted_iota(jnp.int32, head.shape, 0)
        yp[:SUBLANES, :] = (gbp[:SUBLANES, :] * _conv_rows(head, carry[...], w, width, t8)).astype(yp.dtype)
        tail = x[ts - SUBLANES:, :]
        carry[...] = pltpu.roll(tail, width - 1, axis=0)
        zp_tail[0] = tail

    @pl.when(i == nt)
    def _():
        x = gcs[...] * us[...]
        t = lax.broadcasted_iota(jnp.int32, x.shape, 0) % dec_seq
        ys[...] = (gbs[...] * _conv_rows(x, hs[...], w, width, t)).astype(ys.dtype)
        zs[...] = x


def _pad_hist(state, dec_seq):
    b, hw, c = state.shape
    assert dec_seq >= hw
    return jnp.concatenate([state.astype(F32), jnp.zeros((b, dec_seq - hw, c), F32)], axis=1).reshape(b * dec_seq, c)


def _sconv_pair(gcu, state, w, nb, seq, dec_seq):
    gp, gs = gcu
    mp = gp.shape[0]
    ms = gs.shape[0]
    ch = gp.shape[1] // 3
    tc = 1024 if ch % 1024 == 0 else 512
    ncb = ch // tc
    ts = _row_tile(seq, 512)
    tps = seq // ts
    nt = mp // ts
    pr = lambda j, i: jnp.minimum(i, nt - 1)
    in_specs = []
    for part in range(3):
        in_specs.append(pl.BlockSpec((ts, tc), lambda j, i, part=part: (pr(j, i), part * ncb + j)))
    for part in range(3):
        in_specs.append(pl.BlockSpec((ms, tc), lambda j, i, part=part: (0, part * ncb + j)))
    in_specs += [pl.BlockSpec((ms, tc), lambda j, i: (0, j)), pl.BlockSpec((SC_WIDTH, tc), lambda j, i: (0, j))]
    out_specs = [pl.BlockSpec((ts, tc), lambda j, i: (pr(j, i), j)),
                 pl.BlockSpec((ms, tc), lambda j, i: (0, j)),
                 pl.BlockSpec((1, SUBLANES, tc), lambda j, i: (pr(j, i) // tps, 0, j)),
                 pl.BlockSpec((ms, tc), lambda j, i: (0, j))]
    out_shape = [jax.ShapeDtypeStruct((mp, ch), BF16), jax.ShapeDtypeStruct((ms, ch), BF16),
                 jax.ShapeDtypeStruct((nb, SUBLANES, ch), F32), jax.ShapeDtypeStruct((ms, ch), F32)]
    yp, ys, zp_tail, zs = pl.pallas_call(
        functools.partial(_sconv_body, nt, tps, SC_WIDTH, dec_seq),
        grid=(ncb, nt + 1), in_specs=in_specs, out_specs=out_specs, out_shape=out_shape,
        scratch_shapes=[pltpu.VMEM((SUBLANES, tc), F32)],
        compiler_params=_params(2), name="short_conv",
    )(gp, gp, gp, gs, gs, gs, _pad_hist(state, dec_seq), w.astype(F32))
    hw = SC_WIDTH - 1
    new_p = zp_tail[:, SUBLANES - hw:, :]
    new_s = zs.reshape(ms // dec_seq, dec_seq, ch)[:, dec_seq - hw:, :]
    return (yp, ys), new_p, new_s


def _fox_prompt_body(tq, hps, q, k, v, fr, o, m_s, l_s, acc_s):
    qi = pl.program_id(2)
    dh = ATT_HEAD_DIM
    m_s[...] = jnp.full_like(m_s, NEG_BIG)
    l_s[...] = jnp.zeros_like(l_s)
    acc_s[...] = jnp.zeros_like(acc_s)

    def step(ki, masked):
        start = pl.multiple_of(ki * tq, tq)
        for hh in range(hps):
            cols = slice(hh * dh, (hh + 1) * dh)
            kb = k[pl.ds(start, tq), cols].astype(BF16)
            vb = v[pl.ds(start, tq), cols].astype(BF16)
            s = lax.dot_general(q[:, cols], kb, NT_DIMS, preferred_element_type=F32) - fr[0, hh, ki]
            if masked:
                r = lax.broadcasted_iota(jnp.int32, s.shape, 0)
                c = lax.broadcasted_iota(jnp.int32, s.shape, 1)
                s = jnp.where(r >= c, s, NEG_BIG)
            m_prev = m_s[hh]
            m_new = jnp.maximum(m_prev, jnp.max(s, axis=-1, keepdims=True))
            p = jnp.exp2(s - jnp.tile(m_new, (1, tq // dh)))
            alpha = jnp.exp2(m_prev - m_new)
            l_s[hh] = alpha * l_s[hh] + jnp.sum(p, axis=-1, keepdims=True)
            acc_s[hh] = alpha * acc_s[hh] + jnp.dot(p.astype(BF16), vb, preferred_element_type=F32)
            m_s[hh] = m_new

    def loop_body(ki, carry):
        step(ki, False)
        return carry

    lax.fori_loop(0, qi, loop_body, 0)
    step(qi, True)
    for hh in range(hps):
        o[:, hh * dh:(hh + 1) * dh] = (acc_s[hh] / l_s[hh]).astype(o.dtype)


def _fox_prompt(q, k, v, fcum, nb, seq):
    mp = q.shape[0]
    dh = ATT_HEAD_DIM
    nh = q.shape[1] // dh
    tq = _row_tile(seq, 512)
    nq = seq // tq
    frow = jnp.transpose(fcum, (0, 2, 1)).reshape(nb, nh, nq, 1, tq)
    hps = 2 if nh % 2 == 0 else 1
    gw = hps * dh
    return pl.pallas_call(
        functools.partial(_fox_prompt_body, tq, hps),
        grid=(nb, nh // hps, nq),
        in_specs=[pl.BlockSpec((tq, gw), lambda b, h, i: (b * nq + i, h)),
                  pl.BlockSpec((seq, gw), lambda b, h, i: (b, h)),
                  pl.BlockSpec((seq, gw), lambda b, h, i: (b, h)),
                  pl.BlockSpec((1, hps, nq, 1, tq), lambda b, h, i: (b, h, 0, 0, 0))],
        out_specs=pl.BlockSpec((tq, gw), lambda b, h, i: (b * nq + i, h)),
        out_shape=jax.ShapeDtypeStruct((mp, nh * dh), BF16),
        scratch_shapes=[pltpu.VMEM((hps, tq, dh), F32), pltpu.VMEM((hps, tq, dh), F32),
                        pltpu.VMEM((hps, tq, dh), F32)],
        compiler_params=_params(3), name="fox_prompt",
    )(q, k, v, frow)


def _past_bias_body(n_pages, page, nh, *refs):
    lf_refs = refs[1:1 + n_pages]
    out = refs[1 + n_pages]
    rows, later_in_page, spread_heads = refs[2 + n_pages:]
    w = page * nh
    npp = rows.shape[0]

    @pl.when(pl.program_id(0) == 0)
    def _():
        a = lax.broadcasted_iota(jnp.int32, (w, w), 0)
        c = lax.broadcasted_iota(jnp.int32, (w, w), 1)
        later_in_page[...] = (((a % nh) == (c % nh)) & ((a // nh) > (c // nh))).astype(BF16)
        k = lax.broadcasted_iota(jnp.int32, spread_heads.shape, 0)
        c2 = lax.broadcasted_iota(jnp.int32, spread_heads.shape, 1)
        spread_heads[...] = ((k % LANES) == (c2 % nh)).astype(BF16)
        rows[...] = jnp.zeros_like(rows)

    for p in range(n_pages):
        rows[p:p + 1, :] = lf_refs[p][0]
    lf = rows[...]
    within3 = jnp.dot(jnp.concatenate(_split3(lf), axis=0), later_in_page[...], preferred_element_type=F32)
    within = within3[:npp] + within3[npp:2 * npp] + within3[2 * npp:]
    first = (within + lf)[:, :LANES]
    totals = jnp.dot(jnp.concatenate(_split3(first), axis=1), spread_heads[...], preferred_element_type=F32)
    pr = lax.broadcasted_iota(jnp.int32, (npp, npp), 0)
    pc = lax.broadcasted_iota(jnp.int32, (npp, npp), 1)
    later_pages = (pc > pr).astype(BF16)
    after3 = jnp.dot(later_pages, jnp.concatenate(_split3(totals), axis=1), preferred_element_type=F32)
    after = after3[:, :w] + after3[:, w:2 * w] + after3[:, 2 * w:]
    out[0] = (within + after)[:n_pages, :] * LOG2E


def _past_bias(cache_lf, page_table):
    n_phys, page, nh = cache_lf.shape
    nb, n_pages = page_table.shape
    w = page * nh
    np_pad = -(-n_pages // LANES) * LANES
    lf = cache_lf.reshape(n_phys, 1, w)
    in_specs = [pl.BlockSpec((1, 1, w), lambda b, pt, p=p: (pt[b, p], 0, 0)) for p in range(n_pages)]
    grid_spec = pltpu.PrefetchScalarGridSpec(
        num_scalar_prefetch=1, grid=(nb,), in_specs=in_specs,
        out_specs=pl.BlockSpec((1, n_pages, w), lambda b, pt: (b, 0, 0)),
        scratch_shapes=[pltpu.VMEM((np_pad, w), F32), pltpu.VMEM((w, w), BF16), pltpu.VMEM((3 * LANES, w), BF16)])
    return pl.pallas_call(
        functools.partial(_past_bias_body, n_pages, page, nh),
        grid_spec=grid_spec, out_shape=jax.ShapeDtypeStruct((nb, n_pages, w), F32),
        compiler_params=_params(1), name="fox_past_bias",
    )(page_table, *([lf] * n_pages))


def _fox_sample_body(pps, n_steps, nh, dec_seq, *refs):
    qall, knew, vnew, frow_new, rb = refs[1:6]
    krefs = refs[6:6 + pps]
    vrefs = refs[6 + pps:6 + 2 * pps]
    o = refs[6 + 2 * pps]
    m_s, l_s, acc_s, s_scr = refs[7 + 2 * pps:]
    step = pl.program_id(1)
    qv = qall[0]
    nrow = qv.shape[0]

    @pl.when(step == 0)
    def _():
        kn = knew[0].astype(BF16)
        g = lax.dot_general(qv, kn, NT_DIMS, preferred_element_type=F32)
        r = lax.broadcasted_iota(jnp.int32, g.shape, 0)
        c = lax.broadcasted_iota(jnp.int32, g.shape, 1)
        valid = ((c % nh) == (r // dec_seq)) & ((c // nh) <= (r % dec_seq))
        s = jnp.where(valid, g - frow_new[0], NEG_BIG)
        m = jnp.max(s, axis=-1, keepdims=True)
        p = jnp.exp2(s - m)
        m_s[...] = m
        l_s[...] = jnp.sum(p, axis=-1, keepdims=True)
        acc_s[...] = jnp.dot(p.astype(BF16), vnew[0].astype(BF16), preferred_element_type=F32)

    lanes = krefs[0].shape[1]
    r = lax.broadcasted_iota(jnp.int32, (nrow, lanes), 0)
    c = lax.broadcasted_iota(jnp.int32, (nrow, lanes), 1)
    head_ok = (c % nh) == (r // dec_seq)
    m_prev = m_s[...]
    m_new = m_prev
    for i in range(pps):
        g = lax.dot_general(qv, krefs[i][0].astype(BF16), NT_DIMS, preferred_element_type=F32)
        s = jnp.where(head_ok, g + rb[0, i:i + 1, :], NEG_BIG)
        s_scr[i] = s
        m_new = jnp.maximum(m_new, jnp.max(s, axis=-1, keepdims=True))
    alpha = jnp.exp2(m_prev - m_new)
    psum = acc = None
    for i in range(pps):
        p = jnp.exp2(s_scr[i] - m_new)
        pv = jnp.dot(p.astype(BF16), vrefs[i][0].astype(BF16), preferred_element_type=F32)
        ps = jnp.sum(p, axis=-1, keepdims=True)
        psum = ps if psum is None else psum + ps
        acc = pv if acc is None else acc + pv
    l_s[...] = alpha * l_s[...] + psum
    acc_s[...] = alpha * acc_s[...] + acc
    m_s[...] = m_new

    @pl.when(step == n_steps - 1)
    def _():
        o[0] = (acc_s[...] / l_s[...]).astype(o.dtype)


def _fox_sample(q, k, v, fn, cache_k, cache_v, cache_lf, page_table, dec_seq):
    ms = q.shape[0]
    nb = ms // dec_seq
    nh, dh = ATT_HEADS, ATT_HEAD_DIM
    n_phys, page = cache_k.shape[0], cache_k.shape[1]
    n_pages = page_table.shape[1]
    pps = 16
    while n_pages % pps:
        pps //= 2
    n_steps = n_pages // pps
    nrow = nh * dec_seq
    rbias = _past_bias(cache_lf, page_table)
    qall = jnp.transpose(q.reshape(nb, dec_seq, nh, dh), (0, 2, 1, 3)).reshape(nb, nrow, dh)
    knew = k.reshape(nb, dec_seq * nh, dh)
    vnew = v.reshape(nb, dec_seq * nh, dh)
    frow_new = fn.reshape(nb, 1, dec_seq * nh)
    ck = cache_k.reshape(n_phys, page * nh, dh)
    cv = cache_v.reshape(n_phys, page * nh, dh)
    per_b = lambda shape: pl.BlockSpec((1,) + shape, lambda b, s, pt: (b, 0, 0))
    in_specs = [per_b((nrow, dh)), per_b((dec_seq * nh, dh)), per_b((dec_seq * nh, dh)),
                per_b((1, dec_seq * nh)),
                pl.BlockSpec((1, pps, page * nh), lambda b, s, pt: (b, s, 0))]
    for _ in range(2):
        in_specs += [pl.BlockSpec((1, page * nh, dh), lambda b, s, pt, i=i: (pt[b, s * pps + i], 0, 0))
                     for i in range(pps)]
    grid_spec = pltpu.PrefetchScalarGridSpec(
        num_scalar_prefetch=1, grid=(nb, n_steps), in_specs=in_specs,
        out_specs=pl.BlockSpec((1, nrow, dh), lambda b, s, pt: (b, 0, 0)),
        scratch_shapes=[pltpu.VMEM((nrow, 1), F32), pltpu.VMEM((nrow, 1), F32), pltpu.VMEM((nrow, dh), F32),
                        pltpu.VMEM((pps, nrow, page * nh), F32)])
    o = pl.pallas_call(
        functools.partial(_fox_sample_body, pps, n_steps, nh, dec_seq),
        grid_spec=grid_spec, out_shape=jax.ShapeDtypeStruct((nb, nrow, dh), BF16),
        compiler_params=_params(2), name="fox_sample",
    )(page_table, qall, knew, vnew, frow_new, rbias, *([ck] * pps), *([cv] * pps))
    return jnp.transpose(o.reshape(nb, nh, dec_seq, dh), (0, 2, 1, 3)).reshape(ms, nh * dh)


def _ssd_body(t_in, nseq, nsub, carried, xs_r, z_r, b_r, c_r, dt_r, gc_r, sel_r, ex_r, *rest):
    if carried:
        hx_r, hb_r, hc_r, s0_r, y_o, st_o, st, carry = rest
    else:
        hx_r = hb_r = hc_r = s0_r = None
        y_o, st_o, st, carry = rest
    ci = pl.program_id(2)
    nc = pl.num_programs(2)
    L = t_in * nseq
    N = D_STATE
    gw = xs_r.shape[1]

    @pl.when(ci == 0)
    def _():
        if carried:
            for s in range(nseq):
                st[:, s * N:(s + 1) * N] = s0_r[s, 0]
        else:
            st[...] = jnp.zeros_like(st)
            carry[...] = jnp.zeros_like(carry)

    d_skip = gc_r[SSM_CONV + 1:SSM_CONV + 2, 0:gw]
    a_rate = gc_r[SSM_CONV + 1:SSM_CONV + 2, gw:gw + LANES]
    gain = gc_r[SSM_CONV + 2:SSM_CONV + 3, 0:gw]

    def conv_act(raw_r, hist_r, lo, hi):
        x = raw_r[...]
        w_r = gc_r[0:SSM_CONV, lo:hi]
        bias = gc_r[SSM_CONV:SSM_CONV + 1, lo:hi]
        if carried:
            t = lax.broadcasted_iota(jnp.int32, x.shape, 0) % t_in
            y = _conv_rows(x, hist_r[...], w_r, SSM_CONV, t)
        else:
            y = _conv_rows(x, None, w_r, SSM_CONV, None)
            head = x[:SUBLANES, :]
            t8 = lax.broadcasted_iota(jnp.int32, head.shape, 0)
            y8 = _conv_rows(head, carry[:, lo:hi], w_r, SSM_CONV, t8)
            y = jnp.concatenate([y8, y[SUBLANES:, :]], axis=0)
            carry[:, lo:hi] = pltpu.roll(x[x.shape[0] - SUBLANES:, :], SSM_CONV - 1, axis=0)
        return _silu(y + bias)

    xs_blk = conv_act(xs_r, hx_r, 0, gw)
    bmf_blk = conv_act(b_r, hb_r, gw, gw + N)
    cmf_blk = conv_act(c_r, hc_r, gw + N, gw + 2 * N)
    dt_blk = jnp.dot(jnp.concatenate(_split3(dt_r[...]), axis=1), sel_r[...], preferred_element_type=F32)
    r = lax.broadcasted_iota(jnp.int32, (L, L), 0)
    c = lax.broadcasted_iota(jnp.int32, (L, L), 1)
    for u in range(nsub):
        rows = slice(u * L, (u + 1) * L)
        _ssd_chunk(t_in, nseq, r, c, xs_blk[rows], bmf_blk[rows], cmf_blk[rows], dt_blk[rows], z_r[rows, :],
                   a_rate, d_skip, gain, ex_r, st, y_o.at[rows, :])

    @pl.when(ci == nc - 1)
    def _():
        for s in range(nseq):
            st_o[s, 0] = st[:, s * N:(s + 1) * N]


def _ssd_chunk(t_in, nseq, r, c, xs, bmf, cmf, dt, z, a_rate, d_skip, gain, ex_r, st, y_o):
    L = t_in * nseq
    P, N = SSM_HEAD_DIM, D_STATE
    R = xs.shape[1] // P
    bm, cm = bmf.astype(BF16), cmf.astype(BF16)
    da = dt * (a_rate * LOG2E)
    if nseq == 1:
        tri = r >= c
        acum = jnp.dot(tri.astype(F32), da, preferred_element_type=F32, precision=HIGHEST)
        a_tot = acum[L - 1:L, :]
    else:
        same = (r // t_in) == (c // t_in)
        tri = same & (r >= c)
        sums = jnp.dot(jnp.concatenate([tri.astype(F32), same.astype(F32)], axis=0), da,
                       preferred_element_type=F32, precision=HIGHEST)
        acum, a_tot = sums[:L], sums[L:]
    acum_t = acum.T
    dt_t = dt.T.astype(BF16)
    cols = jnp.concatenate([dt * jnp.exp2(a_tot - acum), jnp.exp2(acum)], axis=0)
    wide = jnp.dot(jnp.concatenate(_split3(cols), axis=1), ex_r[...], preferred_element_type=F32)
    w_state, w_off = wide[:L], wide[L:]
    cb = lax.dot_general(cm, bm, NT_DIMS, preferred_element_type=F32).astype(BF16)
    if nseq == 1:
        cblk, bblk = cm, bm
        decay_t = acum_t
    else:
        seq_of_row = lax.broadcasted_iota(jnp.int32, (L, N), 0) // t_in
        cblk = jnp.concatenate([jnp.where(seq_of_row == s, cmf, 0.0) for s in range(nseq)], axis=1).astype(BF16)
        bblk = jnp.concatenate([jnp.where(seq_of_row == s, bmf, 0.0) for s in range(nseq)], axis=1).astype(BF16)
        decay_t = a_tot.T
    s_prev = st[...]
    y_off = lax.dot_general(cblk, s_prev.astype(BF16), NT_DIMS, preferred_element_type=F32) * w_off
    xb = xs.astype(BF16)
    ys = []
    for h in range(R):
        decay = jnp.exp2(jnp.where(tri, acum[:, h:h + 1] - acum_t[h:h + 1, :], -jnp.inf))
        m = cb * decay.astype(BF16) * dt_t[h:h + 1, :]
        ys.append(jnp.dot(m, xb[:, h * P:(h + 1) * P], preferred_element_type=F32))
    new_states = lax.dot_general((xs * w_state).astype(BF16), bblk, TN_DIMS, preferred_element_type=F32)
    for s in range(nseq):
        last = (s + 1) * t_in - 1
        for h in range(R):
            blk = (slice(h * P, (h + 1) * P), slice(s * N, (s + 1) * N))
            st[blk] = s_prev[blk] * jnp.exp2(decay_t[h:h + 1, last:last + 1]) + new_states[blk]
    y = jnp.concatenate(ys, axis=1) + y_off + d_skip * xs
    y = y * _silu(z)
    y = y * lax.rsqrt(jnp.mean(y * y, axis=-1, keepdims=True) + EPS)
    y_o[...] = (y * gain).astype(y_o.dtype)


def _ssd(xbc, z, dt, w_conv, b_conv, a_pad, d_exp, gn_w, hist, init_state, nb, seq):
    G, N, P = SSM_GROUPS, D_STATE, SSM_HEAD_DIM
    inner = z.shape[1]
    gw = inner // G
    rg = gw // P
    L = SSM_CHUNK
    carried = init_state is not None
    if carried:
        t_in, nsub = seq, 1
        assert L % seq == 0 and nb % (L // seq) == 0, (nb, seq)
    else:
        t_in, nsub = L, 8 if seq % (8 * L) == 0 else 1
        assert seq % L == 0, seq
    nseq = L // t_in
    nbb = nb // nseq
    nbc = inner // N
    lb = L * nsub
    nc = max(seq // lb, 1)
    row = lambda b, g, c: b * nc + c

    def triple():
        return [pl.BlockSpec((lb, gw), lambda b, g, c: (row(b, g, c), g)),
                pl.BlockSpec((lb, N), lambda b, g, c: (row(b, g, c), nbc + g)),
                pl.BlockSpec((lb, N), lambda b, g, c: (row(b, g, c), nbc + G + g))]

    cw = gw + 2 * N

    def by_group(v):
        parts = [v[:, :inner].reshape(-1, G, gw), v[:, inner:inner + G * N].reshape(-1, G, N),
                 v[:, inner + G * N:].reshape(-1, G, N)]
        return jnp.transpose(jnp.concatenate(parts, axis=2), (1, 0, 2))

    zpad = lambda n: jnp.zeros((G, 1, n), F32)
    gconst = jnp.concatenate([
        by_group(w_conv.astype(F32)), by_group(b_conv.reshape(1, -1).astype(F32)),
        jnp.concatenate([d_exp.reshape(G, 1, gw), a_pad.reshape(G, 1, LANES), zpad(cw - gw - LANES)], axis=2),
        jnp.concatenate([gn_w.reshape(G, 1, gw), zpad(cw - gw)], axis=2),
        jnp.zeros((G, SUBLANES - SSM_CONV - 3, cw), F32)], axis=1)
    x_spec, b_spec, c_spec = triple()
    in_specs = [x_spec, pl.BlockSpec((lb, gw), lambda b, g, c: (row(b, g, c), g)), b_spec, c_spec,
                pl.BlockSpec((lb, LANES), lambda b, g, c: (row(b, g, c), 0)),
                pl.BlockSpec((None, SUBLANES, cw), lambda b, g, c: (g, 0, 0)),
                pl.BlockSpec((None, 3 * LANES, LANES), lambda b, g, c: (g, 0, 0)),
                pl.BlockSpec((3 * LANES, gw), lambda b, g, c: (0, 0))]
    lane_head = jnp.arange(3 * LANES) % LANES
    widen = (lane_head[:, None] == jnp.arange(gw)[None, :] // P).astype(BF16)
    lane = jnp.arange(LANES)
    pick = ((lane_head[None, :, None] == jnp.arange(G)[:, None, None] * rg + lane[None, None, :])
            & (lane[None, None, :] < rg)).astype(BF16)
    args = [xbc, z, xbc, xbc, dt, gconst, pick, widen]
    if carried:
        in_specs += triple() + [pl.BlockSpec((nseq, 1, gw, N), lambda b, g, c: (b, g, 0, 0))]
        args += [hist, hist, hist, init_state]
    y, st = pl.pallas_call(
        functools.partial(_ssd_body, t_in, nseq, nsub, carried),
        grid=(nbb, G, nc), in_specs=in_specs,
        out_specs=[pl.BlockSpec((lb, gw), lambda b, g, c: (row(b, g, c), g)),
                   pl.BlockSpec((nseq, 1, gw, N), lambda b, g, c: (b, g, 0, 0))],
        out_shape=[jax.ShapeDtypeStruct((nb * seq, inner), BF16), jax.ShapeDtypeStruct((nb, G, gw, N), F32)],
        scratch_shapes=[pltpu.VMEM((gw, nseq * N), F32), pltpu.VMEM((SUBLANES, gw + 2 * N), F32)],
        compiler_params=_params(3), name="ssd",
    )(*args)
    return y, st


def _ffn(x, norm_w, w_gate, w_up, w_down, layer):
    h = _rmsnorm_pair(x, norm_w, BF16)
    a = _mm_pair([h], [_W(w_gate, layer), _W(w_up, layer)], out_dtype=BF16, mode="swiglu", name="ffn_gate_up")
    return _mm_pair([a], [_W(w_down, layer)], res=x, name="ffn_down")


def kernel(x_prompt, x_sample, cache_k, cache_v, cache_logf, page_table, state_sconv, state_mconv, state_ssm, norm_mix, norm_ffn, norm_final, w_in_even, b_forget, w_sconv, w_out_even, w_in_ssm, w_mconv, b_mconv, dt_bias, a_log, d_skip, w_gnorm, w_out_ssm, w_gate, w_up, w_down):
    bp, seq, d = x_prompt.shape
    bs, dec_seq, _ = x_sample.shape
    mp, ms = bp * seq, bs * dec_seq
    nh, dh = ATT_HEADS, ATT_HEAD_DIM
    att = nh * dh
    sc_ch = state_sconv.shape[-1]
    x = (x_prompt.reshape(mp, d), x_sample.reshape(ms, d))

    i = 0
    h = _rmsnorm_pair(x, norm_mix[0], BF16)
    c0 = 3 * sc_ch
    w_in_t = jnp.swapaxes(w_in_even, 1, 2)
    win = lambda col0, n: _W(w_in_t, i, col0=col0, n=n, transposed=True)
    gcu = _mm_pair([h], [win(0, c0)], name="in_even_conv")
    q = _mm_pair([h], [win(c0, att)], out_dtype=BF16, scale=dh ** -0.5 * LOG2E, name="in_even_q")
    k = _mm_pair([h], [win(c0 + att, att)], name="in_even_k")
    v = _mm_pair([h], [win(c0 + 2 * att, att)], name="in_even_v")
    lf_p, fc_p, lf_s, fc_s = _forget_pair(h, w_in_even[i, :, c0 + 3 * att:], b_forget[i], bp, seq, dec_seq)
    logf_p = lf_p[:, :nh].reshape(bp, seq, nh)
    logf_s = lf_s[:, :nh].reshape(bs, dec_seq, nh)
    y_a, sconv_p, sconv_s = _sconv_pair(gcu, state_sconv[i], w_sconv[i], bp, seq, dec_seq)
    yb_p = _fox_prompt(q[0], k[0], v[0], fc_p[:, :nh].reshape(bp, seq, nh), bp, seq)
    yb_s = _fox_sample(q[1], k[1], v[1], fc_s[:, :nh].reshape(bs, dec_seq, nh),
                       cache_k[i], cache_v[i], cache_logf[i], page_table, dec_seq)
    x = _mm_pair([y_a, (yb_p, yb_s)],
                 [_W(w_out_even, i, row0=0, k=sc_ch), _W(w_out_even, i, row0=sc_ch, k=att)], res=x, name="out_even")
    x = _ffn(x, norm_ffn[0], w_gate, w_up, w_down, 0)

    j = 0
    G = SSM_GROUPS
    n_heads = a_log.shape[1]
    rg = n_heads // G
    inner = n_heads * SSM_HEAD_DIM
    conv_dim = state_mconv.shape[-1]
    h = _rmsnorm_pair(x, norm_mix[1], BF16)
    w_ssm_t = jnp.swapaxes(w_in_ssm, 1, 2)
    z = _mm_pair([h], [_W(w_ssm_t, j, n=inner, transposed=True)], name="in_ssm_z")
    xbc = _mm_pair([h], [_W(w_ssm_t, j, col0=inner, n=conv_dim, transposed=True)], name="in_ssm_xbc")
    assert n_heads <= LANES
    w_dt = jnp.zeros((d, LANES), BF16).at[:, :n_heads].set(w_in_ssm[j, :, inner + conv_dim:].astype(BF16))
    b_dt = jnp.zeros((1, LANES), F32).at[0, :n_heads].set(dt_bias[j].astype(F32))
    dt = _mm_pair([h], [_W(w_dt)], mode="softplus", bias=b_dt, name="in_ssm_dt")
    a_pad = jnp.zeros((G, LANES), F32).at[:, :rg].set(-jnp.exp(a_log[j].astype(F32)).reshape(G, rg)).reshape(1, G * LANES)
    d_exp = jnp.repeat(d_skip[j].astype(F32), SSM_HEAD_DIM).reshape(1, inner)
    gn_w = w_gnorm[j].astype(F32).reshape(1, inner)
    hw = SSM_CONV - 1
    mconv_p = xbc[0].reshape(bp, seq, conv_dim)[:, seq - hw:, :]
    mconv_s = xbc[1].reshape(bs, dec_seq, conv_dim)[:, dec_seq - hw:, :]
    yp, ssm_p = _ssd(xbc[0], z[0], dt[0], w_mconv[j], b_mconv[j], a_pad, d_exp, gn_w, None, None, bp, seq)
    init = state_ssm[j].astype(F32).reshape(bs, G, rg * SSM_HEAD_DIM, D_STATE)
    ysm, ssm_s = _ssd(xbc[1], z[1], dt[1], w_mconv[j], b_mconv[j], a_pad, d_exp, gn_w,
                      _pad_hist(state_mconv[j], dec_seq), init, bs, dec_seq)
    x = _mm_pair([(yp, ysm)], [_W(w_out_ssm, j)], res=x, name="out_ssm")
    x = _ffn(x, norm_ffn[1], w_gate, w_up, w_down, 1)

    y = _rmsnorm_pair(x, norm_final, F32)
    st_shape = (n_heads, SSM_HEAD_DIM, D_STATE)
    return (y[0].reshape(bp, seq, d), y[1].reshape(bs, dec_seq, d),
            k[0].reshape(1, bp, seq, nh, dh), v[0].reshape(1, bp, seq, nh, dh), logf_p[None],
            sconv_p[None], mconv_p[None], ssm_p.reshape((1, bp) + st_shape),
            k[1].reshape(1, bs, dec_seq, nh, dh), v[1].reshape(1, bs, dec_seq, nh, dh), logf_s[None],
            sconv_s[None], mconv_s[None], ssm_s.reshape((1, bs) + st_shape))
```

```python
import functools

import jax
import jax.numpy as jnp
from jax import lax
from jax.experimental import pallas as pl
from jax.experimental.pallas import tpu as pltpu

F32 = jnp.float32
BF16 = jnp.bfloat16

SC_WIDTH = 3
ATT_HEADS = 8
ATT_HEAD_DIM = 128
SSM_HEAD_DIM = 64
SSM_GROUPS = 8
D_STATE = 128
SSM_CONV = 4
SSM_CHUNK = 128
EPS = 1e-6

LANES = 128
SUBLANES = 8
VMEM_LIMIT_BYTES = 56 * 1024 * 1024
NEG_BIG = -1e30
LOG2E = 1.4426950408889634

HIGHEST = lax.Precision.HIGHEST
NT_DIMS = (((1,), (1,)), ((), ()))
TN_DIMS = (((0,), (0,)), ((), ()))


def _params(n_grid_dims):
    return pltpu.CompilerParams(dimension_semantics=("arbitrary",) * n_grid_dims,
                                vmem_limit_bytes=VMEM_LIMIT_BYTES)


def _row_tile(mp, pref):
    t = pref
    while mp % t:
        t //= 2
    assert t >= SUBLANES, (mp, pref)
    return t


def _rmsnorm_body(nt, xp, xs, w, op, os):
    i = pl.program_id(0)

    def norm(x):
        y = x * lax.rsqrt(jnp.mean(x * x, axis=-1, keepdims=True) + EPS)
        return (y * w[...]).astype(op.dtype)

    @pl.when(i < nt)
    def _():
        op[...] = norm(xp[...])

    @pl.when(i == nt)
    def _():
        os[...] = norm(xs[...])


def _rmsnorm_pair(x, w, out_dtype):
    xp, xs = x
    mp, d = xp.shape
    ms = xs.shape[0]
    tm = _row_tile(mp, 1024)
    nt = mp // tm
    pidx = lambda i: (jnp.minimum(i, nt - 1), 0)
    return pl.pallas_call(
        functools.partial(_rmsnorm_body, nt),
        grid=(nt + 1,),
        in_specs=[pl.BlockSpec((tm, d), pidx), pl.BlockSpec((ms, d), lambda i: (0, 0)),
                  pl.BlockSpec((1, d), lambda i: (0, 0))],
        out_specs=[pl.BlockSpec((tm, d), pidx), pl.BlockSpec((ms, d), lambda i: (0, 0))],
        out_shape=[jax.ShapeDtypeStruct((mp, d), out_dtype), jax.ShapeDtypeStruct((ms, d), out_dtype)],
        compiler_params=_params(1), name="rmsnorm",
    )(xp, xs, w.reshape(1, d).astype(F32))


def _softplus(x):
    return jnp.maximum(x, 0.0) + jnp.log1p(jnp.exp(-jnp.abs(x)))


def _silu(x):
    return x * (1.0 / (1.0 + jnp.exp(-x)))


def _split3(v):
    hi = v.astype(BF16)
    r1 = v - hi.astype(F32)
    mid = r1.astype(BF16)
    lo = (r1 - mid.astype(F32)).astype(BF16)
    return hi, mid, lo


class _W:
    def __init__(self, arr, layer=None, row0=0, k=None, col0=0, n=None, transposed=False):
        self.arr, self.layer, self.row0, self.col0, self.transposed = arr, layer, row0, col0, transposed
        kdim, ndim = (-1, -2) if transposed else (-2, -1)
        self.k = arr.shape[kdim] if k is None else k
        self.n = arr.shape[ndim] if n is None else n

    def block(self, tn):
        return (tn, self.k) if self.transposed else (self.k, tn)

    def spec(self, tn):
        assert self.row0 % self.k == 0 and self.col0 % tn == 0
        rb, cb = self.row0 // self.k, self.col0 // tn
        if self.transposed:
            idx = lambda j, i: (cb + j, rb)
        else:
            idx = lambda j, i: (rb, cb + j)
        if self.layer is None:
            return pl.BlockSpec(self.block(tn), idx)
        layer = self.layer
        return pl.BlockSpec((None,) + self.block(tn), lambda j, i: (layer,) + idx(j, i))


def _mm_body(n_lhs, mode, has_res, has_bias, scale, w_is_f32, w_transposed, emit_norm, norm_in, *refs):
    j = pl.program_id(0)
    i = pl.program_id(1)

    def mm(x, idx):
        dims = NT_DIMS if w_transposed[idx] else (((1,), (0,)), ((), ()))
        return lax.dot_general(x, wrefs[idx][...], dims, preferred_element_type=F32)

    xrefs = refs[:2 * n_lhs]
    n_w = len(w_is_f32)
    wrefs = list(refs[2 * n_lhs:2 * n_lhs + n_w])
    pos = 2 * n_lhs + n_w
    rp = rs = bias = None
    if has_res:
        rp, rs = refs[pos], refs[pos + 1]
        pos += 2
    if has_bias:
        bias = refs[pos]
        pos += 1
    inv_in = gain = None
    if norm_in:
        inv_in, gain = (refs[pos], refs[pos + 1]), refs[pos + 2]
        pos += 3
    op, os = refs[pos], refs[pos + 1]
    pos += 2
    xb_out = inv_out = None
    if emit_norm:
        xb_out, inv_out = (refs[pos], refs[pos + 1]), (refs[pos + 2], refs[pos + 3])
        pos += 4
    scratch = list(refs[pos:])

    for idx in range(n_w):
        if w_is_f32[idx]:
            wsrc, wdst = wrefs[idx], scratch.pop(0)

            @pl.when(i == 0)
            def _(wsrc=wsrc, wdst=wdst):
                w = wsrc[...]
                if norm_in:
                    w = w * gain[...]
                wdst[...] = w.astype(BF16)

            wrefs[idx] = wdst

    def compute(which, out, res):
        inv = None
        if norm_in:
            inv = jnp.tile(inv_in[which][...], (1, out.shape[1] // LANES))
        if mode == "swiglu":
            x = xrefs[which][...]
            g, u = mm(x, 0), mm(x, 1)
            if norm_in:
                g, u = g * inv, u * inv
            acc = _silu(g) * u
        else:
            acc = mm(xrefs[which][...], 0)
            for k in range(1, n_lhs):
                acc += mm(xrefs[2 * k + which][...], k)
            if norm_in:
                acc = acc * inv
        if scale is not None:
            acc = acc * scale
        if has_bias:
            acc = acc + bias[...]
        if mode == "softplus":
            acc = _softplus(acc)
        if has_res:
            acc = res[...] + acc
        out[...] = acc.astype(out.dtype)
        if emit_norm:
            xb_out[which][...] = acc.astype(BF16)
            ssq = scratch[1] if which == 1 else scratch[0].at[i - 1]
            ss = jnp.sum(acc * acc, axis=-1, keepdims=True)

            @pl.when(j == 0)
            def _():
                ssq[...] = jnp.broadcast_to(ss, ssq.shape)

            @pl.when(j > 0)
            def _():
                ssq[...] = ssq[...] + ss

            inv_out[which][...] = lax.rsqrt(ssq[...] * (1.0 / emit_norm) + EPS)

    @pl.when(i == 0)
    def _():
        compute(1, os, rs)

    @pl.when(i > 0)
    def _():
        compute(0, op, rp)


def _mm_pair(xs_list, ws_list, *, out_dtype=F32, mode="plain", res=None, bias=None, scale=None, name="matmul",
             emit_norm=False, norm_in=None):
    n_lhs = len(xs_list)
    mp = xs_list[0][0].shape[0]
    ms = xs_list[0][1].shape[0]
    n = ws_list[0].n
    kmax = max(w.k for w in ws_list)
    tm = _row_tile(mp, 512 if kmax > 4096 or (emit_norm and kmax > 2048) else 1024)
    nt = mp // tm
    if n % 1024 == 0 and kmax <= 2048 and mode != "swiglu" and not emit_norm:
        tn = 1024
    elif n % 512 == 0:
        tn = 512
    else:
        assert n % LANES == 0 and n <= 512, n
        tn = n
    pidx_x = lambda j, i: (jnp.maximum(i - 1, 0), 0)
    pidx_o = lambda j, i: (jnp.maximum(i - 1, 0), j)
    in_specs, args = [], []
    for xp, xs in xs_list:
        kk = xp.shape[1]
        in_specs += [pl.BlockSpec((tm, kk), pidx_x), pl.BlockSpec((ms, kk), lambda j, i: (0, 0))]
        args += [xp, xs]
    w_is_f32, scratch = [], []
    for w in ws_list:
        in_specs.append(w.spec(tn))
        args.append(w.arr)
        w_is_f32.append(w.arr.dtype != BF16)
        if w_is_f32[-1]:
            scratch.append(pltpu.VMEM(w.block(tn), BF16))
    if res is not None:
        in_specs += [pl.BlockSpec((tm, tn), pidx_o), pl.BlockSpec((ms, tn), lambda j, i: (0, j))]
        args += [res[0], res[1]]
    if bias is not None:
        in_specs.append(pl.BlockSpec((1, tn), lambda j, i: (0, j)))
        args.append(bias.reshape(1, n).astype(F32))
    if norm_in is not None:
        (inv_p, inv_s), gain = norm_in
        last = inv_p.shape[0] - 1
        assert not any(w.transposed for w in ws_list) and all(w.k == gain.shape[0] for w in ws_list)
        in_specs += [pl.BlockSpec((None, tm, LANES), lambda j, i: (last, jnp.maximum(i - 1, 0), 0)),
                     pl.BlockSpec((None, ms, LANES), lambda j, i: (last, 0, 0)),
                     pl.BlockSpec((gain.shape[0], 1), lambda j, i: (0, 0))]
        args += [inv_p, inv_s, gain.reshape(-1, 1).astype(F32)]
    out_specs = [pl.BlockSpec((tm, tn), pidx_o), pl.BlockSpec((ms, tn), lambda j, i: (0, j))]
    out_shape = [jax.ShapeDtypeStruct((mp, n), out_dtype), jax.ShapeDtypeStruct((ms, n), out_dtype)]
    if emit_norm:
        nj = n // tn
        out_specs += [pl.BlockSpec((tm, tn), pidx_o), pl.BlockSpec((ms, tn), lambda j, i: (0, j)),
                      pl.BlockSpec((None, tm, LANES), lambda j, i: (j, jnp.maximum(i - 1, 0), 0)),
                      pl.BlockSpec((None, ms, LANES), lambda j, i: (j, 0, 0))]
        out_shape += [jax.ShapeDtypeStruct((mp, n), BF16), jax.ShapeDtypeStruct((ms, n), BF16),
                      jax.ShapeDtypeStruct((nj, mp, LANES), F32), jax.ShapeDtypeStruct((nj, ms, LANES), F32)]
        scratch += [pltpu.VMEM((nt, tm, LANES), F32), pltpu.VMEM((ms, LANES), F32)]
    outs = pl.pallas_call(
        functools.partial(_mm_body, n_lhs, mode, res is not None, bias is not None, scale, tuple(w_is_f32),
                          tuple(w.transposed for w in ws_list), n if emit_norm else 0, norm_in is not None),
        grid=(n // tn, nt + 1),
        in_specs=in_specs, out_specs=out_specs, out_shape=out_shape,
        scratch_shapes=scratch,
        compiler_params=_params(2), name=name,
    )(*args)
    if emit_norm:
        return (outs[0], outs[1]), ((outs[2], outs[3]), (outs[4], outs[5]))
    return outs[0], outs[1]


def _log_sigmoid(x):
    return -_softplus(-x)


def _forget_body(nb, seq, dec_seq, ck, xp, xs, w, b, lp, fp, ls, fs):
    i = pl.program_id(0)

    @pl.when(i < nb)
    def _():
        r = lax.broadcasted_iota(jnp.int32, (ck, ck), 0)
        c = lax.broadcasted_iota(jnp.int32, (ck, ck), 1)
        tri = (r >= c).astype(F32)
        carry = jnp.zeros((1, LANES), F32)
        for t in range(seq // ck):
            f = jnp.dot(xp[t * ck:(t + 1) * ck, :], w[...], preferred_element_type=F32) + b[...]
            lf = _log_sigmoid(f)
            lp[t * ck:(t + 1) * ck, :] = lf
            cs = jnp.dot(tri, lf, preferred_element_type=F32, precision=HIGHEST) + carry
            fp[t * ck:(t + 1) * ck, :] = cs * LOG2E
            carry = cs[ck - 1:ck, :]

    @pl.when(i == nb)
    def _():
        ms = xs.shape[0]
        r = lax.broadcasted_iota(jnp.int32, (ms, ms), 0)
        c = lax.broadcasted_iota(jnp.int32, (ms, ms), 1)
        tri = ((r // dec_seq == c // dec_seq) & (r >= c)).astype(F32)
        f = jnp.dot(xs[...], w[...], preferred_element_type=F32) + b[...]
        lf = _log_sigmoid(f)
        ls[...] = lf
        fs[...] = jnp.dot(tri, lf, preferred_element_type=F32, precision=HIGHEST) * LOG2E


def _forget_pair(h, w_f, b_f, nb, seq, dec_seq):
    hp, hs = h
    mp, d = hp.shape
    ms = hs.shape[0]
    assert ms <= 512 and ms % SUBLANES == 0
    ck = _row_tile(seq, 256)
    n_h = w_f.shape[1]
    wpad = jnp.zeros((d, LANES), BF16).at[:, :n_h].set(w_f.astype(BF16))
    bpad = jnp.zeros((1, LANES), F32).at[0, :n_h].set(b_f.astype(F32))
    pidx = lambda i: (jnp.minimum(i, nb - 1), 0)
    cidx = lambda i: (0, 0)
    return pl.pallas_call(
        functools.partial(_forget_body, nb, seq, dec_seq, ck),
        grid=(nb + 1,),
        in_specs=[pl.BlockSpec((seq, d), pidx), pl.BlockSpec((ms, d), cidx),
                  pl.BlockSpec((d, LANES), cidx), pl.BlockSpec((1, LANES), cidx)],
        out_specs=[pl.BlockSpec((seq, LANES), pidx), pl.BlockSpec((seq, LANES), pidx),
                   pl.BlockSpec((ms, LANES), cidx), pl.BlockSpec((ms, LANES), cidx)],
        out_shape=[jax.ShapeDtypeStruct((mp, LANES), F32), jax.ShapeDtypeStruct((mp, LANES), F32),
                   jax.ShapeDtypeStruct((ms, LANES), F32), jax.ShapeDtypeStruct((ms, LANES), F32)],
        compiler_params=_params(1), name="forget_gates",
    )(hp, hs, wpad, bpad)


def _shifted(x, hist, k, width, t_in_seq):
    n = x.shape[0]
    xk = pltpu.roll(x, k, axis=0)
    if hist is None:
        return xk
    back = (width - 1 - k) % n
    hk = hist if back == 0 else pltpu.roll(hist, n - back, axis=0)
    return jnp.where(t_in_seq >= k, xk, hk)


def _conv_rows(x, hist, wref, width, t_in_seq):
    acc = x * wref[width - 1:width, :]
    for k in range(1, width):
        acc += _shifted(x, hist, k, width, t_in_seq) * wref[width - 1 - k:width - k, :]
    return acc


def _sconv_body(nt, tiles_per_seq, width, dec_seq, gbp, gcp, up, gbs, gcs, us, hs, w, yp, ys, zp_tail, zs, carry):
    i = pl.program_id(1)

    @pl.when(i < nt)
    def _():
        t0 = i % tiles_per_seq
        x = gcp[...] * up[...]
        ts = x.shape[0]

        @pl.when(t0 == 0)
        def _():
            carry[...] = jnp.zeros_like(carry)

        yp[...] = (gbp[...] * _conv_rows(x, None, w, width, None)).astype(yp.dtype)
        head = x[:SUBLANES, :]
        t8 = lax.broadcasted_iota(jnp.int32, head.shape, 0)
        yp[:SUBLANES, :] = (gbp[:SUBLANES, :] * _conv_rows(head, carry[...], w, width, t8)).astype(yp.dtype)
        tail = x[ts - SUBLANES:, :]
        carry[...] = pltpu.roll(tail, width - 1, axis=0)
        zp_tail[0] = tail

    @pl.when(i == nt)
    def _():
        x = gcs[...] * us[...]
        t = lax.broadcasted_iota(jnp.int32, x.shape, 0) % dec_seq
        ys[...] = (gbs[...] * _conv_rows(x, hs[...], w, width, t)).astype(ys.dtype)
        zs[...] = x


def _pad_hist(state, dec_seq):
    b, hw, c = state.shape
    assert dec_seq >= hw
    return jnp.concatenate([state.astype(F32), jnp.zeros((b, dec_seq - hw, c), F32)], axis=1).reshape(b * dec_seq, c)


def _sconv_pair(gcu, state, w, nb, seq, dec_seq):
    gp, gs = gcu
    mp = gp.shape[0]
    ms = gs.shape[0]
    ch = gp.shape[1] // 3
    tc = 1024 if ch % 1024 == 0 else 512
    ncb = ch // tc
    ts = _row_tile(seq, 512)
    tps = seq // ts
    nt = mp // ts
    pr = lambda j, i: jnp.minimum(i, nt - 1)
    in_specs = []
    for part in range(3):
        in_specs.append(pl.BlockSpec((ts, tc), lambda j, i, part=part: (pr(j, i), part * ncb + j)))
    for part in range(3):
        in_specs.append(pl.BlockSpec((ms, tc), lambda j, i, part=part: (0, part * ncb + j)))
    in_specs += [pl.BlockSpec((ms, tc), lambda j, i: (0, j)), pl.BlockSpec((SC_WIDTH, tc), lambda j, i: (0, j))]
    out_specs = [pl.BlockSpec((ts, tc), lambda j, i: (pr(j, i), j)),
                 pl.BlockSpec((ms, tc), lambda j, i: (0, j)),
                 pl.BlockSpec((1, SUBLANES, tc), lambda j, i: (pr(j, i) // tps, 0, j)),
                 pl.BlockSpec((ms, tc), lambda j, i: (0, j))]
    out_shape = [jax.ShapeDtypeStruct((mp, ch), BF16), jax.ShapeDtypeStruct((ms, ch), BF16),
                 jax.ShapeDtypeStruct((nb, SUBLANES, ch), F32), jax.ShapeDtypeStruct((ms, ch), F32)]
    yp, ys, zp_tail, zs = pl.pallas_call(
        functools.partial(_sconv_body, nt, tps, SC_WIDTH, dec_seq),
        grid=(ncb, nt + 1), in_specs=in_specs, out_specs=out_specs, out_shape=out_shape,
        scratch_shapes=[pltpu.VMEM((SUBLANES, tc), F32)],
        compiler_params=_params(2), name="short_conv",
    )(gp, gp, gp, gs, gs, gs, _pad_hist(state, dec_seq), w.astype(F32))
    hw = SC_WIDTH - 1
    new_p = zp_tail[:, SUBLANES - hw:, :]
    new_s = zs.reshape(ms // dec_seq, dec_seq, ch)[:, dec_seq - hw:, :]
    return (yp, ys), new_p, new_s


def _fox_prompt_body(tq, hps, q, k, v, fr, o, m_s, l_s, acc_s):
    qi = pl.program_id(2)
    dh = ATT_HEAD_DIM
    m_s[...] = jnp.full_like(m_s, NEG_BIG)
    l_s[...] = jnp.zeros_like(l_s)
    acc_s[...] = jnp.zeros_like(acc_s)

    def step(ki, masked):
        start = pl.multiple_of(ki * tq, tq)
        for hh in range(hps):
            cols = slice(hh * dh, (hh + 1) * dh)
            kb = k[pl.ds(start, tq), cols].astype(BF16)
            vb = v[pl.ds(start, tq), cols].astype(BF16)
            s = lax.dot_general(q[:, cols], kb, NT_DIMS, preferred_element_type=F32) - fr[0, hh, ki]
            if masked:
                r = lax.broadcasted_iota(jnp.int32, s.shape, 0)
                c = lax.broadcasted_iota(jnp.int32, s.shape, 1)
                s = jnp.where(r >= c, s, NEG_BIG)
            m_prev = m_s[hh]
            m_new = jnp.maximum(m_prev, jnp.max(s, axis=-1, keepdims=True))
            p = jnp.exp2(s - jnp.tile(m_new, (1, tq // dh)))
            alpha = jnp.exp2(m_prev - m_new)
            l_s[hh] = alpha * l_s[hh] + jnp.sum(p, axis=-1, keepdims=True)
            acc_s[hh] = alpha * acc_s[hh] + jnp.dot(p.astype(BF16), vb, preferred_element_type=F32)
            m_s[hh] = m_new

    def loop_body(ki, carry):
        step(ki, False)
        return carry

    lax.fori_loop(0, qi, loop_body, 0)
    step(qi, True)
    for hh in range(hps):
        o[:, hh * dh:(hh + 1) * dh] = (acc_s[hh] / l_s[hh]).astype(o.dtype)


def _fox_prompt(q, k, v, fcum, nb, seq):
    mp = q.shape[0]
    dh = ATT_HEAD_DIM
    nh = q.shape[1] // dh
    tq = _row_tile(seq, 512)
    nq = seq // tq
    frow = jnp.transpose(fcum, (0, 2, 1)).reshape(nb, nh, nq, 1, tq)
    hps = 4 if nh % 4 == 0 else 1
    gw = hps * dh
    return pl.pallas_call(
        functools.partial(_fox_prompt_body, tq, hps),
        grid=(nb, nh // hps, nq),
        in_specs=[pl.BlockSpec((tq, gw), lambda b, h, i: (b * nq + i, h)),
                  pl.BlockSpec((seq, gw), lambda b, h, i: (b, h)),
                  pl.BlockSpec((seq, gw), lambda b, h, i: (b, h)),
                  pl.BlockSpec((1, hps, nq, 1, tq), lambda b, h, i: (b, h, 0, 0, 0))],
        out_specs=pl.BlockSpec((tq, gw), lambda b, h, i: (b * nq + i, h)),
        out_shape=jax.ShapeDtypeStruct((mp, nh * dh), BF16),
        scratch_shapes=[pltpu.VMEM((hps, tq, dh), F32), pltpu.VMEM((hps, tq, dh), F32),
                        pltpu.VMEM((hps, tq, dh), F32)],
        compiler_params=_params(3), name="fox_prompt",
    )(q, k, v, frow)


def _past_bias_body(n_pages, page, nh, *refs):
    lf_refs = refs[1:1 + n_pages]
    out = refs[1 + n_pages]
    rows, later_in_page, spread_heads = refs[2 + n_pages:]
    w = page * nh
    npp = rows.shape[0]

    @pl.when(pl.program_id(0) == 0)
    def _():
        a = lax.broadcasted_iota(jnp.int32, (w, w), 0)
        c = lax.broadcasted_iota(jnp.int32, (w, w), 1)
        later_in_page[...] = (((a % nh) == (c % nh)) & ((a // nh) > (c // nh))).astype(BF16)
        k = lax.broadcasted_iota(jnp.int32, spread_heads.shape, 0)
        c2 = lax.broadcasted_iota(jnp.int32, spread_heads.shape, 1)
        spread_heads[...] = ((k % LANES) == (c2 % nh)).astype(BF16)
        rows[...] = jnp.zeros_like(rows)

    for p in range(n_pages):
        rows[p:p + 1, :] = lf_refs[p][0]
    lf = rows[...]
    within3 = jnp.dot(jnp.concatenate(_split3(lf), axis=0), later_in_page[...], preferred_element_type=F32)
    within = within3[:npp] + within3[npp:2 * npp] + within3[2 * npp:]
    first = (within + lf)[:, :LANES]
    totals = jnp.dot(jnp.concatenate(_split3(first), axis=1), spread_heads[...], preferred_element_type=F32)
    pr = lax.broadcasted_iota(jnp.int32, (npp, npp), 0)
    pc = lax.broadcasted_iota(jnp.int32, (npp, npp), 1)
    later_pages = (pc > pr).astype(BF16)
    after3 = jnp.dot(later_pages, jnp.concatenate(_split3(totals), axis=1), preferred_element_type=F32)
    after = after3[:, :w] + after3[:, w:2 * w] + after3[:, 2 * w:]
    out[0] = (within + after)[:n_pages, :] * LOG2E


def _past_bias(cache_lf, page_table):
    n_phys, page, nh = cache_lf.shape
    nb, n_pages = page_table.shape
    w = page * nh
    np_pad = -(-n_pages // LANES) * LANES
    lf = cache_lf.reshape(n_phys, 1, w)
    in_specs = [pl.BlockSpec((1, 1, w), lambda b, pt, p=p: (pt[b, p], 0, 0)) for p in range(n_pages)]
    grid_spec = pltpu.PrefetchScalarGridSpec(
        num_scalar_prefetch=1, grid=(nb,), in_specs=in_specs,
        out_specs=pl.BlockSpec((1, n_pages, w), lambda b, pt: (b, 0, 0)),
        scratch_shapes=[pltpu.VMEM((np_pad, w), F32), pltpu.VMEM((w, w), BF16), pltpu.VMEM((3 * LANES, w), BF16)])
    return pl.pallas_call(
        functools.partial(_past_bias_body, n_pages, page, nh),
        grid_spec=grid_spec, out_shape=jax.ShapeDtypeStruct((nb, n_pages, w), F32),
        compiler_params=_params(1), name="fox_past_bias",
    )(page_table, *([lf] * n_pages))


def _fox_sample_body(pps, n_steps, nh, dec_seq, *refs):
    qall, knew, vnew, frow_new, rb = refs[1:6]
    krefs = refs[6:6 + pps]
    vrefs = refs[6 + pps:6 + 2 * pps]
    o = refs[6 + 2 * pps]
    m_s, l_s, acc_s, s_scr = refs[7 + 2 * pps:]
    step = pl.program_id(1)
    qv = qall[0]
    nrow = qv.shape[0]

    @pl.when(step == 0)
    def _():
        kn = knew[0].astype(BF16)
        g = lax.dot_general(qv, kn, NT_DIMS, preferred_element_type=F32)
        r = lax.broadcasted_iota(jnp.int32, g.shape, 0)
        c = lax.broadcasted_iota(jnp.int32, g.shape, 1)
        valid = ((c % nh) == (r // dec_seq)) & ((c // nh) <= (r % dec_seq))
        s = jnp.where(valid, g - frow_new[0], NEG_BIG)
        m = jnp.max(s, axis=-1, keepdims=True)
        p = jnp.exp2(s - m)
        m_s[...] = m
        l_s[...] = jnp.sum(p, axis=-1, keepdims=True)
        acc_s[...] = jnp.dot(p.astype(BF16), vnew[0].astype(BF16), preferred_element_type=F32)

    lanes = krefs[0].shape[1]
    r = lax.broadcasted_iota(jnp.int32, (nrow, lanes), 0)
    c = lax.broadcasted_iota(jnp.int32, (nrow, lanes), 1)
    head_ok = (c % nh) == (r // dec_seq)
    m_prev = m_s[...]
    m_new = m_prev
    for i in range(pps):
        g = lax.dot_general(qv, krefs[i][0].astype(BF16), NT_DIMS, preferred_element_type=F32)
        s = jnp.where(head_ok, g + rb[0, i:i + 1, :], NEG_BIG)
        s_scr[i] = s
        m_new = jnp.maximum(m_new, jnp.max(s, axis=-1, keepdims=True))
    alpha = jnp.exp2(m_prev - m_new)
    psum = acc = None
    for i in range(pps):
        p = jnp.exp2(s_scr[i] - m_new)
        pv = jnp.dot(p.astype(BF16), vrefs[i][0].astype(BF16), preferred_element_type=F32)
        ps = jnp.sum(p, axis=-1, keepdims=True)
        psum = ps if psum is None else psum + ps
        acc = pv if acc is None else acc + pv
    l_s[...] = alpha * l_s[...] + psum
    acc_s[...] = alpha * acc_s[...] + acc
    m_s[...] = m_new

    @pl.when(step == n_steps - 1)
    def _():
        o[0] = (acc_s[...] / l_s[...]).astype(o.dtype)


def _fox_sample(q, k, v, fn, cache_k, cache_v, cache_lf, page_table, dec_seq):
    ms = q.shape[0]
    nb = ms // dec_seq
    nh, dh = ATT_HEADS, ATT_HEAD_DIM
    n_phys, page = cache_k.shape[0], cache_k.shape[1]
    n_pages = page_table.shape[1]
    pps = 16
    while n_pages % pps:
        pps //= 2
    n_steps = n_pages // pps
    nrow = nh * dec_seq
    rbias = _past_bias(cache_lf, page_table)
    qall = jnp.transpose(q.reshape(nb, dec_seq, nh, dh), (0, 2, 1, 3)).reshape(nb, nrow, dh)
    knew = k.reshape(nb, dec_seq * nh, dh)
    vnew = v.reshape(nb, dec_seq * nh, dh)
    frow_new = fn.reshape(nb, 1, dec_seq * nh)
    ck = cache_k.reshape(n_phys, page * nh, dh)
    cv = cache_v.reshape(n_phys, page * nh, dh)
    per_b = lambda shape: pl.BlockSpec((1,) + shape, lambda b, s, pt: (b, 0, 0))
    in_specs = [per_b((nrow, dh)), per_b((dec_seq * nh, dh)), per_b((dec_seq * nh, dh)),
                per_b((1, dec_seq * nh)),
                pl.BlockSpec((1, pps, page * nh), lambda b, s, pt: (b, s, 0))]
    for _ in range(2):
        in_specs += [pl.BlockSpec((1, page * nh, dh), lambda b, s, pt, i=i: (pt[b, s * pps + i], 0, 0))
                     for i in range(pps)]
    grid_spec = pltpu.PrefetchScalarGridSpec(
        num_scalar_prefetch=1, grid=(nb, n_steps), in_specs=in_specs,
        out_specs=pl.BlockSpec((1, nrow, dh), lambda b, s, pt: (b, 0, 0)),
        scratch_shapes=[pltpu.VMEM((nrow, 1), F32), pltpu.VMEM((nrow, 1), F32), pltpu.VMEM((nrow, dh), F32),
                        pltpu.VMEM((pps, nrow, page * nh), F32)])
    o = pl.pallas_call(
        functools.partial(_fox_sample_body, pps, n_steps, nh, dec_seq),
        grid_spec=grid_spec, out_shape=jax.ShapeDtypeStruct((nb, nrow, dh), BF16),
        compiler_params=_params(2), name="fox_sample",
    )(page_table, qall, knew, vnew, frow_new, rbias, *([ck] * pps), *([cv] * pps))
    return jnp.transpose(o.reshape(nb, nh, dec_seq, dh), (0, 2, 1, 3)).reshape(ms, nh * dh)


def _ssd_body(t_in, nseq, nsub, carried, xs_r, z_r, b_r, c_r, dt_r, gc_r, sel_r, ex_r, *rest):
    if carried:
        hx_r, hb_r, hc_r, s0_r, y_o, st_o, st, carry = rest
    else:
        hx_r = hb_r = hc_r = s0_r = None
        y_o, st_o, st, carry = rest
    ci = pl.program_id(2)
    nc = pl.num_programs(2)
    L = t_in * nseq
    N = D_STATE
    gw = xs_r.shape[1]

    @pl.when(ci == 0)
    def _():
        if carried:
            for s in range(nseq):
                st[:, s * N:(s + 1) * N] = s0_r[s, 0]
        else:
            st[...] = jnp.zeros_like(st)
            carry[...] = jnp.zeros_like(carry)

    d_skip = gc_r[SSM_CONV + 1:SSM_CONV + 2, 0:gw]
    a_rate = gc_r[SSM_CONV + 1:SSM_CONV + 2, gw:gw + LANES]
    gain = gc_r[SSM_CONV + 2:SSM_CONV + 3, 0:gw]

    def conv_act(raw_r, hist_r, lo, hi):
        x = raw_r[...]
        w_r = gc_r[0:SSM_CONV, lo:hi]
        bias = gc_r[SSM_CONV:SSM_CONV + 1, lo:hi]
        if carried:
            t = lax.broadcasted_iota(jnp.int32, x.shape, 0) % t_in
            y = _conv_rows(x, hist_r[...], w_r, SSM_CONV, t)
        else:
            y = _conv_rows(x, None, w_r, SSM_CONV, None)
            head = x[:SUBLANES, :]
            t8 = lax.broadcasted_iota(jnp.int32, head.shape, 0)
            y8 = _conv_rows(head, carry[:, lo:hi], w_r, SSM_CONV, t8)
            y = jnp.concatenate([y8, y[SUBLANES:, :]], axis=0)
            carry[:, lo:hi] = pltpu.roll(x[x.shape[0] - SUBLANES:, :], SSM_CONV - 1, axis=0)
        return _silu(y + bias)

    xs_blk = conv_act(xs_r, hx_r, 0, gw)
    bmf_blk = conv_act(b_r, hb_r, gw, gw + N)
    cmf_blk = conv_act(c_r, hc_r, gw + N, gw + 2 * N)
    dt_blk = jnp.dot(jnp.concatenate(_split3(dt_r[...]), axis=1), sel_r[...], preferred_element_type=F32)
    r = lax.broadcasted_iota(jnp.int32, (L, L), 0)
    c = lax.broadcasted_iota(jnp.int32, (L, L), 1)
    for u in range(nsub):
        rows = slice(u * L, (u + 1) * L)
        _ssd_chunk(t_in, nseq, r, c, xs_blk[rows], bmf_blk[rows], cmf_blk[rows], dt_blk[rows], z_r[rows, :],
                   a_rate, d_skip, gain, ex_r, st, y_o.at[rows, :])

    @pl.when(ci == nc - 1)
    def _():
        for s in range(nseq):
            st_o[s, 0] = st[:, s * N:(s + 1) * N]


def _ssd_chunk(t_in, nseq, r, c, xs, bmf, cmf, dt, z, a_rate, d_skip, gain, ex_r, st, y_o):
    L = t_in * nseq
    P, N = SSM_HEAD_DIM, D_STATE
    R = xs.shape[1] // P
    bm, cm = bmf.astype(BF16), cmf.astype(BF16)
    da = dt * (a_rate * LOG2E)
    if nseq == 1:
        tri = r >= c
        acum = jnp.dot(tri.astype(F32), da, preferred_element_type=F32, precision=HIGHEST)
        a_tot = acum[L - 1:L, :]
    else:
        same = (r // t_in) == (c // t_in)
        tri = same & (r >= c)
        sums = jnp.dot(jnp.concatenate([tri.astype(F32), same.astype(F32)], axis=0), da,
                       preferred_element_type=F32, precision=HIGHEST)
        acum, a_tot = sums[:L], sums[L:]
    acum_t = acum.T
    dt_t = dt.T.astype(BF16)
    cols = jnp.concatenate([dt * jnp.exp2(a_tot - acum), jnp.exp2(acum)], axis=0)
    wide = jnp.dot(jnp.concatenate(_split3(cols), axis=1), ex_r[...], preferred_element_type=F32)
    w_state, w_off = wide[:L], wide[L:]
    cb = lax.dot_general(cm, bm, NT_DIMS, preferred_element_type=F32).astype(BF16)
    if nseq == 1:
        cblk, bblk = cm, bm
        decay_t = acum_t
    else:
        seq_of_row = lax.broadcasted_iota(jnp.int32, (L, N), 0) // t_in
        cblk = jnp.concatenate([jnp.where(seq_of_row == s, cmf, 0.0) for s in range(nseq)], axis=1).astype(BF16)
        bblk = jnp.concatenate([jnp.where(seq_of_row == s, bmf, 0.0) for s in range(nseq)], axis=1).astype(BF16)
        decay_t = a_tot.T
    s_prev = st[...]
    y_off = lax.dot_general(cblk, s_prev.astype(BF16), NT_DIMS, preferred_element_type=F32) * w_off
    xb = xs.astype(BF16)
    ys = []
    for h in range(R):
        decay = jnp.exp2(jnp.where(tri, acum[:, h:h + 1] - acum_t[h:h + 1, :], -jnp.inf))
        m = cb * decay.astype(BF16) * dt_t[h:h + 1, :]
        ys.append(jnp.dot(m, xb[:, h * P:(h + 1) * P], preferred_element_type=F32))
    new_states = lax.dot_general((xs * w_state).astype(BF16), bblk, TN_DIMS, preferred_element_type=F32)
    for s in range(nseq):
        last = (s + 1) * t_in - 1
        for h in range(R):
            blk = (slice(h * P, (h + 1) * P), slice(s * N, (s + 1) * N))
            st[blk] = s_prev[blk] * jnp.exp2(decay_t[h:h + 1, last:last + 1]) + new_states[blk]
    y = jnp.concatenate(ys, axis=1) + y_off + d_skip * xs
    y = y * _silu(z)
    y = y * lax.rsqrt(jnp.mean(y * y, axis=-1, keepdims=True) + EPS)
    y_o[...] = (y * gain).astype(y_o.dtype)


def _ssd(xbc, z, dt, w_conv, b_conv, a_pad, d_exp, gn_w, hist, init_state, nb, seq):
    G, N, P = SSM_GROUPS, D_STATE, SSM_HEAD_DIM
    inner = z.shape[1]
    gw = inner // G
    rg = gw // P
    L = SSM_CHUNK
    carried = init_state is not None
    if carried:
        t_in, nsub = seq, 1
        assert L % seq == 0 and nb % (L // seq) == 0, (nb, seq)
    else:
        t_in, nsub = L, 8 if seq % (8 * L) == 0 else 1
        assert seq % L == 0, seq
    nseq = L // t_in
    nbb = nb // nseq
    nbc = inner // N
    lb = L * nsub
    nc = max(seq // lb, 1)
    row = lambda b, g, c: b * nc + c

    def triple():
        return [pl.BlockSpec((lb, gw), lambda b, g, c: (row(b, g, c), g)),
                pl.BlockSpec((lb, N), lambda b, g, c: (row(b, g, c), nbc + g)),
                pl.BlockSpec((lb, N), lambda b, g, c: (row(b, g, c), nbc + G + g))]

    cw = gw + 2 * N

    def by_group(v):
        parts = [v[:, :inner].reshape(-1, G, gw), v[:, inner:inner + G * N].reshape(-1, G, N),
                 v[:, inner + G * N:].reshape(-1, G, N)]
        return jnp.transpose(jnp.concatenate(parts, axis=2), (1, 0, 2))

    zpad = lambda n: jnp.zeros((G, 1, n), F32)
    gconst = jnp.concatenate([
        by_group(w_conv.astype(F32)), by_group(b_conv.reshape(1, -1).astype(F32)),
        jnp.concatenate([d_exp.reshape(G, 1, gw), a_pad.reshape(G, 1, LANES), zpad(cw - gw - LANES)], axis=2),
        jnp.concatenate([gn_w.reshape(G, 1, gw), zpad(cw - gw)], axis=2),
        jnp.zeros((G, SUBLANES - SSM_CONV - 3, cw), F32)], axis=1)
    x_spec, b_spec, c_spec = triple()
    in_specs = [x_spec, pl.BlockSpec((lb, gw), lambda b, g, c: (row(b, g, c), g)), b_spec, c_spec,
                pl.BlockSpec((lb, LANES), lambda b, g, c: (row(b, g, c), 0)),
                pl.BlockSpec((None, SUBLANES, cw), lambda b, g, c: (g, 0, 0)),
                pl.BlockSpec((None, 3 * LANES, LANES), lambda b, g, c: (g, 0, 0)),
                pl.BlockSpec((3 * LANES, gw), lambda b, g, c: (0, 0))]
    lane_head = jnp.arange(3 * LANES) % LANES
    widen = (lane_head[:, None] == jnp.arange(gw)[None, :] // P).astype(BF16)
    lane = jnp.arange(LANES)
    pick = ((lane_head[None, :, None] == jnp.arange(G)[:, None, None] * rg + lane[None, None, :])
            & (lane[None, None, :] < rg)).astype(BF16)
    args = [xbc, z, xbc, xbc, dt, gconst, pick, widen]
    if carried:
        in_specs += triple() + [pl.BlockSpec((nseq, 1, gw, N), lambda b, g, c: (b, g, 0, 0))]
        args += [hist, hist, hist, init_state]
    y, st = pl.pallas_call(
        functools.partial(_ssd_body, t_in, nseq, nsub, carried),
        grid=(nbb, G, nc), in_specs=in_specs,
        out_specs=[pl.BlockSpec((lb, gw), lambda b, g, c: (row(b, g, c), g)),
                   pl.BlockSpec((nseq, 1, gw, N), lambda b, g, c: (b, g, 0, 0))],
        out_shape=[jax.ShapeDtypeStruct((nb * seq, inner), BF16), jax.ShapeDtypeStruct((nb, G, gw, N), F32)],
        scratch_shapes=[pltpu.VMEM((gw, nseq * N), F32), pltpu.VMEM((SUBLANES, gw + 2 * N), F32)],
        compiler_params=_params(3), name="ssd",
    )(*args)
    return y, st


def _ffn(x, stats, norm_w, w_gate, w_up, w_down, layer):
    xb, inv = stats
    a = _mm_pair([xb], [_W(w_gate, layer), _W(w_up, layer)], out_dtype=BF16, mode="swiglu", name="ffn_gate_up",
                 norm_in=(inv, norm_w))
    return _mm_pair([a], [_W(w_down, layer)], res=x, name="ffn_down")


def kernel(x_prompt, x_sample, cache_k, cache_v, cache_logf, page_table, state_sconv, state_mconv, state_ssm, norm_mix, norm_ffn, norm_final, w_in_even, b_forget, w_sconv, w_out_even, w_in_ssm, w_mconv, b_mconv, dt_bias, a_log, d_skip, w_gnorm, w_out_ssm, w_gate, w_up, w_down):
    bp, seq, d = x_prompt.shape
    bs, dec_seq, _ = x_sample.shape
    mp, ms = bp * seq, bs * dec_seq
    nh, dh = ATT_HEADS, ATT_HEAD_DIM
    att = nh * dh
    sc_ch = state_sconv.shape[-1]
    x = (x_prompt.reshape(mp, d), x_sample.reshape(ms, d))

    i = 0
    h = _rmsnorm_pair(x, norm_mix[0], BF16)
    c0 = 3 * sc_ch
    w_in_t = jnp.swapaxes(w_in_even, 1, 2)
    win = lambda col0, n: _W(w_in_t, i, col0=col0, n=n, transposed=True)
    gcu = _mm_pair([h], [win(0, c0)], name="in_even_conv")
    q = _mm_pair([h], [win(c0, att)], out_dtype=BF16, scale=dh ** -0.5 * LOG2E, name="in_even_q")
    k = _mm_pair([h], [win(c0 + att, att)], name="in_even_k")
    v = _mm_pair([h], [win(c0 + 2 * att, att)], name="in_even_v")
    lf_p, fc_p, lf_s, fc_s = _forget_pair(h, w_in_even[i, :, c0 + 3 * att:], b_forget[i], bp, seq, dec_seq)
    logf_p = lf_p[:, :nh].reshape(bp, seq, nh)
    logf_s = lf_s[:, :nh].reshape(bs, dec_seq, nh)
    y_a, sconv_p, sconv_s = _sconv_pair(gcu, state_sconv[i], w_sconv[i], bp, seq, dec_seq)
    yb_p = _fox_prompt(q[0], k[0], v[0], fc_p[:, :nh].reshape(bp, seq, nh), bp, seq)
    yb_s = _fox_sample(q[1], k[1], v[1], fc_s[:, :nh].reshape(bs, dec_seq, nh),
                       cache_k[i], cache_v[i], cache_logf[i], page_table, dec_seq)
    x, stats = _mm_pair([y_a, (yb_p, yb_s)],
                        [_W(w_out_even, i, row0=0, k=sc_ch), _W(w_out_even, i, row0=sc_ch, k=att)], res=x,
                        name="out_even", emit_norm=True)
    x = _ffn(x, stats, norm_ffn[0], w_gate, w_up, w_down, 0)

    j = 0
    G = SSM_GROUPS
    n_heads = a_log.shape[1]
    rg = n_heads // G
    inner = n_heads * SSM_HEAD_DIM
    conv_dim = state_mconv.shape[-1]
    h = _rmsnorm_pair(x, norm_mix[1], BF16)
    w_ssm_t = jnp.swapaxes(w_in_ssm, 1, 2)
    z = _mm_pair([h], [_W(w_ssm_t, j, n=inner, transposed=True)], name="in_ssm_z")
    xbc = _mm_pair([h], [_W(w_ssm_t, j, col0=inner, n=conv_dim, transposed=True)], name="in_ssm_xbc")
    assert n_heads <= LANES
    w_dt = jnp.zeros((d, LANES), BF16).at[:, :n_heads].set(w_in_ssm[j, :, inner + conv_dim:].astype(BF16))
    b_dt = jnp.zeros((1, LANES), F32).at[0, :n_heads].set(dt_bias[j].astype(F32))
    dt = _mm_pair([h], [_W(w_dt)], mode="softplus", bias=b_dt, name="in_ssm_dt")
    a_pad = jnp.zeros((G, LANES), F32).at[:, :rg].set(-jnp.exp(a_log[j].astype(F32)).reshape(G, rg)).reshape(1, G * LANES)
    d_exp = jnp.repeat(d_skip[j].astype(F32), SSM_HEAD_DIM).reshape(1, inner)
    gn_w = w_gnorm[j].astype(F32).reshape(1, inner)
    hw = SSM_CONV - 1
    mconv_p = xbc[0].reshape(bp, seq, conv_dim)[:, seq - hw:, :]
    mconv_s = xbc[1].reshape(bs, dec_seq, conv_dim)[:, dec_seq - hw:, :]
    yp, ssm_p = _ssd(xbc[0], z[0], dt[0], w_mconv[j], b_mconv[j], a_pad, d_exp, gn_w, None, None, bp, seq)
    init = state_ssm[j].astype(F32).reshape(bs, G, rg * SSM_HEAD_DIM, D_STATE)
    ysm, ssm_s = _ssd(xbc[1], z[1], dt[1], w_mconv[j], b_mconv[j], a_pad, d_exp, gn_w,
                      _pad_hist(state_mconv[j], dec_seq), init, bs, dec_seq)
    x, stats = _mm_pair([(yp, ysm)], [_W(w_out_ssm, j)], res=x, name="out_ssm", emit_norm=True)
    x = _ffn(x, stats, norm_ffn[1], w_gate, w_up, w_down, 1)

    y = _rmsnorm_pair(x, norm_final, F32)
    st_shape = (n_heads, SSM_HEAD_DIM, D_STATE)
    return (y[0].reshape(bp, seq, d), y[1].reshape(bs, dec_seq, d),
            k[0].reshape(1, bp, seq, nh, dh), v[0].reshape(1, bp, seq, nh, dh), logf_p[None],
            sconv_p[None], mconv_p[None], ssm_p.reshape((1, bp) + st_shape),
            k[1].reshape(1, bs, dec_seq, nh, dh), v[1].reshape(1, bs, dec_seq, nh, dh), logf_s[None],
            sconv_s[None], mconv_s[None], ssm_s.reshape((1, bs) + st_shape))
```

```python
import functools

import jax
import jax.numpy as jnp
from jax import lax
from jax.experimental import pallas as pl
from jax.experimental.pallas import tpu as pltpu

F32 = jnp.float32
BF16 = jnp.bfloat16

SC_WIDTH = 3
ATT_HEADS = 8
ATT_HEAD_DIM = 128
SSM_HEAD_DIM = 64
SSM_GROUPS = 8
D_STATE = 128
SSM_CONV = 4
SSM_CHUNK = 128
EPS = 1e-6

LANES = 128
SUBLANES = 8
VMEM_LIMIT_BYTES = 56 * 1024 * 1024
NEG_BIG = -1e30
LOG2E = 1.4426950408889634

HIGHEST = lax.Precision.HIGHEST
NT_DIMS = (((1,), (1,)), ((), ()))
TN_DIMS = (((0,), (0,)), ((), ()))


def _params(n_grid_dims):
    return pltpu.CompilerParams(dimension_semantics=("arbitrary",) * n_grid_dims,
                                vmem_limit_bytes=VMEM_LIMIT_BYTES)


def _row_tile(mp, pref):
    t = pref
    while mp % t:
        t //= 2
    assert t >= SUBLANES, (mp, pref)
    return t


def _rmsnorm_body(nt, xp, xs, w, op, os):
    i = pl.program_id(0)

    def norm(x):
        y = x * lax.rsqrt(jnp.mean(x * x, axis=-1, keepdims=True) + EPS)
        return (y * w[...]).astype(op.dtype)

    @pl.when(i < nt)
    def _():
        op[...] = norm(xp[...])

    @pl.when(i == nt)
    def _():
        os[...] = norm(xs[...])


def _rmsnorm_pair(x, w, out_dtype):
    xp, xs = x
    mp, d = xp.shape
    ms = xs.shape[0]
    tm = _row_tile(mp, 1024)
    nt = mp // tm
    pidx = lambda i: (jnp.minimum(i, nt - 1), 0)
    return pl.pallas_call(
        functools.partial(_rmsnorm_body, nt),
        grid=(nt + 1,),
        in_specs=[pl.BlockSpec((tm, d), pidx), pl.BlockSpec((ms, d), lambda i: (0, 0)),
                  pl.BlockSpec((1, d), lambda i: (0, 0))],
        out_specs=[pl.BlockSpec((tm, d), pidx), pl.BlockSpec((ms, d), lambda i: (0, 0))],
        out_shape=[jax.ShapeDtypeStruct((mp, d), out_dtype), jax.ShapeDtypeStruct((ms, d), out_dtype)],
        compiler_params=_params(1), name="rmsnorm",
    )(xp, xs, w.reshape(1, d).astype(F32))


def _softplus(x):
    return jnp.maximum(x, 0.0) + jnp.log1p(jnp.exp(-jnp.abs(x)))


def _silu(x):
    return x * (1.0 / (1.0 + jnp.exp(-x)))


def _split3(v):
    hi = v.astype(BF16)
    r1 = v - hi.astype(F32)
    mid = r1.astype(BF16)
    lo = (r1 - mid.astype(F32)).astype(BF16)
    return hi, mid, lo


class _W:
    def __init__(self, arr, layer=None, row0=0, k=None, col0=0, n=None, transposed=False):
        self.arr, self.layer, self.row0, self.col0, self.transposed = arr, layer, row0, col0, transposed
        kdim, ndim = (-1, -2) if transposed else (-2, -1)
        self.k = arr.shape[kdim] if k is None else k
        self.n = arr.shape[ndim] if n is None else n

    def block(self, tn):
        return (tn, self.k) if self.transposed else (self.k, tn)

    def spec(self, tn):
        assert self.row0 % self.k == 0 and self.col0 % tn == 0
        rb, cb = self.row0 // self.k, self.col0 // tn
        if self.transposed:
            idx = lambda j, i: (cb + j, rb)
        else:
            idx = lambda j, i: (rb, cb + j)
        if self.layer is None:
            return pl.BlockSpec(self.block(tn), idx)
        layer = self.layer
        return pl.BlockSpec((None,) + self.block(tn), lambda j, i: (layer,) + idx(j, i))


def _mm_body(n_lhs, mode, has_res, has_bias, scale, w_is_f32, w_transposed, *refs):
    i = pl.program_id(1)

    def mm(x, idx):
        dims = NT_DIMS if w_transposed[idx] else (((1,), (0,)), ((), ()))
        return lax.dot_general(x, wrefs[idx][...], dims, preferred_element_type=F32)

    xrefs = refs[:2 * n_lhs]
    n_w = len(w_is_f32)
    wrefs = list(refs[2 * n_lhs:2 * n_lhs + n_w])
    pos = 2 * n_lhs + n_w
    rp = rs = bias = None
    if has_res:
        rp, rs = refs[pos], refs[pos + 1]
        pos += 2
    if has_bias:
        bias = refs[pos]
        pos += 1
    op, os = refs[pos], refs[pos + 1]
    scratch = list(refs[pos + 2:])

    for idx in range(n_w):
        if w_is_f32[idx]:
            wsrc, wdst = wrefs[idx], scratch.pop(0)

            @pl.when(i == 0)
            def _(wsrc=wsrc, wdst=wdst):
                wdst[...] = wsrc[...].astype(BF16)

            wrefs[idx] = wdst

    def compute(which, out, res):
        if mode == "swiglu":
            x = xrefs[which][...]
            acc = _silu(mm(x, 0)) * mm(x, 1)
        else:
            acc = mm(xrefs[which][...], 0)
            for k in range(1, n_lhs):
                acc += mm(xrefs[2 * k + which][...], k)
        if scale is not None:
            acc = acc * scale
        if has_bias:
            acc = acc + bias[...]
        if mode == "softplus":
            acc = _softplus(acc)
        if has_res:
            acc = res[...] + acc
        out[...] = acc.astype(out.dtype)

    @pl.when(i == 0)
    def _():
        compute(1, os, rs)

    @pl.when(i > 0)
    def _():
        compute(0, op, rp)


def _mm_pair(xs_list, ws_list, *, out_dtype=F32, mode="plain", res=None, bias=None, scale=None, name="matmul"):
    n_lhs = len(xs_list)
    mp = xs_list[0][0].shape[0]
    ms = xs_list[0][1].shape[0]
    n = ws_list[0].n
    kmax = max(w.k for w in ws_list)
    tm = _row_tile(mp, 1024 if kmax <= 4096 else 512)
    nt = mp // tm
    if n % 1024 == 0 and kmax <= 2048 and mode != "swiglu":
        tn = 1024
    elif n % 512 == 0:
        tn = 512
    else:
        assert n % LANES == 0 and n <= 512, n
        tn = n
    pidx_x = lambda j, i: (jnp.maximum(i - 1, 0), 0)
    pidx_o = lambda j, i: (jnp.maximum(i - 1, 0), j)
    in_specs, args = [], []
    for xp, xs in xs_list:
        kk = xp.shape[1]
        in_specs += [pl.BlockSpec((tm, kk), pidx_x), pl.BlockSpec((ms, kk), lambda j, i: (0, 0))]
        args += [xp, xs]
    w_is_f32, scratch = [], []
    for w in ws_list:
        in_specs.append(w.spec(tn))
        args.append(w.arr)
        w_is_f32.append(w.arr.dtype != BF16)
        if w_is_f32[-1]:
            scratch.append(pltpu.VMEM(w.block(tn), BF16))
    if res is not None:
        in_specs += [pl.BlockSpec((tm, tn), pidx_o), pl.BlockSpec((ms, tn), lambda j, i: (0, j))]
        args += [res[0], res[1]]
    if bias is not None:
        in_specs.append(pl.BlockSpec((1, tn), lambda j, i: (0, j)))
        args.append(bias.reshape(1, n).astype(F32))
    outs = pl.pallas_call(
        functools.partial(_mm_body, n_lhs, mode, res is not None, bias is not None, scale, tuple(w_is_f32),
                          tuple(w.transposed for w in ws_list)),
        grid=(n // tn, nt + 1),
        in_specs=in_specs,
        out_specs=[pl.BlockSpec((tm, tn), pidx_o), pl.BlockSpec((ms, tn), lambda j, i: (0, j))],
        out_shape=[jax.ShapeDtypeStruct((mp, n), out_dtype), jax.ShapeDtypeStruct((ms, n), out_dtype)],
        scratch_shapes=scratch,
        compiler_params=_params(2), name=name,
    )(*args)
    return outs[0], outs[1]


def _log_sigmoid(x):
    return -_softplus(-x)


def _forget_body(nb, seq, dec_seq, ck, xp, xs, w, b, lp, fp, ls, fs):
    i = pl.program_id(0)

    @pl.when(i < nb)
    def _():
        r = lax.broadcasted_iota(jnp.int32, (ck, ck), 0)
        c = lax.broadcasted_iota(jnp.int32, (ck, ck), 1)
        tri = (r >= c).astype(F32)
        carry = jnp.zeros((1, LANES), F32)
        for t in range(seq // ck):
            f = jnp.dot(xp[t * ck:(t + 1) * ck, :], w[...], preferred_element_type=F32) + b[...]
            lf = _log_sigmoid(f)
            lp[t * ck:(t + 1) * ck, :] = lf
            cs = jnp.dot(tri, lf, preferred_element_type=F32, precision=HIGHEST) + carry
            fp[t * ck:(t + 1) * ck, :] = cs * LOG2E
            carry = cs[ck - 1:ck, :]

    @pl.when(i == nb)
    def _():
        ms = xs.shape[0]
        r = lax.broadcasted_iota(jnp.int32, (ms, ms), 0)
        c = lax.broadcasted_iota(jnp.int32, (ms, ms), 1)
        tri = ((r // dec_seq == c // dec_seq) & (r >= c)).astype(F32)
        f = jnp.dot(xs[...], w[...], preferred_element_type=F32) + b[...]
        lf = _log_sigmoid(f)
        ls[...] = lf
        fs[...] = jnp.dot(tri, lf, preferred_element_type=F32, precision=HIGHEST) * LOG2E


def _forget_pair(h, w_f, b_f, nb, seq, dec_seq):
    hp, hs = h
    mp, d = hp.shape
    ms = hs.shape[0]
    assert ms <= 512 and ms % SUBLANES == 0
    ck = _row_tile(seq, 256)
    n_h = w_f.shape[1]
    wpad = jnp.zeros((d, LANES), BF16).at[:, :n_h].set(w_f.astype(BF16))
    bpad = jnp.zeros((1, LANES), F32).at[0, :n_h].set(b_f.astype(F32))
    pidx = lambda i: (jnp.minimum(i, nb - 1), 0)
    cidx = lambda i: (0, 0)
    return pl.pallas_call(
        functools.partial(_forget_body, nb, seq, dec_seq, ck),
        grid=(nb + 1,),
        in_specs=[pl.BlockSpec((seq, d), pidx), pl.BlockSpec((ms, d), cidx),
                  pl.BlockSpec((d, LANES), cidx), pl.BlockSpec((1, LANES), cidx)],
        out_specs=[pl.BlockSpec((seq, LANES), pidx), pl.BlockSpec((seq, LANES), pidx),
                   pl.BlockSpec((ms, LANES), cidx), pl.BlockSpec((ms, LANES), cidx)],
        out_shape=[jax.ShapeDtypeStruct((mp, LANES), F32), jax.ShapeDtypeStruct((mp, LANES), F32),
                   jax.ShapeDtypeStruct((ms, LANES), F32), jax.ShapeDtypeStruct((ms, LANES), F32)],
        compiler_params=_params(1), name="forget_gates",
    )(hp, hs, wpad, bpad)


def _shifted(x, hist, k, width, t_in_seq):
    n = x.shape[0]
    xk = pltpu.roll(x, k, axis=0)
    if hist is None:
        return xk
    back = (width - 1 - k) % n
    hk = hist if back == 0 else pltpu.roll(hist, n - back, axis=0)
    return jnp.where(t_in_seq >= k, xk, hk)


def _conv_rows(x, hist, wref, width, t_in_seq):
    acc = x * wref[width - 1:width, :]
    for k in range(1, width):
        acc += _shifted(x, hist, k, width, t_in_seq) * wref[width - 1 - k:width - k, :]
    return acc


def _sconv_body(nt, tiles_per_seq, width, dec_seq, gbp, gcp, up, gbs, gcs, us, hs, w, yp, ys, zp_tail, zs, carry):
    i = pl.program_id(1)

    @pl.when(i < nt)
    def _():
        t0 = i % tiles_per_seq
        x = gcp[...] * up[...]
        ts = x.shape[0]

        @pl.when(t0 == 0)
        def _():
            carry[...] = jnp.zeros_like(carry)

        yp[...] = (gbp[...] * _conv_rows(x, None, w, width, None)).astype(yp.dtype)
        head = x[:SUBLANES, :]
        t8 = lax.broadcasted_iota(jnp.int32, head.shape, 0)
        yp[:SUBLANES, :] = (gbp[:SUBLANES, :] * _conv_rows(head, carry[...], w, width, t8)).astype(yp.dtype)
        tail = x[ts - SUBLANES:, :]
        carry[...] = pltpu.roll(tail, width - 1, axis=0)
        zp_tail[0] = tail

    @pl.when(i == nt)
    def _():
        x = gcs[...] * us[...]
        t = lax.broadcasted_iota(jnp.int32, x.shape, 0) % dec_seq
        ys[...] = (gbs[...] * _conv_rows(x, hs[...], w, width, t)).astype(ys.dtype)
        zs[...] = x


def _pad_hist(state, dec_seq):
    b, hw, c = state.shape
    assert dec_seq >= hw
    return jnp.concatenate([state.astype(F32), jnp.zeros((b, dec_seq - hw, c), F32)], axis=1).reshape(b * dec_seq, c)


def _sconv_pair(gcu, state, w, nb, seq, dec_seq):
    gp, gs = gcu
    mp = gp.shape[0]
    ms = gs.shape[0]
    ch = gp.shape[1] // 3
    tc = 1024 if ch % 1024 == 0 else 512
    ncb = ch // tc
    ts = _row_tile(seq, 512)
    tps = seq // ts
    nt = mp // ts
    pr = lambda j, i: jnp.minimum(i, nt - 1)
    in_specs = []
    for part in range(3):
        in_specs.append(pl.BlockSpec((ts, tc), lambda j, i, part=part: (pr(j, i), part * ncb + j)))
    for part in range(3):
        in_specs.append(pl.BlockSpec((ms, tc), lambda j, i, part=part: (0, part * ncb + j)))
    in_specs += [pl.BlockSpec((ms, tc), lambda j, i: (0, j)), pl.BlockSpec((SC_WIDTH, tc), lambda j, i: (0, j))]
    out_specs = [pl.BlockSpec((ts, tc), lambda j, i: (pr(j, i), j)),
                 pl.BlockSpec((ms, tc), lambda j, i: (0, j)),
                 pl.BlockSpec((1, SUBLANES, tc), lambda j, i: (pr(j, i) // tps, 0, j)),
                 pl.BlockSpec((ms, tc), lambda j, i: (0, j))]
    out_shape = [jax.ShapeDtypeStruct((mp, ch), BF16), jax.ShapeDtypeStruct((ms, ch), BF16),
                 jax.ShapeDtypeStruct((nb, SUBLANES, ch), F32), jax.ShapeDtypeStruct((ms, ch), F32)]
    yp, ys, zp_tail, zs = pl.pallas_call(
        functools.partial(_sconv_body, nt, tps, SC_WIDTH, dec_seq),
        grid=(ncb, nt + 1), in_specs=in_specs, out_specs=out_specs, out_shape=out_shape,
        scratch_shapes=[pltpu.VMEM((SUBLANES, tc), F32)],
        compiler_params=_params(2), name="short_conv",
    )(gp, gp, gp, gs, gs, gs, _pad_hist(state, dec_seq), w.astype(F32))
    hw = SC_WIDTH - 1
    new_p = zp_tail[:, SUBLANES - hw:, :]
    new_s = zs.reshape(ms // dec_seq, dec_seq, ch)[:, dec_seq - hw:, :]
    return (yp, ys), new_p, new_s


def _fox_prompt_body(tq, hps, q, k, v, fr, o, m_s, l_s, acc_s):
    qi = pl.program_id(2)
    dh = ATT_HEAD_DIM
    m_s[...] = jnp.full_like(m_s, NEG_BIG)
    l_s[...] = jnp.zeros_like(l_s)
    acc_s[...] = jnp.zeros_like(acc_s)

    def step(ki, masked):
        start = pl.multiple_of(ki * tq, tq)
        for hh in range(hps):
            cols = slice(hh * dh, (hh + 1) * dh)
            kb = k[pl.ds(start, tq), cols].astype(BF16)
            vb = v[pl.ds(start, tq), cols].astype(BF16)
            s = lax.dot_general(q[:, cols], kb, NT_DIMS, preferred_element_type=F32) - fr[0, hh, ki]
            if masked:
                r = lax.broadcasted_iota(jnp.int32, s.shape, 0)
                c = lax.broadcasted_iota(jnp.int32, s.shape, 1)
                s = jnp.where(r >= c, s, NEG_BIG)
            m_prev = m_s[hh]
            m_new = jnp.maximum(m_prev, jnp.max(s, axis=-1, keepdims=True))
            p = jnp.exp2(s - jnp.tile(m_new, (1, tq // dh)))
            alpha = jnp.exp2(m_prev - m_new)
            l_s[hh] = alpha * l_s[hh] + jnp.sum(p, axis=-1, keepdims=True)
            acc_s[hh] = alpha * acc_s[hh] + jnp.dot(p.astype(BF16), vb, preferred_element_type=F32)
            m_s[hh] = m_new

    def loop_body(ki, carry):
        step(ki, False)
        return carry

    lax.fori_loop(0, qi, loop_body, 0)
    step(qi, True)
    for hh in range(hps):
        o[:, hh * dh:(hh + 1) * dh] = (acc_s[hh] / l_s[hh]).astype(o.dtype)


def _fox_prompt(q, k, v, fcum, nb, seq):
    mp = q.shape[0]
    dh = ATT_HEAD_DIM
    nh = q.shape[1] // dh
    tq = _row_tile(seq, 512)
    nq = seq // tq
    frow = jnp.transpose(fcum, (0, 2, 1)).reshape(nb, nh, nq, 1, tq)
    hps = 4 if nh % 4 == 0 else 1
    gw = hps * dh
    return pl.pallas_call(
        functools.partial(_fox_prompt_body, tq, hps),
        grid=(nb, nh // hps, nq),
        in_specs=[pl.BlockSpec((tq, gw), lambda b, h, i: (b * nq + i, h)),
                  pl.BlockSpec((seq, gw), lambda b, h, i: (b, h)),
                  pl.BlockSpec((seq, gw), lambda b, h, i: (b, h)),
                  pl.BlockSpec((1, hps, nq, 1, tq), lambda b, h, i: (b, h, 0, 0, 0))],
        out_specs=pl.BlockSpec((tq, gw), lambda b, h, i: (b * nq + i, h)),
        out_shape=jax.ShapeDtypeStruct((mp, nh * dh), BF16),
        scratch_shapes=[pltpu.VMEM((hps, tq, dh), F32), pltpu.VMEM((hps, tq, dh), F32),
                        pltpu.VMEM((hps, tq, dh), F32)],
        compiler_params=_params(3), name="fox_prompt",
    )(q, k, v, frow)


def _past_bias_body(n_pages, page, nh, *refs):
    lf_refs = refs[1:1 + n_pages]
    out = refs[1 + n_pages]
    rows, later_in_page, spread_heads = refs[2 + n_pages:]
    w = page * nh
    npp = rows.shape[0]

    @pl.when(pl.program_id(0) == 0)
    def _():
        a = lax.broadcasted_iota(jnp.int32, (w, w), 0)
        c = lax.broadcasted_iota(jnp.int32, (w, w), 1)
        later_in_page[...] = (((a % nh) == (c % nh)) & ((a // nh) > (c // nh))).astype(BF16)
        k = lax.broadcasted_iota(jnp.int32, spread_heads.shape, 0)
        c2 = lax.broadcasted_iota(jnp.int32, spread_heads.shape, 1)
        spread_heads[...] = ((k % LANES) == (c2 % nh)).astype(BF16)
        rows[...] = jnp.zeros_like(rows)

    for p in range(n_pages):
        rows[p:p + 1, :] = lf_refs[p][0]
    lf = rows[...]
    within3 = jnp.dot(jnp.concatenate(_split3(lf), axis=0), later_in_page[...], preferred_element_type=F32)
    within = within3[:npp] + within3[npp:2 * npp] + within3[2 * npp:]
    first = (within + lf)[:, :LANES]
    totals = jnp.dot(jnp.concatenate(_split3(first), axis=1), spread_heads[...], preferred_element_type=F32)
    pr = lax.broadcasted_iota(jnp.int32, (npp, npp), 0)
    pc = lax.broadcasted_iota(jnp.int32, (npp, npp), 1)
    later_pages = (pc > pr).astype(BF16)
    after3 = jnp.dot(later_pages, jnp.concatenate(_split3(totals), axis=1), preferred_element_type=F32)
    after = after3[:, :w] + after3[:, w:2 * w] + after3[:, 2 * w:]
    out[0] = (within + after)[:n_pages, :] * LOG2E


def _past_bias(cache_lf, page_table):
    n_phys, page, nh = cache_lf.shape
    nb, n_pages = page_table.shape
    w = page * nh
    np_pad = -(-n_pages // LANES) * LANES
    lf = cache_lf.reshape(n_phys, 1, w)
    in_specs = [pl.BlockSpec((1, 1, w), lambda b, pt, p=p: (pt[b, p], 0, 0)) for p in range(n_pages)]
    grid_spec = pltpu.PrefetchScalarGridSpec(
        num_scalar_prefetch=1, grid=(nb,), in_specs=in_specs,
        out_specs=pl.BlockSpec((1, n_pages, w), lambda b, pt: (b, 0, 0)),
        scratch_shapes=[pltpu.VMEM((np_pad, w), F32), pltpu.VMEM((w, w), BF16), pltpu.VMEM((3 * LANES, w), BF16)])
    return pl.pallas_call(
        functools.partial(_past_bias_body, n_pages, page, nh),
        grid_spec=grid_spec, out_shape=jax.ShapeDtypeStruct((nb, n_pages, w), F32),
        compiler_params=_params(1), name="fox_past_bias",
    )(page_table, *([lf] * n_pages))


def _fox_sample_body(pps, n_steps, nh, dec_seq, *refs):
    qall, knew, vnew, frow_new, rb = refs[1:6]
    krefs = refs[6:6 + pps]
    vrefs = refs[6 + pps:6 + 2 * pps]
    o = refs[6 + 2 * pps]
    m_s, l_s, acc_s, s_scr = refs[7 + 2 * pps:]
    step = pl.program_id(1)
    qv = qall[0]
    nrow = qv.shape[0]

    @pl.when(step == 0)
    def _():
        kn = knew[0].astype(BF16)
        g = lax.dot_general(qv, kn, NT_DIMS, preferred_element_type=F32)
        r = lax.broadcasted_iota(jnp.int32, g.shape, 0)
        c = lax.broadcasted_iota(jnp.int32, g.shape, 1)
        valid = ((c % nh) == (r // dec_seq)) & ((c // nh) <= (r % dec_seq))
        s = jnp.where(valid, g - frow_new[0], NEG_BIG)
        m = jnp.max(s, axis=-1, keepdims=True)
        p = jnp.exp2(s - m)
        m_s[...] = m
        l_s[...] = jnp.sum(p, axis=-1, keepdims=True)
        acc_s[...] = jnp.dot(p.astype(BF16), vnew[0].astype(BF16), preferred_element_type=F32)

    lanes = krefs[0].shape[1]
    r = lax.broadcasted_iota(jnp.int32, (nrow, lanes), 0)
    c = lax.broadcasted_iota(jnp.int32, (nrow, lanes), 1)
    head_ok = (c % nh) == (r // dec_seq)
    m_prev = m_s[...]
    m_new = m_prev
    for i in range(pps):
        g = lax.dot_general(qv, krefs[i][0].astype(BF16), NT_DIMS, preferred_element_type=F32)
        s = jnp.where(head_ok, g + rb[0, i:i + 1, :], NEG_BIG)
        s_scr[i] = s
        m_new = jnp.maximum(m_new, jnp.max(s, axis=-1, keepdims=True))
    alpha = jnp.exp2(m_prev - m_new)
    psum = acc = None
    for i in range(pps):
        p = jnp.exp2(s_scr[i] - m_new)
        pv = jnp.dot(p.astype(BF16), vrefs[i][0].astype(BF16), preferred_element_type=F32)
        ps = jnp.sum(p, axis=-1, keepdims=True)
        psum = ps if psum is None else psum + ps
        acc = pv if acc is None else acc + pv
    l_s[...] = alpha * l_s[...] + psum
    acc_s[...] = alpha * acc_s[...] + acc
    m_s[...] = m_new

    @pl.when(step == n_steps - 1)
    def _():
        o[0] = (acc_s[...] / l_s[...]).astype(o.dtype)


def _fox_sample(q, k, v, fn, cache_k, cache_v, cache_lf, page_table, dec_seq):
    ms = q.shape[0]
    nb = ms // dec_seq
    nh, dh = ATT_HEADS, ATT_HEAD_DIM
    n_phys, page = cache_k.shape[0], cache_k.shape[1]
    n_pages = page_table.shape[1]
    pps = 16
    while n_pages % pps:
        pps //= 2
    n_steps = n_pages // pps
    nrow = nh * dec_seq
    rbias = _past_bias(cache_lf, page_table)
    qall = jnp.transpose(q.reshape(nb, dec_seq, nh, dh), (0, 2, 1, 3)).reshape(nb, nrow, dh)
    knew = k.reshape(nb, dec_seq * nh, dh)
    vnew = v.reshape(nb, dec_seq * nh, dh)
    frow_new = fn.reshape(nb, 1, dec_seq * nh)
    ck = cache_k.reshape(n_phys, page * nh, dh)
    cv = cache_v.reshape(n_phys, page * nh, dh)
    per_b = lambda shape: pl.BlockSpec((1,) + shape, lambda b, s, pt: (b, 0, 0))
    in_specs = [per_b((nrow, dh)), per_b((dec_seq * nh, dh)), per_b((dec_seq * nh, dh)),
                per_b((1, dec_seq * nh)),
                pl.BlockSpec((1, pps, page * nh), lambda b, s, pt: (b, s, 0))]
    for _ in range(2):
        in_specs += [pl.BlockSpec((1, page * nh, dh), lambda b, s, pt, i=i: (pt[b, s * pps + i], 0, 0))
                     for i in range(pps)]
    grid_spec = pltpu.PrefetchScalarGridSpec(
        num_scalar_prefetch=1, grid=(nb, n_steps), in_specs=in_specs,
        out_specs=pl.BlockSpec((1, nrow, dh), lambda b, s, pt: (b, 0, 0)),
        scratch_shapes=[pltpu.VMEM((nrow, 1), F32), pltpu.VMEM((nrow, 1), F32), pltpu.VMEM((nrow, dh), F32),
                        pltpu.VMEM((pps, nrow, page * nh), F32)])
    o = pl.pallas_call(
        functools.partial(_fox_sample_body, pps, n_steps, nh, dec_seq),
        grid_spec=grid_spec, out_shape=jax.ShapeDtypeStruct((nb, nrow, dh), BF16),
        compiler_params=_params(2), name="fox_sample",
    )(page_table, qall, knew, vnew, frow_new, rbias, *([ck] * pps), *([cv] * pps))
    return jnp.transpose(o.reshape(nb, nh, dec_seq, dh), (0, 2, 1, 3)).reshape(ms, nh * dh)


def _ssd_body(t_in, nseq, nsub, carried, xs_r, z_r, b_r, c_r, dt_r, gc_r, sel_r, ex_r, *rest):
    if carried:
        hx_r, hb_r, hc_r, s0_r, y_o, st_o, st, carry = rest
    else:
        hx_r = hb_r = hc_r = s0_r = None
        y_o, st_o, st, carry = rest
    ci = pl.program_id(2)
    nc = pl.num_programs(2)
    L = t_in * nseq
    N = D_STATE
    gw = xs_r.shape[1]

    @pl.when(ci == 0)
    def _():
        if carried:
            for s in range(nseq):
                st[:, s * N:(s + 1) * N] = s0_r[s, 0]
        else:
            st[...] = jnp.zeros_like(st)
            carry[...] = jnp.zeros_like(carry)

    d_skip = gc_r[SSM_CONV + 1:SSM_CONV + 2, 0:gw]
    a_rate = gc_r[SSM_CONV + 1:SSM_CONV + 2, gw:gw + LANES]
    gain = gc_r[SSM_CONV + 2:SSM_CONV + 3, 0:gw]

    def conv_act(raw_r, hist_r, lo, hi):
        x = raw_r[...]
        w_r = gc_r[0:SSM_CONV, lo:hi]
        bias = gc_r[SSM_CONV:SSM_CONV + 1, lo:hi]
        if carried:
            t = lax.broadcasted_iota(jnp.int32, x.shape, 0) % t_in
            y = _conv_rows(x, hist_r[...], w_r, SSM_CONV, t)
        else:
            y = _conv_rows(x, None, w_r, SSM_CONV, None)
            head = x[:SUBLANES, :]
            t8 = lax.broadcasted_iota(jnp.int32, head.shape, 0)
            y8 = _conv_rows(head, carry[:, lo:hi], w_r, SSM_CONV, t8)
            y = jnp.concatenate([y8, y[SUBLANES:, :]], axis=0)
            carry[:, lo:hi] = pltpu.roll(x[x.shape[0] - SUBLANES:, :], SSM_CONV - 1, axis=0)
        return _silu(y + bias)

    xs_blk = conv_act(xs_r, hx_r, 0, gw)
    bmf_blk = conv_act(b_r, hb_r, gw, gw + N)
    cmf_blk = conv_act(c_r, hc_r, gw + N, gw + 2 * N)
    dt_blk = jnp.dot(jnp.concatenate(_split3(dt_r[...]), axis=1), sel_r[...], preferred_element_type=F32)
    r = lax.broadcasted_iota(jnp.int32, (L, L), 0)
    c = lax.broadcasted_iota(jnp.int32, (L, L), 1)
    for u in range(nsub):
        rows = slice(u * L, (u + 1) * L)
        _ssd_chunk(t_in, nseq, r, c, xs_blk[rows], bmf_blk[rows], cmf_blk[rows], dt_blk[rows], z_r[rows, :],
                   a_rate, d_skip, gain, ex_r, st, y_o.at[rows, :])

    @pl.when(ci == nc - 1)
    def _():
        for s in range(nseq):
            st_o[s, 0] = st[:, s * N:(s + 1) * N]


def _ssd_chunk(t_in, nseq, r, c, xs, bmf, cmf, dt, z, a_rate, d_skip, gain, ex_r, st, y_o):
    L = t_in * nseq
    P, N = SSM_HEAD_DIM, D_STATE
    R = xs.shape[1] // P
    bm, cm = bmf.astype(BF16), cmf.astype(BF16)
    da = dt * (a_rate * LOG2E)
    if nseq == 1:
        tri = r >= c
        acum = jnp.dot(tri.astype(F32), da, preferred_element_type=F32, precision=HIGHEST)
        a_tot = acum[L - 1:L, :]
    else:
        same = (r // t_in) == (c // t_in)
        tri = same & (r >= c)
        sums = jnp.dot(jnp.concatenate([tri.astype(F32), same.astype(F32)], axis=0), da,
                       preferred_element_type=F32, precision=HIGHEST)
        acum, a_tot = sums[:L], sums[L:]
    acum_t = acum.T
    dt_t = dt.T.astype(BF16)
    cols = jnp.concatenate([dt * jnp.exp2(a_tot - acum), jnp.exp2(acum)], axis=0)
    wide = jnp.dot(jnp.concatenate(_split3(cols), axis=1), ex_r[...], preferred_element_type=F32)
    w_state, w_off = wide[:L], wide[L:]
    cb = lax.dot_general(cm, bm, NT_DIMS, preferred_element_type=F32).astype(BF16)
    if nseq == 1:
        cblk, bblk = cm, bm
        decay_t = acum_t
    else:
        seq_of_row = lax.broadcasted_iota(jnp.int32, (L, N), 0) // t_in
        cblk = jnp.concatenate([jnp.where(seq_of_row == s, cmf, 0.0) for s in range(nseq)], axis=1).astype(BF16)
        bblk = jnp.concatenate([jnp.where(seq_of_row == s, bmf, 0.0) for s in range(nseq)], axis=1).astype(BF16)
        decay_t = a_tot.T
    s_prev = st[...]
    y_off = lax.dot_general(cblk, s_prev.astype(BF16), NT_DIMS, preferred_element_type=F32) * w_off
    xb = xs.astype(BF16)
    ys = []
    for h in range(R):
        decay = jnp.exp2(jnp.where(tri, acum[:, h:h + 1] - acum_t[h:h + 1, :], -jnp.inf))
        m = cb * decay.astype(BF16) * dt_t[h:h + 1, :]
        ys.append(jnp.dot(m, xb[:, h * P:(h + 1) * P], preferred_element_type=F32))
    new_states = lax.dot_general((xs * w_state).astype(BF16), bblk, TN_DIMS, preferred_element_type=F32)
    for s in range(nseq):
        last = (s + 1) * t_in - 1
        for h in range(R):
            blk = (slice(h * P, (h + 1) * P), slice(s * N, (s + 1) * N))
            st[blk] = s_prev[blk] * jnp.exp2(decay_t[h:h + 1, last:last + 1]) + new_states[blk]
    y = jnp.concatenate(ys, axis=1) + y_off + d_skip * xs
    y = y * _silu(z)
    y = y * lax.rsqrt(jnp.mean(y * y, axis=-1, keepdims=True) + EPS)
    y_o[...] = (y * gain).astype(y_o.dtype)


def _ssd(xbc, z, dt, w_conv, b_conv, a_pad, d_exp, gn_w, hist, init_state, nb, seq):
    G, N, P = SSM_GROUPS, D_STATE, SSM_HEAD_DIM
    inner = z.shape[1]
    gw = inner // G
    rg = gw // P
    L = SSM_CHUNK
    carried = init_state is not None
    if carried:
        t_in, nsub = seq, 1
        assert L % seq == 0 and nb % (L // seq) == 0, (nb, seq)
    else:
        t_in, nsub = L, 8 if seq % (8 * L) == 0 else 1
        assert seq % L == 0, seq
    nseq = L // t_in
    nbb = nb // nseq
    nbc = inner // N
    lb = L * nsub
    nc = max(seq // lb, 1)
    row = lambda b, g, c: b * nc + c

    def triple():
        return [pl.BlockSpec((lb, gw), lambda b, g, c: (row(b, g, c), g)),
                pl.BlockSpec((lb, N), lambda b, g, c: (row(b, g, c), nbc + g)),
                pl.BlockSpec((lb, N), lambda b, g, c: (row(b, g, c), nbc + G + g))]

    cw = gw + 2 * N

    def by_group(v):
        parts = [v[:, :inner].reshape(-1, G, gw), v[:, inner:inner + G * N].reshape(-1, G, N),
                 v[:, inner + G * N:].reshape(-1, G, N)]
        return jnp.transpose(jnp.concatenate(parts, axis=2), (1, 0, 2))

    zpad = lambda n: jnp.zeros((G, 1, n), F32)
    gconst = jnp.concatenate([
        by_group(w_conv.astype(F32)), by_group(b_conv.reshape(1, -1).astype(F32)),
        jnp.concatenate([d_exp.reshape(G, 1, gw), a_pad.reshape(G, 1, LANES), zpad(cw - gw - LANES)], axis=2),
        jnp.concatenate([gn_w.reshape(G, 1, gw), zpad(cw - gw)], axis=2),
        jnp.zeros((G, SUBLANES - SSM_CONV - 3, cw), F32)], axis=1)
    x_spec, b_spec, c_spec = triple()
    in_specs = [x_spec, pl.BlockSpec((lb, gw), lambda b, g, c: (row(b, g, c), g)), b_spec, c_spec,
                pl.BlockSpec((lb, LANES), lambda b, g, c: (row(b, g, c), 0)),
                pl.BlockSpec((None, SUBLANES, cw), lambda b, g, c: (g, 0, 0)),
                pl.BlockSpec((None, 3 * LANES, LANES), lambda b, g, c: (g, 0, 0)),
                pl.BlockSpec((3 * LANES, gw), lambda b, g, c: (0, 0))]
    lane_head = jnp.arange(3 * LANES) % LANES
    widen = (lane_head[:, None] == jnp.arange(gw)[None, :] // P).astype(BF16)
    lane = jnp.arange(LANES)
    pick = ((lane_head[None, :, None] == jnp.arange(G)[:, None, None] * rg + lane[None, None, :])
            & (lane[None, None, :] < rg)).astype(BF16)
    args = [xbc, z, xbc, xbc, dt, gconst, pick, widen]
    if carried:
        in_specs += triple() + [pl.BlockSpec((nseq, 1, gw, N), lambda b, g, c: (b, g, 0, 0))]
        args += [hist, hist, hist, init_state]
    y, st = pl.pallas_call(
        functools.partial(_ssd_body, t_in, nseq, nsub, carried),
        grid=(nbb, G, nc), in_specs=in_specs,
        out_specs=[pl.BlockSpec((lb, gw), lambda b, g, c: (row(b, g, c), g)),
                   pl.BlockSpec((nseq, 1, gw, N), lambda b, g, c: (b, g, 0, 0))],
        out_shape=[jax.ShapeDtypeStruct((nb * seq, inner), BF16), jax.ShapeDtypeStruct((nb, G, gw, N), F32)],
        scratch_shapes=[pltpu.VMEM((gw, nseq * N), F32), pltpu.VMEM((SUBLANES, gw + 2 * N), F32)],
        compiler_params=_params(3), name="ssd",
    )(*args)
    return y, st


def _ffn(x, norm_w, w_gate, w_up, w_down, layer):
    h = _rmsnorm_pair(x, norm_w, BF16)
    a = _mm_pair([h], [_W(w_gate, layer), _W(w_up, layer)], out_dtype=BF16, mode="swiglu", name="ffn_gate_up")
    return _mm_pair([a], [_W(w_down, layer)], res=x, name="ffn_down")


def kernel(x_prompt, x_sample, cache_k, cache_v, cache_logf, page_table, state_sconv, state_mconv, state_ssm, norm_mix, norm_ffn, norm_final, w_in_even, b_forget, w_sconv, w_out_even, w_in_ssm, w_mconv, b_mconv, dt_bias, a_log, d_skip, w_gnorm, w_out_ssm, w_gate, w_up, w_down):
    bp, seq, d = x_prompt.shape
    bs, dec_seq, _ = x_sample.shape
    mp, ms = bp * seq, bs * dec_seq
    nh, dh = ATT_HEADS, ATT_HEAD_DIM
    att = nh * dh
    sc_ch = state_sconv.shape[-1]
    x = (x_prompt.reshape(mp, d), x_sample.reshape(ms, d))

    i = 0
    h = _rmsnorm_pair(x, norm_mix[0], BF16)
    c0 = 3 * sc_ch
    w_in_t = jnp.swapaxes(w_in_even, 1, 2)
    win = lambda col0, n: _W(w_in_t, i, col0=col0, n=n, transposed=True)
    gcu = _mm_pair([h], [win(0, c0)], name="in_even_conv")
    q = _mm_pair([h], [win(c0, att)], out_dtype=BF16, scale=dh ** -0.5 * LOG2E, name="in_even_q")
    k = _mm_pair([h], [win(c0 + att, att)], name="in_even_k")
    v = _mm_pair([h], [win(c0 + 2 * att, att)], name="in_even_v")
    lf_p, fc_p, lf_s, fc_s = _forget_pair(h, w_in_even[i, :, c0 + 3 * att:], b_forget[i], bp, seq, dec_seq)
    logf_p = lf_p[:, :nh].reshape(bp, seq, nh)
    logf_s = lf_s[:, :nh].reshape(bs, dec_seq, nh)
    y_a, sconv_p, sconv_s = _sconv_pair(gcu, state_sconv[i], w_sconv[i], bp, seq, dec_seq)
    yb_p = _fox_prompt(q[0], k[0], v[0], fc_p[:, :nh].reshape(bp, seq, nh), bp, seq)
    yb_s = _fox_sample(q[1], k[1], v[1], fc_s[:, :nh].reshape(bs, dec_seq, nh),
                       cache_k[i], cache_v[i], cache_logf[i], page_table, dec_seq)
    x = _mm_pair([y_a, (yb_p, yb_s)],
                 [_W(w_out_even, i, row0=0, k=sc_ch), _W(w_out_even, i, row0=sc_ch, k=att)], res=x, name="out_even")
    x = _ffn(x, norm_ffn[0], w_gate, w_up, w_down, 0)

    j = 0
    G = SSM_GROUPS
    n_heads = a_log.shape[1]
    rg = n_heads // G
    inner = n_heads * SSM_HEAD_DIM
    conv_dim = state_mconv.shape[-1]
    h = _rmsnorm_pair(x, norm_mix[1], BF16)
    w_ssm_t = jnp.swapaxes(w_in_ssm, 1, 2)
    z = _mm_pair([h], [_W(w_ssm_t, j, n=inner, transposed=True)], name="in_ssm_z")
    xbc = _mm_pair([h], [_W(w_ssm_t, j, col0=inner, n=conv_dim, transposed=True)], name="in_ssm_xbc")
    assert n_heads <= LANES
    w_dt = jnp.zeros((d, LANES), BF16).at[:, :n_heads].set(w_in_ssm[j, :, inner + conv_dim:].astype(BF16))
    b_dt = jnp.zeros((1, LANES), F32).at[0, :n_heads].set(dt_bias[j].astype(F32))
    dt = _mm_pair([h], [_W(w_dt)], mode="softplus", bias=b_dt, name="in_ssm_dt")
    a_pad = jnp.zeros((G, LANES), F32).at[:, :rg].set(-jnp.exp(a_log[j].astype(F32)).reshape(G, rg)).reshape(1, G * LANES)
    d_exp = jnp.repeat(d_skip[j].astype(F32), SSM_HEAD_DIM).reshape(1, inner)
    gn_w = w_gnorm[j].astype(F32).reshape(1, inner)
    hw = SSM_CONV - 1
    mconv_p = xbc[0].reshape(bp, seq, conv_dim)[:, seq - hw:, :]
    mconv_s = xbc[1].reshape(bs, dec_seq, conv_dim)[:, dec_seq - hw:, :]
    yp, ssm_p = _ssd(xbc[0], z[0], dt[0], w_mconv[j], b_mconv[j], a_pad, d_exp, gn_w, None, None, bp, seq)
    init = state_ssm[j].astype(F32).reshape(bs, G, rg * SSM_HEAD_DIM, D_STATE)
    ysm, ssm_s = _ssd(xbc[1], z[1], dt[1], w_mconv[j], b_mconv[j], a_pad, d_exp, gn_w,
                      _pad_hist(state_mconv[j], dec_seq), init, bs, dec_seq)
    x = _mm_pair([(yp, ysm)], [_W(w_out_ssm, j)], res=x, name="out_ssm")
    x = _ffn(x, norm_ffn[1], w_gate, w_up, w_down, 1)

    y = _rmsnorm_pair(x, norm_final, F32)
    st_shape = (n_heads, SSM_HEAD_DIM, D_STATE)
    return (y[0].reshape(bp, seq, d), y[1].reshape(bs, dec_seq, d),
            k[0].reshape(1, bp, seq, nh, dh), v[0].reshape(1, bp, seq, nh, dh), logf_p[None],
            sconv_p[None], mconv_p[None], ssm_p.reshape((1, bp) + st_shape),
            k[1].reshape(1, bs, dec_seq, nh, dh), v[1].reshape(1, bs, dec_seq, nh, dh), logf_s[None],
            sconv_s[None], mconv_s[None], ssm_s.reshape((1, bs) + st_shape))
```

```python
import functools

import jax
import jax.numpy as jnp
from jax import lax
from jax.experimental import pallas as pl
from jax.experimental.pallas import tpu as pltpu

F32 = jnp.float32
BF16 = jnp.bfloat16

SC_WIDTH = 3
ATT_HEADS = 8
ATT_HEAD_DIM = 128
SSM_HEAD_DIM = 64
SSM_GROUPS = 8
D_STATE = 128
SSM_CONV = 4
SSM_CHUNK = 128
EPS = 1e-6

LANES = 128
SUBLANES = 8
VMEM_LIMIT_BYTES = 56 * 1024 * 1024
NEG_BIG = -1e30
LOG2E = 1.4426950408889634

HIGHEST = lax.Precision.HIGHEST
NT_DIMS = (((1,), (1,)), ((), ()))
TN_DIMS = (((0,), (0,)), ((), ()))


def _params(n_grid_dims):
    return pltpu.CompilerParams(dimension_semantics=("arbitrary",) * n_grid_dims,
                                vmem_limit_bytes=VMEM_LIMIT_BYTES)


def _row_tile(mp, pref):
    t = pref
    while mp % t:
        t //= 2
    assert t >= SUBLANES, (mp, pref)
    return t


def _rmsnorm_body(nt, xp, xs, w, op, os):
    i = pl.program_id(0)

    def norm(x):
        y = x * lax.rsqrt(jnp.mean(x * x, axis=-1, keepdims=True) + EPS)
        return (y * w[...]).astype(op.dtype)

    @pl.when(i < nt)
    def _():
        op[...] = norm(xp[...])

    @pl.when(i == nt)
    def _():
        os[...] = norm(xs[...])


def _rmsnorm_pair(x, w, out_dtype):
    xp, xs = x
    mp, d = xp.shape
    ms = xs.shape[0]
    tm = _row_tile(mp, 1024)
    nt = mp // tm
    pidx = lambda i: (jnp.minimum(i, nt - 1), 0)
    return pl.pallas_call(
        functools.partial(_rmsnorm_body, nt),
        grid=(nt + 1,),
        in_specs=[pl.BlockSpec((tm, d), pidx), pl.BlockSpec((ms, d), lambda i: (0, 0)),
                  pl.BlockSpec((1, d), lambda i: (0, 0))],
        out_specs=[pl.BlockSpec((tm, d), pidx), pl.BlockSpec((ms, d), lambda i: (0, 0))],
        out_shape=[jax.ShapeDtypeStruct((mp, d), out_dtype), jax.ShapeDtypeStruct((ms, d), out_dtype)],
        compiler_params=_params(1), name="rmsnorm",
    )(xp, xs, w.reshape(1, d).astype(F32))


def _softplus(x):
    return jnp.maximum(x, 0.0) + jnp.log1p(jnp.exp(-jnp.abs(x)))


def _silu(x):
    return x * (1.0 / (1.0 + jnp.exp(-x)))


def _split3(v):
    hi = v.astype(BF16)
    r1 = v - hi.astype(F32)
    mid = r1.astype(BF16)
    lo = (r1 - mid.astype(F32)).astype(BF16)
    return hi, mid, lo


class _W:
    def __init__(self, arr, layer=None, row0=0, k=None, col0=0, n=None, transposed=False):
        self.arr, self.layer, self.row0, self.col0, self.transposed = arr, layer, row0, col0, transposed
        kdim, ndim = (-1, -2) if transposed else (-2, -1)
        self.k = arr.shape[kdim] if k is None else k
        self.n = arr.shape[ndim] if n is None else n

    def block(self, tn):
        return (tn, self.k) if self.transposed else (self.k, tn)

    def spec(self, tn):
        assert self.row0 % self.k == 0 and self.col0 % tn == 0
        rb, cb = self.row0 // self.k, self.col0 // tn
        if self.transposed:
            idx = lambda j, i: (cb + j, rb)
        else:
            idx = lambda j, i: (rb, cb + j)
        if self.layer is None:
            return pl.BlockSpec(self.block(tn), idx)
        layer = self.layer
        return pl.BlockSpec((None,) + self.block(tn), lambda j, i: (layer,) + idx(j, i))


def _mm_body(n_lhs, mode, has_res, has_bias, scale, w_is_f32, w_transposed, dup_bf16, *refs):
    i = pl.program_id(1)

    def mm(x, idx):
        dims = NT_DIMS if w_transposed[idx] else (((1,), (0,)), ((), ()))
        return lax.dot_general(x, wrefs[idx][...], dims, preferred_element_type=F32)

    xrefs = refs[:2 * n_lhs]
    n_w = len(w_is_f32)
    wrefs = list(refs[2 * n_lhs:2 * n_lhs + n_w])
    pos = 2 * n_lhs + n_w
    rp = rs = bias = None
    if has_res:
        rp, rs = refs[pos], refs[pos + 1]
        pos += 2
    if has_bias:
        bias = refs[pos]
        pos += 1
    op, os = refs[pos], refs[pos + 1]
    pos += 2
    dup = None
    if dup_bf16:
        dup = (refs[pos], refs[pos + 1])
        pos += 2
    scratch = list(refs[pos:])

    for idx in range(n_w):
        if w_is_f32[idx]:
            wsrc, wdst = wrefs[idx], scratch.pop(0)

            @pl.when(i == 0)
            def _(wsrc=wsrc, wdst=wdst):
                wdst[...] = wsrc[...].astype(BF16)

            wrefs[idx] = wdst

    def compute(which, out, res):
        if mode == "swiglu":
            x = xrefs[which][...]
            acc = _silu(mm(x, 0)) * mm(x, 1)
        else:
            acc = mm(xrefs[which][...], 0)
            for k in range(1, n_lhs):
                acc += mm(xrefs[2 * k + which][...], k)
        if scale is not None:
            acc = acc * scale
        if has_bias:
            acc = acc + bias[...]
        if mode == "softplus":
            acc = _softplus(acc)
        if has_res:
            acc = res[...] + acc
        out[...] = acc.astype(out.dtype)
        if dup_bf16:
            dup[which][...] = acc.astype(BF16)

    @pl.when(i == 0)
    def _():
        compute(1, os, rs)

    @pl.when(i > 0)
    def _():
        compute(0, op, rp)


def _mm_pair(xs_list, ws_list, *, out_dtype=F32, mode="plain", res=None, bias=None, scale=None, name="matmul",
             dup_bf16=False):
    n_lhs = len(xs_list)
    mp = xs_list[0][0].shape[0]
    ms = xs_list[0][1].shape[0]
    n = ws_list[0].n
    kmax = max(w.k for w in ws_list)
    tm = _row_tile(mp, 1024 if kmax <= 4096 else 512)
    nt = mp // tm
    if n % 1024 == 0 and kmax <= 2048 and mode != "swiglu":
        tn = 1024
    elif n % 512 == 0:
        tn = 512
    else:
        assert n % LANES == 0 and n <= 512, n
        tn = n
    pidx_x = lambda j, i: (jnp.maximum(i - 1, 0), 0)
    pidx_o = lambda j, i: (jnp.maximum(i - 1, 0), j)
    in_specs, args = [], []
    for xp, xs in xs_list:
        kk = xp.shape[1]
        in_specs += [pl.BlockSpec((tm, kk), pidx_x), pl.BlockSpec((ms, kk), lambda j, i: (0, 0))]
        args += [xp, xs]
    w_is_f32, scratch = [], []
    for w in ws_list:
        in_specs.append(w.spec(tn))
        args.append(w.arr)
        w_is_f32.append(w.arr.dtype != BF16)
        if w_is_f32[-1]:
            scratch.append(pltpu.VMEM(w.block(tn), BF16))
    if res is not None:
        in_specs += [pl.BlockSpec((tm, tn), pidx_o), pl.BlockSpec((ms, tn), lambda j, i: (0, j))]
        args += [res[0], res[1]]
    if bias is not None:
        in_specs.append(pl.BlockSpec((1, tn), lambda j, i: (0, j)))
        args.append(bias.reshape(1, n).astype(F32))
    out_specs = [pl.BlockSpec((tm, tn), pidx_o), pl.BlockSpec((ms, tn), lambda j, i: (0, j))]
    out_shape = [jax.ShapeDtypeStruct((mp, n), out_dtype), jax.ShapeDtypeStruct((ms, n), out_dtype)]
    if dup_bf16:
        out_specs += [pl.BlockSpec((tm, tn), pidx_o), pl.BlockSpec((ms, tn), lambda j, i: (0, j))]
        out_shape += [jax.ShapeDtypeStruct((mp, n), BF16), jax.ShapeDtypeStruct((ms, n), BF16)]
    outs = pl.pallas_call(
        functools.partial(_mm_body, n_lhs, mode, res is not None, bias is not None, scale, tuple(w_is_f32),
                          tuple(w.transposed for w in ws_list), dup_bf16),
        grid=(n // tn, nt + 1),
        in_specs=in_specs, out_specs=out_specs, out_shape=out_shape,
        scratch_shapes=scratch,
        compiler_params=_params(2), name=name,
    )(*args)
    if dup_bf16:
        return (outs[0], outs[1]), (outs[2], outs[3])
    return outs[0], outs[1]


def _log_sigmoid(x):
    return -_softplus(-x)


def _forget_body(nb, seq, dec_seq, ck, xp, xs, w, b, lp, fp, ls, fs):
    i = pl.program_id(0)

    @pl.when(i < nb)
    def _():
        r = lax.broadcasted_iota(jnp.int32, (ck, ck), 0)
        c = lax.broadcasted_iota(jnp.int32, (ck, ck), 1)
        tri = (r >= c).astype(F32)
        carry = jnp.zeros((1, LANES), F32)
        for t in range(seq // ck):
            f = jnp.dot(xp[t * ck:(t + 1) * ck, :], w[...], preferred_element_type=F32) + b[...]
            lf = _log_sigmoid(f)
            lp[t * ck:(t + 1) * ck, :] = lf
            cs = jnp.dot(tri, lf, preferred_element_type=F32, precision=HIGHEST) + carry
            fp[t * ck:(t + 1) * ck, :] = cs * LOG2E
            carry = cs[ck - 1:ck, :]

    @pl.when(i == nb)
    def _():
        ms = xs.shape[0]
        r = lax.broadcasted_iota(jnp.int32, (ms, ms), 0)
        c = lax.broadcasted_iota(jnp.int32, (ms, ms), 1)
        tri = ((r // dec_seq == c // dec_seq) & (r >= c)).astype(F32)
        f = jnp.dot(xs[...], w[...], preferred_element_type=F32) + b[...]
        lf = _log_sigmoid(f)
        ls[...] = lf
        fs[...] = jnp.dot(tri, lf, preferred_element_type=F32, precision=HIGHEST) * LOG2E


def _forget_pair(h, w_f, b_f, nb, seq, dec_seq):
    hp, hs = h
    mp, d = hp.shape
    ms = hs.shape[0]
    assert ms <= 512 and ms % SUBLANES == 0
    ck = _row_tile(seq, 256)
    n_h = w_f.shape[1]
    wpad = jnp.zeros((d, LANES), BF16).at[:, :n_h].set(w_f.astype(BF16))
    bpad = jnp.zeros((1, LANES), F32).at[0, :n_h].set(b_f.astype(F32))
    pidx = lambda i: (jnp.minimum(i, nb - 1), 0)
    cidx = lambda i: (0, 0)
    return pl.pallas_call(
        functools.partial(_forget_body, nb, seq, dec_seq, ck),
        grid=(nb + 1,),
        in_specs=[pl.BlockSpec((seq, d), pidx), pl.BlockSpec((ms, d), cidx),
                  pl.BlockSpec((d, LANES), cidx), pl.BlockSpec((1, LANES), cidx)],
        out_specs=[pl.BlockSpec((seq, LANES), pidx), pl.BlockSpec((seq, LANES), pidx),
                   pl.BlockSpec((ms, LANES), cidx), pl.BlockSpec((ms, LANES), cidx)],
        out_shape=[jax.ShapeDtypeStruct((mp, LANES), F32), jax.ShapeDtypeStruct((mp, LANES), F32),
                   jax.ShapeDtypeStruct((ms, LANES), F32), jax.ShapeDtypeStruct((ms, LANES), F32)],
        compiler_params=_params(1), name="forget_gates",
    )(hp, hs, wpad, bpad)


def _shifted(x, hist, k, width, t_in_seq):
    n = x.shape[0]
    xk = pltpu.roll(x, k, axis=0)
    if hist is None:
        return xk
    back = (width - 1 - k) % n
    hk = hist if back == 0 else pltpu.roll(hist, n - back, axis=0)
    return jnp.where(t_in_seq >= k, xk, hk)


def _conv_rows(x, hist, wref, width, t_in_seq):
    acc = x * wref[width - 1:width, :]
    for k in range(1, width):
        acc += _shifted(x, hist, k, width, t_in_seq) * wref[width - 1 - k:width - k, :]
    return acc


def _sconv_body(nt, tiles_per_seq, width, dec_seq, gbp, gcp, up, gbs, gcs, us, hs, w, yp, ys, zp_tail, zs, carry):
    i = pl.program_id(1)

    @pl.when(i < nt)
    def _():
        t0 = i % tiles_per_seq
        x = gcp[...] * up[...]
        ts = x.shape[0]

        @pl.when(t0 == 0)
        def _():
            carry[...] = jnp.zeros_like(carry)

        yp[...] = (gbp[...] * _conv_rows(x, None, w, width, None)).astype(yp.dtype)
        head = x[:SUBLANES, :]
        t8 = lax.broadcasted_iota(jnp.int32, head.shape, 0)
        yp[:SUBLANES, :] = (gbp[:SUBLANES, :] * _conv_rows(head, carry[...], w, width, t8)).astype(yp.dtype)
        tail = x[ts - SUBLANES:, :]
        carry[...] = pltpu.roll(tail, width - 1, axis=0)
        zp_tail[0] = tail

    @pl.when(i == nt)
    def _():
        x = gcs[...] * us[...]
        t = lax.broadcasted_iota(jnp.int32, x.shape, 0) % dec_seq
        ys[...] = (gbs[...] * _conv_rows(x, hs[...], w, width, t)).astype(ys.dtype)
        zs[...] = x


def _pad_hist(state, dec_seq):
    b, hw, c = state.shape
    assert dec_seq >= hw
    return jnp.concatenate([state.astype(F32), jnp.zeros((b, dec_seq - hw, c), F32)], axis=1).reshape(b * dec_seq, c)


def _sconv_pair(gcu, state, w, nb, seq, dec_seq):
    gp, gs = gcu
    mp = gp.shape[0]
    ms = gs.shape[0]
    ch = gp.shape[1] // 3
    tc = 1024 if ch % 1024 == 0 else 512
    ncb = ch // tc
    ts = _row_tile(seq, 512)
    tps = seq // ts
    nt = mp // ts
    pr = lambda j, i: jnp.minimum(i, nt - 1)
    in_specs = []
    for part in range(3):
        in_specs.append(pl.BlockSpec((ts, tc), lambda j, i, part=part: (pr(j, i), part * ncb + j)))
    for part in range(3):
        in_specs.append(pl.BlockSpec((ms, tc), lambda j, i, part=part: (0, part * ncb + j)))
    in_specs += [pl.BlockSpec((ms, tc), lambda j, i: (0, j)), pl.BlockSpec((SC_WIDTH, tc), lambda j, i: (0, j))]
    out_specs = [pl.BlockSpec((ts, tc), lambda j, i: (pr(j, i), j)),
                 pl.BlockSpec((ms, tc), lambda j, i: (0, j)),
                 pl.BlockSpec((1, SUBLANES, tc), lambda j, i: (pr(j, i) // tps, 0, j)),
                 pl.BlockSpec((ms, tc), lambda j, i: (0, j))]
    out_shape = [jax.ShapeDtypeStruct((mp, ch), BF16), jax.ShapeDtypeStruct((ms, ch), BF16),
                 jax.ShapeDtypeStruct((nb, SUBLANES, ch), F32), jax.ShapeDtypeStruct((ms, ch), F32)]
    yp, ys, zp_tail, zs = pl.pallas_call(
        functools.partial(_sconv_body, nt, tps, SC_WIDTH, dec_seq),
        grid=(ncb, nt + 1), in_specs=in_specs, out_specs=out_specs, out_shape=out_shape,
        scratch_shapes=[pltpu.VMEM((SUBLANES, tc), F32)],
        compiler_params=_params(2), name="short_conv",
    )(gp, gp, gp, gs, gs, gs, _pad_hist(state, dec_seq), w.astype(F32))
    hw = SC_WIDTH - 1
    new_p = zp_tail[:, SUBLANES - hw:, :]
    new_s = zs.reshape(ms // dec_seq, dec_seq, ch)[:, dec_seq - hw:, :]
    return (yp, ys), new_p, new_s


def _fox_prompt_body(tq, hps, q, k, v, fr, o, m_s, l_s, acc_s):
    qi = pl.program_id(2)
    dh = ATT_HEAD_DIM
    m_s[...] = jnp.full_like(m_s, NEG_BIG)
    l_s[...] = jnp.zeros_like(l_s)
    acc_s[...] = jnp.zeros_like(acc_s)

    def step(ki, masked):
        start = pl.multiple_of(ki * tq, tq)
        for hh in range(hps):
            cols = slice(hh * dh, (hh + 1) * dh)
            kb = k[pl.ds(start, tq), cols].astype(BF16)
            vb = v[pl.ds(start, tq), cols].astype(BF16)
            s = lax.dot_general(q[:, cols], kb, NT_DIMS, preferred_element_type=F32) - fr[0, hh, ki]
            if masked:
                r = lax.broadcasted_iota(jnp.int32, s.shape, 0)
                c = lax.broadcasted_iota(jnp.int32, s.shape, 1)
                s = jnp.where(r >= c, s, NEG_BIG)
            m_prev = m_s[hh]
            m_new = jnp.maximum(m_prev, jnp.max(s, axis=-1, keepdims=True))
            p = jnp.exp2(s - jnp.tile(m_new, (1, tq // dh)))
            alpha = jnp.exp2(m_prev - m_new)
            l_s[hh] = alpha * l_s[hh] + jnp.sum(p, axis=-1, keepdims=True)
            acc_s[hh] = alpha * acc_s[hh] + jnp.dot(p.astype(BF16), vb, preferred_element_type=F32)
            m_s[hh] = m_new

    def loop_body(ki, carry):
        step(ki, False)
        return carry

    lax.fori_loop(0, qi, loop_body, 0)
    step(qi, True)
    for hh in range(hps):
        o[:, hh * dh:(hh + 1) * dh] = (acc_s[hh] / l_s[hh]).astype(o.dtype)


def _fox_prompt(q, k, v, fcum, nb, seq):
    mp = q.shape[0]
    dh = ATT_HEAD_DIM
    nh = q.shape[1] // dh
    tq = _row_tile(seq, 512)
    nq = seq // tq
    frow = jnp.transpose(fcum, (0, 2, 1)).reshape(nb, nh, nq, 1, tq)
    hps = 4 if nh % 4 == 0 else 1
    gw = hps * dh
    return pl.pallas_call(
        functools.partial(_fox_prompt_body, tq, hps),
        grid=(nb, nh // hps, nq),
        in_specs=[pl.BlockSpec((tq, gw), lambda b, h, i: (b * nq + i, h)),
                  pl.BlockSpec((seq, gw), lambda b, h, i: (b, h)),
                  pl.BlockSpec((seq, gw), lambda b, h, i: (b, h)),
                  pl.BlockSpec((1, hps, nq, 1, tq), lambda b, h, i: (b, h, 0, 0, 0))],
        out_specs=pl.BlockSpec((tq, gw), lambda b, h, i: (b * nq + i, h)),
        out_shape=jax.ShapeDtypeStruct((mp, nh * dh), BF16),
        scratch_shapes=[pltpu.VMEM((hps, tq, dh), F32), pltpu.VMEM((hps, tq, dh), F32),
                        pltpu.VMEM((hps, tq, dh), F32)],
        compiler_params=_params(3), name="fox_prompt",
    )(q, k, v, frow)


def _past_bias_body(n_pages, page, nh, *refs):
    lf_refs = refs[1:1 + n_pages]
    out = refs[1 + n_pages]
    rows, later_in_page, spread_heads = refs[2 + n_pages:]
    w = page * nh
    npp = rows.shape[0]

    @pl.when(pl.program_id(0) == 0)
    def _():
        a = lax.broadcasted_iota(jnp.int32, (w, w), 0)
        c = lax.broadcasted_iota(jnp.int32, (w, w), 1)
        later_in_page[...] = (((a % nh) == (c % nh)) & ((a // nh) > (c // nh))).astype(BF16)
        k = lax.broadcasted_iota(jnp.int32, spread_heads.shape, 0)
        c2 = lax.broadcasted_iota(jnp.int32, spread_heads.shape, 1)
        spread_heads[...] = ((k % LANES) == (c2 % nh)).astype(BF16)
        rows[...] = jnp.zeros_like(rows)

    for p in range(n_pages):
        rows[p:p + 1, :] = lf_refs[p][0]
    lf = rows[...]
    within3 = jnp.dot(jnp.concatenate(_split3(lf), axis=0), later_in_page[...], preferred_element_type=F32)
    within = within3[:npp] + within3[npp:2 * npp] + within3[2 * npp:]
    first = (within + lf)[:, :LANES]
    totals = jnp.dot(jnp.concatenate(_split3(first), axis=1), spread_heads[...], preferred_element_type=F32)
    pr = lax.broadcasted_iota(jnp.int32, (npp, npp), 0)
    pc = lax.broadcasted_iota(jnp.int32, (npp, npp), 1)
    later_pages = (pc > pr).astype(BF16)
    after3 = jnp.dot(later_pages, jnp.concatenate(_split3(totals), axis=1), preferred_element_type=F32)
    after = after3[:, :w] + after3[:, w:2 * w] + after3[:, 2 * w:]
    out[0] = (within + after)[:n_pages, :] * LOG2E


def _past_bias(cache_lf, page_table):
    n_phys, page, nh = cache_lf.shape
    nb, n_pages = page_table.shape
    w = page * nh
    np_pad = -(-n_pages // LANES) * LANES
    lf = cache_lf.reshape(n_phys, 1, w)
    in_specs = [pl.BlockSpec((1, 1, w), lambda b, pt, p=p: (pt[b, p], 0, 0)) for p in range(n_pages)]
    grid_spec = pltpu.PrefetchScalarGridSpec(
        num_scalar_prefetch=1, grid=(nb,), in_specs=in_specs,
        out_specs=pl.BlockSpec((1, n_pages, w), lambda b, pt: (b, 0, 0)),
        scratch_shapes=[pltpu.VMEM((np_pad, w), F32), pltpu.VMEM((w, w), BF16), pltpu.VMEM((3 * LANES, w), BF16)])
    return pl.pallas_call(
        functools.partial(_past_bias_body, n_pages, page, nh),
        grid_spec=grid_spec, out_shape=jax.ShapeDtypeStruct((nb, n_pages, w), F32),
        compiler_params=_params(1), name="fox_past_bias",
    )(page_table, *([lf] * n_pages))


def _fox_sample_body(pps, n_steps, nh, dec_seq, *refs):
    qall, knew, vnew, frow_new, rb = refs[1:6]
    krefs = refs[6:6 + pps]
    vrefs = refs[6 + pps:6 + 2 * pps]
    o = refs[6 + 2 * pps]
    m_s, l_s, acc_s, s_scr = refs[7 + 2 * pps:]
    step = pl.program_id(1)
    qv = qall[0]
    nrow = qv.shape[0]

    @pl.when(step == 0)
    def _():
        kn = knew[0].astype(BF16)
        g = lax.dot_general(qv, kn, NT_DIMS, preferred_element_type=F32)
        r = lax.broadcasted_iota(jnp.int32, g.shape, 0)
        c = lax.broadcasted_iota(jnp.int32, g.shape, 1)
        valid = ((c % nh) == (r // dec_seq)) & ((c // nh) <= (r % dec_seq))
        s = jnp.where(valid, g - frow_new[0], NEG_BIG)
        m = jnp.max(s, axis=-1, keepdims=True)
        p = jnp.exp2(s - m)
        m_s[...] = m
        l_s[...] = jnp.sum(p, axis=-1, keepdims=True)
        acc_s[...] = jnp.dot(p.astype(BF16), vnew[0].astype(BF16), preferred_element_type=F32)

    lanes = krefs[0].shape[1]
    r = lax.broadcasted_iota(jnp.int32, (nrow, lanes), 0)
    c = lax.broadcasted_iota(jnp.int32, (nrow, lanes), 1)
    head_ok = (c % nh) == (r // dec_seq)
    m_prev = m_s[...]
    m_new = m_prev
    for i in range(pps):
        g = lax.dot_general(qv, krefs[i][0].astype(BF16), NT_DIMS, preferred_element_type=F32)
        s = jnp.where(head_ok, g + rb[0, i:i + 1, :], NEG_BIG)
        s_scr[i] = s
        m_new = jnp.maximum(m_new, jnp.max(s, axis=-1, keepdims=True))
    alpha = jnp.exp2(m_prev - m_new)
    psum = acc = None
    for i in range(pps):
        p = jnp.exp2(s_scr[i] - m_new)
        pv = jnp.dot(p.astype(BF16), vrefs[i][0].astype(BF16), preferred_element_type=F32)
        ps = jnp.sum(p, axis=-1, keepdims=True)
        psum = ps if psum is None else psum + ps
        acc = pv if acc is None else acc + pv
    l_s[...] = alpha * l_s[...] + psum
    acc_s[...] = alpha * acc_s[...] + acc
    m_s[...] = m_new

    @pl.when(step == n_steps - 1)
    def _():
        o[0] = (acc_s[...] / l_s[...]).astype(o.dtype)


def _fox_sample(q, k, v, fn, cache_k, cache_v, cache_lf, page_table, dec_seq):
    ms = q.shape[0]
    nb = ms // dec_seq
    nh, dh = ATT_HEADS, ATT_HEAD_DIM
    n_phys, page = cache_k.shape[0], cache_k.shape[1]
    n_pages = page_table.shape[1]
    pps = 16
    while n_pages % pps:
        pps //= 2
    n_steps = n_pages // pps
    nrow = nh * dec_seq
    rbias = _past_bias(cache_lf, page_table)
    qall = jnp.transpose(q.reshape(nb, dec_seq, nh, dh), (0, 2, 1, 3)).reshape(nb, nrow, dh)
    knew = k.reshape(nb, dec_seq * nh, dh)
    vnew = v.reshape(nb, dec_seq * nh, dh)
    frow_new = fn.reshape(nb, 1, dec_seq * nh)
    ck = cache_k.reshape(n_phys, page * nh, dh)
    cv = cache_v.reshape(n_phys, page * nh, dh)
    per_b = lambda shape: pl.BlockSpec((1,) + shape, lambda b, s, pt: (b, 0, 0))
    in_specs = [per_b((nrow, dh)), per_b((dec_seq * nh, dh)), per_b((dec_seq * nh, dh)),
                per_b((1, dec_seq * nh)),
                pl.BlockSpec((1, pps, page * nh), lambda b, s, pt: (b, s, 0))]
    for _ in range(2):
        in_specs += [pl.BlockSpec((1, page * nh, dh), lambda b, s, pt, i=i: (pt[b, s * pps + i], 0, 0))
                     for i in range(pps)]
    grid_spec = pltpu.PrefetchScalarGridSpec(
        num_scalar_prefetch=1, grid=(nb, n_steps), in_specs=in_specs,
        out_specs=pl.BlockSpec((1, nrow, dh), lambda b, s, pt: (b, 0, 0)),
        scratch_shapes=[pltpu.VMEM((nrow, 1), F32), pltpu.VMEM((nrow, 1), F32), pltpu.VMEM((nrow, dh), F32),
                        pltpu.VMEM((pps, nrow, page * nh), F32)])
    o = pl.pallas_call(
        functools.partial(_fox_sample_body, pps, n_steps, nh, dec_seq),
        grid_spec=grid_spec, out_shape=jax.ShapeDtypeStruct((nb, nrow, dh), BF16),
        compiler_params=_params(2), name="fox_sample",
    )(page_table, qall, knew, vnew, frow_new, rbias, *([ck] * pps), *([cv] * pps))
    return jnp.transpose(o.reshape(nb, nh, dec_seq, dh), (0, 2, 1, 3)).reshape(ms, nh * dh)


def _ssd_body(t_in, nseq, nsub, carried, xs_r, z_r, b_r, c_r, dt_r, gc_r, sel_r, ex_r, *rest):
    if carried:
        hx_r, hb_r, hc_r, s0_r, y_o, st_o, st, carry = rest
    else:
        hx_r = hb_r = hc_r = s0_r = None
        y_o, st_o, st, carry = rest
    ci = pl.program_id(2)
    nc = pl.num_programs(2)
    L = t_in * nseq
    N = D_STATE
    gw = xs_r.shape[1]

    @pl.when(ci == 0)
    def _():
        if carried:
            for s in range(nseq):
                st[:, s * N:(s + 1) * N] = s0_r[s, 0]
        else:
            st[...] = jnp.zeros_like(st)
            carry[...] = jnp.zeros_like(carry)

    d_skip = gc_r[SSM_CONV + 1:SSM_CONV + 2, 0:gw]
    a_rate = gc_r[SSM_CONV + 1:SSM_CONV + 2, gw:gw + LANES]
    gain = gc_r[SSM_CONV + 2:SSM_CONV + 3, 0:gw]

    def conv_act(raw_r, hist_r, lo, hi):
        x = raw_r[...]
        w_r = gc_r[0:SSM_CONV, lo:hi]
        bias = gc_r[SSM_CONV:SSM_CONV + 1, lo:hi]
        if carried:
            t = lax.broadcasted_iota(jnp.int32, x.shape, 0) % t_in
            y = _conv_rows(x, hist_r[...], w_r, SSM_CONV, t)
        else:
            y = _conv_rows(x, None, w_r, SSM_CONV, None)
            head = x[:SUBLANES, :]
            t8 = lax.broadcasted_iota(jnp.int32, head.shape, 0)
            y8 = _conv_rows(head, carry[:, lo:hi], w_r, SSM_CONV, t8)
            y = jnp.concatenate([y8, y[SUBLANES:, :]], axis=0)
            carry[:, lo:hi] = pltpu.roll(x[x.shape[0] - SUBLANES:, :], SSM_CONV - 1, axis=0)
        return _silu(y + bias)

    xs_blk = conv_act(xs_r, hx_r, 0, gw)
    bmf_blk = conv_act(b_r, hb_r, gw, gw + N)
    cmf_blk = conv_act(c_r, hc_r, gw + N, gw + 2 * N)
    dt_blk = jnp.dot(jnp.concatenate(_split3(dt_r[...]), axis=1), sel_r[...], preferred_element_type=F32)
    r = lax.broadcasted_iota(jnp.int32, (L, L), 0)
    c = lax.broadcasted_iota(jnp.int32, (L, L), 1)
    for u in range(nsub):
        rows = slice(u * L, (u + 1) * L)
        _ssd_chunk(t_in, nseq, r, c, xs_blk[rows], bmf_blk[rows], cmf_blk[rows], dt_blk[rows], z_r[rows, :],
                   a_rate, d_skip, gain, ex_r, st, y_o.at[rows, :])

    @pl.when(ci == nc - 1)
    def _():
        for s in range(nseq):
            st_o[s, 0] = st[:, s * N:(s + 1) * N]


def _ssd_chunk(t_in, nseq, r, c, xs, bmf, cmf, dt, z, a_rate, d_skip, gain, ex_r, st, y_o):
    L = t_in * nseq
    P, N = SSM_HEAD_DIM, D_STATE
    R = xs.shape[1] // P
    bm, cm = bmf.astype(BF16), cmf.astype(BF16)
    da = dt * (a_rate * LOG2E)
    if nseq == 1:
        tri = r >= c
        acum = jnp.dot(tri.astype(F32), da, preferred_element_type=F32, precision=HIGHEST)
        a_tot = acum[L - 1:L, :]
    else:
        same = (r // t_in) == (c // t_in)
        tri = same & (r >= c)
        sums = jnp.dot(jnp.concatenate([tri.astype(F32), same.astype(F32)], axis=0), da,
                       preferred_element_type=F32, precision=HIGHEST)
        acum, a_tot = sums[:L], sums[L:]
    acum_t = acum.T
    dt_t = dt.T.astype(BF16)
    cols = jnp.concatenate([dt * jnp.exp2(a_tot - acum), jnp.exp2(acum)], axis=0)
    wide = jnp.dot(jnp.concatenate(_split3(cols), axis=1), ex_r[...], preferred_element_type=F32)
    w_state, w_off = wide[:L], wide[L:]
    cb = lax.dot_general(cm, bm, NT_DIMS, preferred_element_type=F32).astype(BF16)
    if nseq == 1:
        cblk, bblk = cm, bm
        decay_t = acum_t
    else:
        seq_of_row = lax.broadcasted_iota(jnp.int32, (L, N), 0) // t_in
        cblk = jnp.concatenate([jnp.where(seq_of_row == s, cmf, 0.0) for s in range(nseq)], axis=1).astype(BF16)
        bblk = jnp.concatenate([jnp.where(seq_of_row == s, bmf, 0.0) for s in range(nseq)], axis=1).astype(BF16)
        decay_t = a_tot.T
    s_prev = st[...]
    y_off = lax.dot_general(cblk, s_prev.astype(BF16), NT_DIMS, preferred_element_type=F32) * w_off
    xb = xs.astype(BF16)
    ys = []
    for h in range(R):
        decay = jnp.exp2(jnp.where(tri, acum[:, h:h + 1] - acum_t[h:h + 1, :], -jnp.inf))
        m = cb * decay.astype(BF16) * dt_t[h:h + 1, :]
        ys.append(jnp.dot(m, xb[:, h * P:(h + 1) * P], preferred_element_type=F32))
    new_states = lax.dot_general((xs * w_state).astype(BF16), bblk, TN_DIMS, preferred_element_type=F32)
    for s in range(nseq):
        last = (s + 1) * t_in - 1
        for h in range(R):
            blk = (slice(h * P, (h + 1) * P), slice(s * N, (s + 1) * N))
            st[blk] = s_prev[blk] * jnp.exp2(decay_t[h:h + 1, last:last + 1]) + new_states[blk]
    y = jnp.concatenate(ys, axis=1) + y_off + d_skip * xs
    y = y * _silu(z)
    y = y * lax.rsqrt(jnp.mean(y * y, axis=-1, keepdims=True) + EPS)
    y_o[...] = (y * gain).astype(y_o.dtype)


def _ssd(xbc, z, dt, w_conv, b_conv, a_pad, d_exp, gn_w, hist, init_state, nb, seq):
    G, N, P = SSM_GROUPS, D_STATE, SSM_HEAD_DIM
    inner = z.shape[1]
    gw = inner // G
    rg = gw // P
    L = SSM_CHUNK
    carried = init_state is not None
    if carried:
        t_in, nsub = seq, 1
        assert L % seq == 0 and nb % (L // seq) == 0, (nb, seq)
    else:
        t_in, nsub = L, 8 if seq % (8 * L) == 0 else 1
        assert seq % L == 0, seq
    nseq = L // t_in
    nbb = nb // nseq
    nbc = inner // N
    lb = L * nsub
    nc = max(seq // lb, 1)
    row = lambda b, g, c: b * nc + c

    def triple():
        return [pl.BlockSpec((lb, gw), lambda b, g, c: (row(b, g, c), g)),
                pl.BlockSpec((lb, N), lambda b, g, c: (row(b, g, c), nbc + g)),
                pl.BlockSpec((lb, N), lambda b, g, c: (row(b, g, c), nbc + G + g))]

    cw = gw + 2 * N

    def by_group(v):
        parts = [v[:, :inner].reshape(-1, G, gw), v[:, inner:inner + G * N].reshape(-1, G, N),
                 v[:, inner + G * N:].reshape(-1, G, N)]
        return jnp.transpose(jnp.concatenate(parts, axis=2), (1, 0, 2))

    zpad = lambda n: jnp.zeros((G, 1, n), F32)
    gconst = jnp.concatenate([
        by_group(w_conv.astype(F32)), by_group(b_conv.reshape(1, -1).astype(F32)),
        jnp.concatenate([d_exp.reshape(G, 1, gw), a_pad.reshape(G, 1, LANES), zpad(cw - gw - LANES)], axis=2),
        jnp.concatenate([gn_w.reshape(G, 1, gw), zpad(cw - gw)], axis=2),
        jnp.zeros((G, SUBLANES - SSM_CONV - 3, cw), F32)], axis=1)
    x_spec, b_spec, c_spec = triple()
    in_specs = [x_spec, pl.BlockSpec((lb, gw), lambda b, g, c: (row(b, g, c), g)), b_spec, c_spec,
                pl.BlockSpec((lb, LANES), lambda b, g, c: (row(b, g, c), 0)),
                pl.BlockSpec((None, SUBLANES, cw), lambda b, g, c: (g, 0, 0)),
                pl.BlockSpec((None, 3 * LANES, LANES), lambda b, g, c: (g, 0, 0)),
                pl.BlockSpec((3 * LANES, gw), lambda b, g, c: (0, 0))]
    lane_head = jnp.arange(3 * LANES) % LANES
    widen = (lane_head[:, None] == jnp.arange(gw)[None, :] // P).astype(BF16)
    lane = jnp.arange(LANES)
    pick = ((lane_head[None, :, None] == jnp.arange(G)[:, None, None] * rg + lane[None, None, :])
            & (lane[None, None, :] < rg)).astype(BF16)
    args = [xbc, z, xbc, xbc, dt, gconst, pick, widen]
    if carried:
        in_specs += triple() + [pl.BlockSpec((nseq, 1, gw, N), lambda b, g, c: (b, g, 0, 0))]
        args += [hist, hist, hist, init_state]
    y, st = pl.pallas_call(
        functools.partial(_ssd_body, t_in, nseq, nsub, carried),
        grid=(nbb, G, nc), in_specs=in_specs,
        out_specs=[pl.BlockSpec((lb, gw), lambda b, g, c: (row(b, g, c), g)),
                   pl.BlockSpec((nseq, 1, gw, N), lambda b, g, c: (b, g, 0, 0))],
        out_shape=[jax.ShapeDtypeStruct((nb * seq, inner), BF16), jax.ShapeDtypeStruct((nb, G, gw, N), F32)],
        scratch_shapes=[pltpu.VMEM((gw, nseq * N), F32), pltpu.VMEM((SUBLANES, gw + 2 * N), F32)],
        compiler_params=_params(3), name="ssd",
    )(*args)
    return y, st


def _ffn(x, norm_w, w_gate, w_up, w_down, layer):
    h = _rmsnorm_pair(x, norm_w, BF16)
    a = _mm_pair([h], [_W(w_gate, layer), _W(w_up, layer)], out_dtype=BF16, mode="swiglu", name="ffn_gate_up")
    return _mm_pair([a], [_W(w_down, layer)], res=x, name="ffn_down")


def kernel(x_prompt, x_sample, cache_k, cache_v, cache_logf, page_table, state_sconv, state_mconv, state_ssm, norm_mix, norm_ffn, norm_final, w_in_even, b_forget, w_sconv, w_out_even, w_in_ssm, w_mconv, b_mconv, dt_bias, a_log, d_skip, w_gnorm, w_out_ssm, w_gate, w_up, w_down):
    bp, seq, d = x_prompt.shape
    bs, dec_seq, _ = x_sample.shape
    mp, ms = bp * seq, bs * dec_seq
    nh, dh = ATT_HEADS, ATT_HEAD_DIM
    att = nh * dh
    sc_ch = state_sconv.shape[-1]
    x = (x_prompt.reshape(mp, d), x_sample.reshape(ms, d))

    i = 0
    h = _rmsnorm_pair(x, norm_mix[0], BF16)
    c0 = 3 * sc_ch
    w_in_t = jnp.swapaxes(w_in_even, 1, 2)
    win = lambda col0, n: _W(w_in_t, i, col0=col0, n=n, transposed=True)
    gcu = _mm_pair([h], [win(0, c0)], name="in_even_conv")
    q = _mm_pair([h], [win(c0, att)], out_dtype=BF16, scale=dh ** -0.5 * LOG2E, name="in_even_q")
    k, kb = _mm_pair([h], [win(c0 + att, att)], name="in_even_k", dup_bf16=True)
    v, vb = _mm_pair([h], [win(c0 + 2 * att, att)], name="in_even_v", dup_bf16=True)
    lf_p, fc_p, lf_s, fc_s = _forget_pair(h, w_in_even[i, :, c0 + 3 * att:], b_forget[i], bp, seq, dec_seq)
    logf_p = lf_p[:, :nh].reshape(bp, seq, nh)
    logf_s = lf_s[:, :nh].reshape(bs, dec_seq, nh)
    y_a, sconv_p, sconv_s = _sconv_pair(gcu, state_sconv[i], w_sconv[i], bp, seq, dec_seq)
    yb_p = _fox_prompt(q[0], kb[0], vb[0], fc_p[:, :nh].reshape(bp, seq, nh), bp, seq)
    yb_s = _fox_sample(q[1], k[1], v[1], fc_s[:, :nh].reshape(bs, dec_seq, nh),
                       cache_k[i], cache_v[i], cache_logf[i], page_table, dec_seq)
    x = _mm_pair([y_a, (yb_p, yb_s)],
                 [_W(w_out_even, i, row0=0, k=sc_ch), _W(w_out_even, i, row0=sc_ch, k=att)], res=x, name="out_even")
    x = _ffn(x, norm_ffn[0], w_gate, w_up, w_down, 0)

    j = 0
    G = SSM_GROUPS
    n_heads = a_log.shape[1]
    rg = n_heads // G
    inner = n_heads * SSM_HEAD_DIM
    conv_dim = state_mconv.shape[-1]
    h = _rmsnorm_pair(x, norm_mix[1], BF16)
    w_ssm_t = jnp.swapaxes(w_in_ssm, 1, 2)
    z = _mm_pair([h], [_W(w_ssm_t, j, n=inner, transposed=True)], name="in_ssm_z")
    xbc = _mm_pair([h], [_W(w_ssm_t, j, col0=inner, n=conv_dim, transposed=True)], name="in_ssm_xbc")
    assert n_heads <= LANES
    w_dt = jnp.zeros((d, LANES), BF16).at[:, :n_heads].set(w_in_ssm[j, :, inner + conv_dim:].astype(BF16))
    b_dt = jnp.zeros((1, LANES), F32).at[0, :n_heads].set(dt_bias[j].astype(F32))
    dt = _mm_pair([h], [_W(w_dt)], mode="softplus", bias=b_dt, name="in_ssm_dt")
    a_pad = jnp.zeros((G, LANES), F32).at[:, :rg].set(-jnp.exp(a_log[j].astype(F32)).reshape(G, rg)).reshape(1, G * LANES)
    d_exp = jnp.repeat(d_skip[j].astype(F32), SSM_HEAD_DIM).reshape(1, inner)
    gn_w = w_gnorm[j].astype(F32).reshape(1, inner)
    hw = SSM_CONV - 1
    mconv_p = xbc[0].reshape(bp, seq, conv_dim)[:, seq - hw:, :]
    mconv_s = xbc[1].reshape(bs, dec_seq, conv_dim)[:, dec_seq - hw:, :]
    yp, ssm_p = _ssd(xbc[0], z[0], dt[0], w_mconv[j], b_mconv[j], a_pad, d_exp, gn_w, None, None, bp, seq)
    init = state_ssm[j].astype(F32).reshape(bs, G, rg * SSM_HEAD_DIM, D_STATE)
    ysm, ssm_s = _ssd(xbc[1], z[1], dt[1], w_mconv[j], b_mconv[j], a_pad, d_exp, gn_w,
                      _pad_hist(state_mconv[j], dec_seq), init, bs, dec_seq)
    x = _mm_pair([(yp, ysm)], [_W(w_out_ssm, j)], res=x, name="out_ssm")
    x = _ffn(x, norm_ffn[1], w_gate, w_up, w_down, 1)

    y = _rmsnorm_pair(x, norm_final, F32)
    st_shape = (n_heads, SSM_HEAD_DIM, D_STATE)
    return (y[0].reshape(bp, seq, d), y[1].reshape(bs, dec_seq, d),
            k[0].reshape(1, bp, seq, nh, dh), v[0].reshape(1, bp, seq, nh, dh), logf_p[None],
            sconv_p[None], mconv_p[None], ssm_p.reshape((1, bp) + st_shape),
            k[1].reshape(1, bs, dec_seq, nh, dh), v[1].reshape(1, bs, dec_seq, nh, dh), logf_s[None],
            sconv_s[None], mconv_s[None], ssm_s.reshape((1, bs) + st_shape))
```
